```python
import math
import jax, jax.numpy as jnp
from jax import lax
import numpy as np

D_MODEL = 1024
BATCH = 8
SEQ = 2048
DEPTH = 4
DEC_BATCH = 128
DEC_SEQ = 1
PAST_LEN = 16384
PAGE_SIZE = 128

N_MIXERS = 4
N_GDN = len(range(0, DEPTH, N_MIXERS))
N_S5 = len(range(1, DEPTH, N_MIXERS))
N_GLA = len(range(2, DEPTH, N_MIXERS))
N_RWKV = len(range(3, DEPTH, N_MIXERS))

D_PLE = 256
D_FF = 4 * D_MODEL
LN_EPS = 1e-5
DEEPNORM_ALPHA = (2.0 * DEPTH) ** 0.25
DEEPNORM_BETA = (8.0 * DEPTH) ** -0.25

GDN_HEAD_DIM = 128
GDN_K_HEADS = D_MODEL // 128
GDN_V_HEADS = 2 * GDN_K_HEADS
GDN_QK = GDN_K_HEADS * GDN_HEAD_DIM
GDN_V = GDN_V_HEADS * GDN_HEAD_DIM
GDN_CONV_DIM = 2 * GDN_QK + GDN_V
GDN_IN = GDN_CONV_DIM + GDN_V + 2 * GDN_V_HEADS
GDN_CONV_W = 4
GDN_CHUNK = 64
S5_GROUP = 16
S5_GROUPS = D_MODEL // S5_GROUP
S5_STATE = 64
GLA_HEADS = 4
GLA_DK = D_MODEL // 2 // GLA_HEADS
GLA_DV = D_MODEL // GLA_HEADS
GLA_QK = GLA_HEADS * GLA_DK
GLA_V = GLA_HEADS * GLA_DV
GLA_GATE_RANK = 16
GLA_GATE_NORM = 16.0
GLA_IN = 2 * GLA_QK + 2 * GLA_V + GLA_GATE_RANK
GLA_CHUNK = 16
RWKV_HEAD_DIM = 64
RWKV_HEADS = D_MODEL // RWKV_HEAD_DIM
RWKV_DECAY_LORA = 64
RWKV_AAA_LORA = 64
RWKV_GATE_LORA = 128
RWKV_GN_EPS = 64e-5

kernel_name = "hybrid_gdn_s5_gla_rwkv7_decode_step"


def _l2norm(x, eps=1e-6):
    return x * lax.rsqrt(jnp.sum(x * x, -1, keepdims=True) + eps)


def _rms_norm(x, g, eps=1e-6):
    return x * lax.rsqrt(jnp.mean(x * x, -1, keepdims=True) + eps) * g


def _layer_norm(x, g, b):
    xf = x.astype(jnp.float32)
    xc = xf - jnp.mean(xf, -1, keepdims=True)
    var = jnp.mean(xc * xc, -1, keepdims=True)
    return (xc * lax.rsqrt(var + LN_EPS) * g + b).astype(x.dtype)


def _causal_conv(x, buf, w):
    xx = jnp.concatenate([buf.astype(x.dtype), x], axis=1)
    y = lax.conv_general_dilated(xx, w[:, None, :].astype(x.dtype), window_strides=(1,), padding="VALID",
                                 dimension_numbers=("NWC", "WIO", "NWC"), feature_group_count=x.shape[-1])
    return y, xx[:, xx.shape[1] - (w.shape[0] - 1):]


def _to_blocks(a, n, c):
    a = a.reshape((a.shape[0], n, c) + a.shape[2:])
    return jnp.moveaxis(a, 3, 1)


def _gdn_recurrence(q, k, v, g, beta, s0):
    bsz, t, h, dk = q.shape
    dv = v.shape[-1]
    c = math.gcd(t, GDN_CHUNK)
    n = t // c
    q, k, v, g, beta = (_to_blocks(a, n, c) for a in (q, k, v, g, beta))
    gc = jnp.cumsum(g, axis=-1)
    incl = jnp.tril(jnp.ones((c, c), dtype=bool))
    strict = jnp.tril(jnp.ones((c, c), dtype=bool), -1)
    diff = gc[..., :, None] - gc[..., None, :]
    decay = jnp.where(incl, jnp.exp(jnp.where(incl, diff, 0.0)), 0.0)
    kb = k * beta[..., None]
    m = jnp.where(strict, jnp.einsum("bhncd,bhnsd->bhncs", kb, k) * decay, 0.0)
    rhs = jnp.concatenate([v * beta[..., None], kb * jnp.exp(gc)[..., None]], axis=-1)
    sol = lax.linalg.triangular_solve(m + jnp.eye(c, dtype=m.dtype), rhs, left_side=True, lower=True,
                                      unit_diagonal=True)
    u, w = sol[..., :dv], sol[..., dv:]
    a_qk = jnp.einsum("bhncd,bhnsd->bhncs", q, k) * decay
    q_in = q * jnp.exp(gc)[..., None]
    k_out = k * jnp.exp(gc[..., -1:] - gc)[..., None]
    g_end = jnp.exp(gc[..., -1])

    def step(s, xs):
        qi, ki, ui, wi, ai, ge = xs
        v_new = ui - jnp.einsum("bhcd,bhde->bhce", wi, s)
        o = jnp.einsum("bhcd,bhde->bhce", qi, s) + jnp.einsum("bhcs,bhse->bhce", ai, v_new)
        s = s * ge[..., None, None] + jnp.einsum("bhcd,bhce->bhde", ki, v_new)
        return s, o

    xs = tuple(jnp.moveaxis(a, 2, 0) for a in (q_in, k_out, u, w, a_qk, g_end))
    s, o = lax.scan(step, s0, xs)
    o = jnp.transpose(o, (1, 0, 3, 2, 4)).reshape(bsz, t, h, dv)
    return o, s


def _gdn_mixer(x, conv_buf, s0, w_in, conv_w, a_log, dt_bias, norm_g, w_out):
    bsz, t, _ = x.shape
    xf = x.astype(jnp.float32)
    proj = xf @ w_in
    qkv, z, b_raw, a_raw = jnp.split(proj, [GDN_CONV_DIM, GDN_CONV_DIM + GDN_V,
                                            GDN_CONV_DIM + GDN_V + GDN_V_HEADS], axis=-1)
    qkv, new_buf = _causal_conv(qkv, conv_buf.astype(jnp.float32), conv_w.astype(jnp.float32))
    qkv = jax.nn.silu(qkv)
    q, k, v = jnp.split(qkv, [GDN_QK, 2 * GDN_QK], axis=-1)
    rep = GDN_V_HEADS // GDN_K_HEADS
    q = jnp.repeat(_l2norm(q.reshape(bsz, t, GDN_K_HEADS, GDN_HEAD_DIM)) * GDN_HEAD_DIM ** -0.5, rep, axis=2)
    k = jnp.repeat(_l2norm(k.reshape(bsz, t, GDN_K_HEADS, GDN_HEAD_DIM)), rep, axis=2)
    v = v.reshape(bsz, t, GDN_V_HEADS, GDN_HEAD_DIM)
    beta = jax.nn.sigmoid(b_raw)
    g = -jnp.exp(a_log) * jax.nn.softplus(a_raw + dt_bias)
    o, s = _gdn_recurrence(q, k, v, g, beta, s0.astype(jnp.float32))
    o = _rms_norm(o, norm_g) * jax.nn.silu(z.reshape(bsz, t, GDN_V_HEADS, GDN_HEAD_DIM))
    y = o.reshape(bsz, t, GDN_V) @ w_out
    return y.astype(x.dtype), new_buf.astype(x.dtype), s.astype(x.dtype)


def _complex_affine_combine(e1, e2):
    a1r, a1i, b1r, b1i = e1
    a2r, a2i, b2r, b2i = e2
    return (a1r * a2r - a1i * a2i, a1r * a2i + a1i * a2r,
            a2r * b1r - a2i * b1i + b2r, a2r * b1i + a2i * b1r + b2i)


def _s5_mixer(x, s_re0, s_im0, a_re, a_im, log_dt, b_re, b_im, c_re, c_im, d_skip, w_o, w_gate):
    bsz, t, d = x.shape
    xf = x.astype(jnp.float32)
    u = xf.reshape(bsz, t, S5_GROUPS, S5_GROUP)
    dt = jnp.exp(log_dt)[:, None]
    mag = jnp.exp(a_re * dt)
    lam_re, lam_im = mag * jnp.cos(a_im * dt), mag * jnp.sin(a_im * dt)
    den = a_re * a_re + a_im * a_im
    f_re = ((lam_re - 1.0) * a_re + lam_im * a_im) / den
    f_im = (lam_im * a_re - (lam_re - 1.0) * a_im) / den
    bb_re = f_re[..., None] * b_re - f_im[..., None] * b_im
    bb_im = f_re[..., None] * b_im + f_im[..., None] * b_re
    bu_re = jnp.einsum("gpc,btgc->btgp", bb_re, u)
    bu_im = jnp.einsum("gpc,btgc->btgp", bb_im, u)
    s_re0 = s_re0.astype(jnp.float32)
    s_im0 = s_im0.astype(jnp.float32)
    bu_re = bu_re.at[:, 0].add(lam_re * s_re0 - lam_im * s_im0)
    bu_im = bu_im.at[:, 0].add(lam_re * s_im0 + lam_im * s_re0)
    lr = jnp.broadcast_to(lam_re, bu_re.shape)
    li = jnp.broadcast_to(lam_im, bu_im.shape)
    _, _, h_re, h_im = lax.associative_scan(_complex_affine_combine, (lr, li, bu_re, bu_im), axis=1)
    y = jnp.einsum("gcp,btgp->btgc", c_re, h_re) - jnp.einsum("gcp,btgp->btgc", c_im, h_im)
    y = y.reshape(bsz, t, d) + d_skip * xf
    z = jax.nn.gelu(y)
    out = (z @ w_o) * jax.nn.sigmoid(z @ w_gate)
    return out.astype(x.dtype), h_re[:, -1].astype(x.dtype), h_im[:, -1].astype(x.dtype)


def _gla_recurrence(q, k, v, logd, s0):
    bsz, t, h, dk = q.shape
    dv = v.shape[-1]
    c = math.gcd(t, GLA_CHUNK)
    n = t // c
    q, k, v, logd = (_to_blocks(a, n, c) for a in (q, k, v, logd))
    bc = jnp.cumsum(logd, axis=3)
    q_in = q * jnp.exp(bc)
    k_rel = k * jnp.exp(-bc)
    incl = jnp.tril(jnp.ones((c, c), dtype=bool))
    a = jnp.where(incl, jnp.einsum("bhncd,bhnsd->bhncs", q_in, k_rel), 0.0)
    o_intra = jnp.einsum("bhncs,bhnse->bhnce", a, v)
    k_out = k * jnp.exp(bc[..., -1:, :] - bc)
    g_end = jnp.exp(bc[..., -1, :])

    def step(s, xs):
        qi, ki, vi, ge = xs
        o = jnp.einsum("bhcd,bhde->bhce", qi, s)
        s = s * ge[..., None] + jnp.einsum("bhcd,bhce->bhde", ki, vi)
        return s, o

    xs = tuple(jnp.moveaxis(z, 2, 0) for z in (q_in, k_out, v, g_end))
    s, o_inter = lax.scan(step, s0, xs)
    o = jnp.moveaxis(o_inter, 0, 2) + o_intra
    o = jnp.transpose(o, (0, 2, 3, 1, 4)).reshape(bsz, t, h, dv)
    return o, s


def _gla_mixer(x, s0, w_in, w_gk2, b_gk, norm_g, w_out):
    bsz, t, _ = x.shape
    xf = x.astype(jnp.float32)
    proj = xf @ w_in
    q, k, v, gate, gk = jnp.split(proj, [GLA_QK, 2 * GLA_QK, 2 * GLA_QK + GLA_V, 2 * GLA_QK + 2 * GLA_V], axis=-1)
    q = q.reshape(bsz, t, GLA_HEADS, GLA_DK) * GLA_DK ** -0.5
    k = k.reshape(bsz, t, GLA_HEADS, GLA_DK)
    v = v.reshape(bsz, t, GLA_HEADS, GLA_DV)
    logd = (jax.nn.log_sigmoid(gk @ w_gk2 + b_gk) / GLA_GATE_NORM).reshape(bsz, t, GLA_HEADS, GLA_DK)
    o, s = _gla_recurrence(q, k, v, logd, s0.astype(jnp.float32))
    o = _rms_norm(o, norm_g) * jax.nn.silu(gate.reshape(bsz, t, GLA_HEADS, GLA_DV))
    y = o.reshape(bsz, t, GLA_V) @ w_out
    return y.astype(x.dtype), s.astype(x.dtype)


def _rwkv7_mixer(x, shift0, s0, mu, w_rkv, w0, w_w1, w_w2, a0, w_a1, w_a2, w_g1, w_g2,
                 k_k, k_a, r_k, ln_g, ln_b, w_o):
    bsz, t, d = x.shape
    xf = x.astype(jnp.float32)
    x_prev = jnp.concatenate([shift0.astype(jnp.float32)[:, None], xf[:, :-1]], axis=1)
    xs = xf[None] + (x_prev - xf)[None] * mu[:, None, None, :]
    r, k, v = jnp.einsum("sbtd,sde->sbte", xs[:3], w_rkv)
    w_log = -jax.nn.softplus(-(w0 + jnp.tanh(xs[3] @ w_w1) @ w_w2)) - 0.5
    decay = jnp.exp(-jnp.exp(w_log))
    a = jax.nn.sigmoid(a0 + (xs[4] @ w_a1) @ w_a2)
    g = jax.nn.sigmoid(xs[5] @ w_g1) @ w_g2

    def heads(z):
        return z.reshape(bsz, t, RWKV_HEADS, RWKV_HEAD_DIM)

    kk = _l2norm(heads(k * k_k))
    k = k * (1.0 + (a - 1.0) * k_a)
    r, k, v, a, decay = heads(r), heads(k), heads(v), heads(a), heads(decay)

    def step(s, inp):
        ri, di, ki, vi, kki, ai = inp
        sa = jnp.einsum("bhvk,bhk->bhv", s, -kki)
        s = s * di[:, :, None, :] + sa[..., None] * (kki * ai)[:, :, None, :] + vi[..., None] * ki[:, :, None, :]
        return s, jnp.einsum("bhvk,bhk->bhv", s, ri)

    seq = tuple(jnp.moveaxis(z, 1, 0) for z in (r, decay, k, v, kk, a))
    s, y = lax.scan(step, s0.astype(jnp.float32), seq)
    y = jnp.moveaxis(y, 0, 1)
    yc = y - jnp.mean(y, -1, keepdims=True)
    y = yc * lax.rsqrt(jnp.mean(yc * yc, -1, keepdims=True) + RWKV_GN_EPS)
    y = y.reshape(bsz, t, d) * ln_g + ln_b
    bonus = jnp.sum(r * k * r_k, -1, keepdims=True) * v
    y = (y + bonus.reshape(bsz, t, d)) * g
    out = y @ w_o
    return out.astype(x.dtype), xf[:, -1].astype(x.dtype), s.astype(x.dtype)


def _sqrelu_mlp(x, w1, w2):
    h = jax.nn.relu(x @ w1)
    return (h * h) @ w2


def _trunk(x, p, states, gdn_w, s5_w, gla_w, rwkv_w, ln_mix_g, ln_mix_b, ln_ffn_g, ln_ffn_b,
           mlp_w1, mlp_w2, ple_w, ple_gate_w):
    gdn_conv, gdn_s, s5_re, s5_im, gla_s, rw_shift, rw_s = states
    new = ([], [], [], [], [], [], [])
    for i in range(DEPTH):
        kind, j = i % N_MIXERS, i // N_MIXERS
        if kind == 0:
            h, c_new, s_new = _gdn_mixer(x, gdn_conv[j], gdn_s[j], *[w[j] for w in gdn_w])
            new[0].append(c_new)
            new[1].append(s_new)
        elif kind == 1:
            h, re_new, im_new = _s5_mixer(x, s5_re[j], s5_im[j], *[w[j] for w in s5_w])
            new[2].append(re_new)
            new[3].append(im_new)
        elif kind == 2:
            h, s_new = _gla_mixer(x, gla_s[j], *[w[j] for w in gla_w])
            new[4].append(s_new)
        else:
            h, sh_new, s_new = _rwkv7_mixer(x, rw_shift[j], rw_s[j], *[w[j] for w in rwkv_w])
            new[5].append(sh_new)
            new[6].append(s_new)
        x = _layer_norm(DEEPNORM_ALPHA * x + h, ln_mix_g[i], ln_mix_b[i])
        x = _layer_norm(DEEPNORM_ALPHA * x + _sqrelu_mlp(x, mlp_w1[i], mlp_w2[i]), ln_ffn_g[i], ln_ffn_b[i])
        x = x + (p[i] @ ple_w[i]) * jax.nn.sigmoid(x @ ple_gate_w[i])
    return x, tuple(jnp.stack(lst) for lst in new)


def setup_inputs(seed: int = 0) -> dict:
    key = jax.random.key(seed)
    ks = iter(jax.random.split(key, 96))
    f32 = jnp.float32
    D = D_MODEL

    def nrm(shape, scale=1.0):
        return scale * jax.random.normal(next(ks), shape, f32)

    def unif(shape, lo, hi):
        return jax.random.uniform(next(ks), shape, f32, lo, hi)

    def gain(shape):
        return 1.0 + nrm(shape, 0.02)

    beta = DEEPNORM_BETA
    gdn_dt = jnp.exp(unif((N_GDN, GDN_V_HEADS), math.log(1e-3), math.log(1e-1)))
    ratio = jnp.arange(D, dtype=f32) / (D - 1)
    return {
        "x_prompt": nrm((BATCH, SEQ, D)),
        "x_sample": nrm((DEC_BATCH, DEC_SEQ, D)),
        "state_gdn_conv": nrm((N_GDN, DEC_BATCH, GDN_CONV_W - 1, GDN_CONV_DIM)),
        "state_gdn": nrm((N_GDN, DEC_BATCH, GDN_V_HEADS, GDN_HEAD_DIM, GDN_HEAD_DIM), 0.1),
        "state_s5_re": nrm((N_S5, DEC_BATCH, S5_GROUPS, S5_STATE), 0.1),
        "state_s5_im": nrm((N_S5, DEC_BATCH, S5_GROUPS, S5_STATE), 0.1),
        "state_gla": nrm((N_GLA, DEC_BATCH, GLA_HEADS, GLA_DK, GLA_DV), 1.0),
        "state_rwkv_shift": nrm((N_RWKV, DEC_BATCH, D)),
        "state_rwkv": nrm((N_RWKV, DEC_BATCH, RWKV_HEADS, RWKV_HEAD_DIM, RWKV_HEAD_DIM), 0.2),
        "p_prompt": nrm((DEPTH, BATCH, SEQ, D_PLE)),
        "p_sample": nrm((DEPTH, DEC_BATCH, DEC_SEQ, D_PLE)),
        "gdn_w_in": nrm((N_GDN, D, GDN_IN), D ** -0.5),
        "gdn_conv_w": nrm((N_GDN, GDN_CONV_W, GDN_CONV_DIM), GDN_CONV_W ** -0.5),
        "gdn_a_log": jnp.log(unif((N_GDN, GDN_V_HEADS), 1.0, 16.0)),
        "gdn_dt_bias": gdn_dt + jnp.log(-jnp.expm1(-gdn_dt)),
        "gdn_norm_g": gain((N_GDN, GDN_HEAD_DIM)),
        "gdn_w_out": nrm((N_GDN, GDN_V, D), beta * GDN_V ** -0.5),
        "s5_a_re": -0.5 + nrm((N_S5, S5_GROUPS, S5_STATE), 0.01),
        "s5_a_im": math.pi * jnp.arange(S5_STATE, dtype=f32) + nrm((N_S5, S5_GROUPS, S5_STATE), 0.01),
        "s5_log_dt": unif((N_S5, S5_GROUPS), math.log(1e-3), math.log(1e-1)),
        "s5_b_re": nrm((N_S5, S5_GROUPS, S5_STATE, S5_GROUP), (2.0 * S5_GROUP) ** -0.5),
        "s5_b_im": nrm((N_S5, S5_GROUPS, S5_STATE, S5_GROUP), (2.0 * S5_GROUP) ** -0.5),
        "s5_c_re": nrm((N_S5, S5_GROUPS, S5_GROUP, S5_STATE), 0.7),
        "s5_c_im": nrm((N_S5, S5_GROUPS, S5_GROUP, S5_STATE), 0.7),
        "s5_d": nrm((N_S5, D)),
        "s5_w_o": nrm((N_S5, D, D), beta * D ** -0.5),
        "s5_w_gate": nrm((N_S5, D, D), D ** -0.5),
        "gla_w_in": nrm((N_GLA, D, GLA_IN), D ** -0.5),
        "gla_w_gk2": nrm((N_GLA, GLA_GATE_RANK, GLA_QK), GLA_GATE_RANK ** -0.5),
        "gla_b_gk": nrm((N_GLA, GLA_QK), 0.1),
        "gla_norm_g": gain((N_GLA, GLA_DV)),
        "gla_w_out": nrm((N_GLA, GLA_V, D), beta * GLA_V ** -0.5),
        "rwkv_mu": unif((N_RWKV, 6, D), 0.0, 1.0),
        "rwkv_w_rkv": nrm((N_RWKV, 3, D, D), D ** -0.5),
        "rwkv_w0": -5.5 + 5.0 * ratio ** 0.85 + nrm((N_RWKV, D), 0.05),
        "rwkv_w_w1": nrm((N_RWKV, D, RWKV_DECAY_LORA), D ** -0.5),
        "rwkv_w_w2": nrm((N_RWKV, RWKV_DECAY_LORA, D), 0.1 * RWKV_DECAY_LORA ** -0.5),
        "rwkv_a0": nrm((N_RWKV, D), 0.1),
        "rwkv_w_a1": nrm((N_RWKV, D, RWKV_AAA_LORA), D ** -0.5),
        "rwkv_w_a2": nrm((N_RWKV, RWKV_AAA_LORA, D), 0.5 * RWKV_AAA_LORA ** -0.5),
        "rwkv_w_g1": nrm((N_RWKV, D, RWKV_GATE_LORA), D ** -0.5),
        "rwkv_w_g2": nrm((N_RWKV, RWKV_GATE_LORA, D), RWKV_GATE_LORA ** -0.5),
        "rwkv_k_k": 0.85 + nrm((N_RWKV, D), 0.02),
        "rwkv_k_a": gain((N_RWKV, D)),
        "rwkv_r_k": nrm((N_RWKV, RWKV_HEADS, RWKV_HEAD_DIM), 0.1),
        "rwkv_ln_g": gain((N_RWKV, D)),
        "rwkv_ln_b": nrm((N_RWKV, D), 0.02),
        "rwkv_w_o": nrm((N_RWKV, D, D), beta * D ** -0.5),
        "ln_mix_g": gain((DEPTH, D)),
        "ln_mix_b": nrm((DEPTH, D), 0.02),
        "ln_ffn_g": gain((DEPTH, D)),
        "ln_ffn_b": nrm((DEPTH, D), 0.02),
        "mlp_w1": nrm((DEPTH, D, D_FF), D ** -0.5),
        "mlp_w2": nrm((DEPTH, D_FF, D), beta * D_FF ** -0.5),
        "ple_w": nrm((DEPTH, D_PLE, D), beta * D_PLE ** -0.5),
        "ple_gate_w": nrm((DEPTH, D, D), D ** -0.5),
    }


def reference(x_prompt, x_sample, state_gdn_conv, state_gdn, state_s5_re, state_s5_im, state_gla,
              state_rwkv_shift, state_rwkv, p_prompt, p_sample,
              gdn_w_in, gdn_conv_w, gdn_a_log, gdn_dt_bias, gdn_norm_g, gdn_w_out,
              s5_a_re, s5_a_im, s5_log_dt, s5_b_re, s5_b_im, s5_c_re, s5_c_im, s5_d, s5_w_o, s5_w_gate,
              gla_w_in, gla_w_gk2, gla_b_gk, gla_norm_g, gla_w_out,
              rwkv_mu, rwkv_w_rkv, rwkv_w0, rwkv_w_w1, rwkv_w_w2, rwkv_a0, rwkv_w_a1, rwkv_w_a2,
              rwkv_w_g1, rwkv_w_g2, rwkv_k_k, rwkv_k_a, rwkv_r_k, rwkv_ln_g, rwkv_ln_b, rwkv_w_o,
              ln_mix_g, ln_mix_b, ln_ffn_g, ln_ffn_b, mlp_w1, mlp_w2, ple_w, ple_gate_w):
    gdn_w = (gdn_w_in, gdn_conv_w, gdn_a_log, gdn_dt_bias, gdn_norm_g, gdn_w_out)
    s5_w = (s5_a_re, s5_a_im, s5_log_dt, s5_b_re, s5_b_im, s5_c_re, s5_c_im, s5_d, s5_w_o, s5_w_gate)
    gla_w = (gla_w_in, gla_w_gk2, gla_b_gk, gla_norm_g, gla_w_out)
    rwkv_w = (rwkv_mu, rwkv_w_rkv, rwkv_w0, rwkv_w_w1, rwkv_w_w2, rwkv_a0, rwkv_w_a1, rwkv_w_a2,
              rwkv_w_g1, rwkv_w_g2, rwkv_k_k, rwkv_k_a, rwkv_r_k, rwkv_ln_g, rwkv_ln_b, rwkv_w_o)
    sample_states = (state_gdn_conv, state_gdn, state_s5_re, state_s5_im, state_gla,
                     state_rwkv_shift, state_rwkv)
    b = x_prompt.shape[0]
    prompt_states = tuple(jnp.zeros((st.shape[0], b) + st.shape[2:], x_prompt.dtype) for st in sample_states)
    y_prompt, (gc_p, gs_p, sr_p, si_p, la_p, sh_p, rs_p) = _trunk(
        x_prompt, p_prompt, prompt_states, gdn_w, s5_w, gla_w, rwkv_w,
        ln_mix_g, ln_mix_b, ln_ffn_g, ln_ffn_b, mlp_w1, mlp_w2, ple_w, ple_gate_w)
    y_sample, (gc_s, gs_s, sr_s, si_s, la_s, sh_s, rs_s) = _trunk(
        x_sample, p_sample, sample_states, gdn_w, s5_w, gla_w, rwkv_w,
        ln_mix_g, ln_mix_b, ln_ffn_g, ln_ffn_b, mlp_w1, mlp_w2, ple_w, ple_gate_w)
    return (y_prompt, y_sample, gc_p, gc_s, gs_p, gs_s, sr_p, sr_s, si_p, si_s, la_p, la_s, sh_p, sh_s, rs_p, rs_s)
```

```python
import functools
import math

import jax
import jax.numpy as jnp
from jax import lax
from jax.experimental import pallas as pl
from jax.experimental.pallas import tpu as pltpu

F32 = jnp.float32
BF16 = jnp.bfloat16

D_MODEL = 1024
DEPTH = 4
D_PLE = 256
D_FF = 4 * D_MODEL
LN_EPS = 1e-5
ALPHA = (2.0 * DEPTH) ** 0.25

GDN_HD = 128
GDN_KH = 8
GDN_VH = 16
GDN_QK = GDN_KH * GDN_HD
GDN_V = GDN_VH * GDN_HD
GDN_CONV_DIM = 2 * GDN_QK + GDN_V
GDN_CONV_W = 4
GDN_CHUNK = 64

S5_GROUP = 16
S5_GROUPS = D_MODEL // S5_GROUP
S5_STATE = 64
S5_H = S5_GROUPS * S5_STATE

GLA_HEADS = 4
GLA_DK = 128
GLA_DV = 256
GLA_QK = GLA_HEADS * GLA_DK
GLA_V = GLA_HEADS * GLA_DV
GLA_RANK = 16
GLA_GATE_NORM = 16.0
GLA_CHUNK = 16

RW_HD = 64
RW_H = D_MODEL // RW_HD
RW_GN_EPS = 64e-5
RW_CHUNK = 64

LANES = 128
SUBLANES = 8
VMEM_LIMIT = 56 * 1024 * 1024


def _cp(sem, vmem=VMEM_LIMIT):
    return pltpu.CompilerParams(dimension_semantics=sem, vmem_limit_bytes=vmem)


def _dot(a, b):
    return jnp.dot(a, b, preferred_element_type=F32)


def _dot_nt(a, b):
    return lax.dot_general(a, b, (((1,), (1,)), ((), ())), preferred_element_type=F32)


def _dot_tn(a, b):
    return lax.dot_general(a, b, (((0,), (0,)), ((), ())), preferred_element_type=F32)


def _dot3(l, x):
    hi = x.astype(BF16).astype(F32)
    r1 = x - hi
    mid = r1.astype(BF16).astype(F32)
    lo = r1 - mid
    return _dot(l, hi) + _dot(l, mid) + _dot(l, lo)


def _softplus(x):
    return jnp.maximum(x, 0.0) + jnp.log(1.0 + jnp.exp(-jnp.abs(x)))


def _sigmoid(x):
    return 1.0 / (1.0 + jnp.exp(-x))


def _silu(x):
    return x * _sigmoid(x)


def _ln(x, g, b):
    xc = x - jnp.mean(x, -1, keepdims=True)
    var = jnp.mean(xc * xc, -1, keepdims=True)
    return xc * lax.rsqrt(var + LN_EPS) * g + b


def _iota2(shape, axis):
    return lax.broadcasted_iota(jnp.int32, shape, axis)


def _tri(n, strict=False, block=None):
    r = _iota2((n, n), 0)
    c = _iota2((n, n), 1)
    m = (r > c) if strict else (r >= c)
    if block is not None:
        m = m & ((r // block) == (c // block))
    return m


def _full_spec(a):
    n = a.ndim
    return pl.BlockSpec(a.shape, lambda *_: (0,) * n)


def _mm(x, w, n_cols, tm, tn, name):
    m, k = x.shape

    def body(x_ref, w_ref, o_ref):
        o_ref[...] = _dot(x_ref[...], w_ref[...])

    return pl.pallas_call(
        body, grid=(m // tm, n_cols // tn),
        in_specs=[pl.BlockSpec((tm, k), lambda i, j: (i, 0)), pl.BlockSpec((k, tn), lambda i, j: (0, j))],
        out_specs=pl.BlockSpec((tm, tn), lambda i, j: (i, j)),
        out_shape=jax.ShapeDtypeStruct((m, n_cols), F32),
        compiler_params=_cp(("parallel", "arbitrary")), name=name)(x, w)


def _rowwise(body, rows, fulls, out_cols, tm, name):
    m = rows[0].shape[0]
    in_specs = [pl.BlockSpec((tm, r.shape[1]), lambda i: (i, 0)) for r in rows] + [_full_spec(f) for f in fulls]
    out_specs = [pl.BlockSpec((tm, c), lambda i: (i, 0)) for c in out_cols]
    out_shape = [jax.ShapeDtypeStruct((m, c), F32) for c in out_cols]
    res = pl.pallas_call(body, grid=(m // tm,), in_specs=in_specs, out_specs=out_specs, out_shape=out_shape,
                         compiler_params=_cp(("parallel",)), name=name)(*rows, *fulls)
    return res


def _out_ln(x, o, w_out, g, b, tm, name):
    def body(x_ref, o_ref, w_ref, g_ref, b_ref, y_ref):
        y_ref[...] = _ln(ALPHA * x_ref[...] + _dot(o_ref[...], w_ref[...]), g_ref[...], b_ref[...])

    return _rowwise(body, [x, o], [w_out, g.reshape(1, -1), b.reshape(1, -1)], [D_MODEL], tm, name)[0]


def _mlp_ple(x1, p, w1, w2, g, b, ple_w, gate_w, tm, tf, name):
    m = x1.shape[0]
    nf = D_FF // tf

    def body(x_ref, p_ref, w1_ref, w2_ref, g_ref, b_ref, pw_ref, gw_ref, o_ref, acc_ref):
        f = pl.program_id(1)
        x = x_ref[...]
        h = jnp.maximum(_dot(x, w1_ref[...]), 0.0)
        c = _dot(h * h, w2_ref[...])

        @pl.when(f == 0)
        def _():
            acc_ref[...] = c

        @pl.when(f > 0)
        def _():
            acc_ref[...] += c

        @pl.when(f == nf - 1)
        def _():
            x2 = _ln(ALPHA * x + acc_ref[...], g_ref[...], b_ref[...])
            o_ref[...] = x2 + _dot(p_ref[...], pw_ref[...]) * _sigmoid(_dot(x2, gw_ref[...]))

    return pl.pallas_call(
        body, grid=(m // tm, nf),
        in_specs=[pl.BlockSpec((tm, D_MODEL), lambda i, f: (i, 0)),
                  pl.BlockSpec((tm, D_PLE), lambda i, f: (i, 0)),
                  pl.BlockSpec((D_MODEL, tf), lambda i, f: (0, f)),
                  pl.BlockSpec((tf, D_MODEL), lambda i, f: (f, 0)),
                  pl.BlockSpec((1, D_MODEL), lambda i, f: (0, 0)),
                  pl.BlockSpec((1, D_MODEL), lambda i, f: (0, 0)),
                  pl.BlockSpec((D_PLE, D_MODEL), lambda i, f: (0, 0)),
                  pl.BlockSpec((D_MODEL, D_MODEL), lambda i, f: (0, 0))],
        out_specs=pl.BlockSpec((tm, D_MODEL), lambda i, f: (i, 0)),
        out_shape=jax.ShapeDtypeStruct((m, D_MODEL), F32),
        scratch_shapes=[pltpu.VMEM((tm, D_MODEL), F32)],
        compiler_params=_cp(("parallel", "arbitrary")), name=name,
    )(x1, p, w1, w2, g.reshape(1, -1), b.reshape(1, -1), ple_w, gate_w)


def _gdn_gates(x, w_in, a_log, dt_bias, tm, name):
    w_ba = jnp.pad(w_in[:, GDN_CONV_DIM + GDN_V:], ((0, 0), (0, LANES - 2 * GDN_VH)))
    pad = lambda v: jnp.pad(v.reshape(1, -1), ((0, 0), (GDN_VH, LANES - 2 * GDN_VH)))

    def body(x_ref, w_ref, al_ref, dt_ref, o_ref):
        y = _dot(x_ref[...], w_ref[...])
        lane = _iota2(y.shape, 1)
        g = -jnp.exp(al_ref[...]) * _softplus(y + dt_ref[...])
        o_ref[...] = jnp.where(lane < GDN_VH, _sigmoid(y), g)

    return _rowwise(body, [x], [w_ba, pad(a_log), pad(dt_bias)], [LANES], tm, name)[0]


def _l2n(x, scale):
    return x * (lax.rsqrt(jnp.sum(x * x, -1, keepdims=True) + 1e-6) * scale)


def _gdn_act(y, o_ref):
    y = _silu(y)
    for h in range(GDN_KH):
        sl = slice(h * GDN_HD, (h + 1) * GDN_HD)
        o_ref[:, sl] = _l2n(y[:, sl], GDN_HD ** -0.5)
        sl = slice(GDN_QK + h * GDN_HD, GDN_QK + (h + 1) * GDN_HD)
        o_ref[:, sl] = _l2n(y[:, sl], 1.0)
    o_ref[:, 2 * GDN_QK:] = y[:, 2 * GDN_QK:]


def _gdn_conv_prompt(proj, conv_w, bsz, t, tt, name):
    nt = t // tt
    c = GDN_CONV_DIM

    def body(x_ref, w_ref, o_ref, tail_ref, carry_ref):
        i = pl.program_id(1)

        @pl.when(i == 0)
        def _():
            carry_ref[...] = jnp.zeros_like(carry_ref)

        x = x_ref[...]
        w = w_ref[...]
        y = x * w[3:4]
        x8 = x[:SUBLANES]
        c8 = carry_ref[...]
        row8 = _iota2(x8.shape, 0)
        y8 = x8 * w[3:4]
        for j in range(1, GDN_CONV_W):
            wj = w[3 - j:4 - j]
            y = y + pltpu.roll(x, j, 0) * wj
            y8 = y8 + jnp.where(row8 < j, pltpu.roll(c8, j, 0), pltpu.roll(x8, j, 0)) * wj
        _gdn_act(y, o_ref)
        _gdn_act(y8, o_ref.at[pl.ds(0, SUBLANES)])
        carry_ref[...] = x[tt - SUBLANES:]
        tail_ref[0] = x[tt - SUBLANES:]

    return pl.pallas_call(
        body, grid=(bsz, nt),
        in_specs=[pl.BlockSpec((tt, c), lambda b, i: (b * nt + i, 0)), _full_spec(conv_w)],
        out_specs=[pl.BlockSpec((tt, c), lambda b, i: (b * nt + i, 0)),
                   pl.BlockSpec((1, SUBLANES, c), lambda b, i: (b, 0, 0))],
        out_shape=[jax.ShapeDtypeStruct((bsz * t, c), F32), jax.ShapeDtypeStruct((bsz, SUBLANES, c), F32)],
        scratch_shapes=[pltpu.VMEM((SUBLANES, c), F32)],
        compiler_params=_cp(("parallel", "arbitrary")), name=name)(proj, conv_w)


def _neumann_inv(m, n):
    eye = (_iota2((n, n), 0) == _iota2((n, n), 1)).astype(F32)
    t = eye - m
    p = m
    k = 2
    while k < n:
        p = _dot(p, p)
        t = t + _dot(t, p)
        k *= 2
    return t


def _gdn_chunks(qkv, bg, proj, norm_g, bsz, t, name):
    c = GDN_CHUNK
    nc = t // c
    z_blk = GDN_CONV_DIM // GDN_V

    def body(q_ref, k_ref, v_ref, bg_ref, z_ref, ng_ref, o_ref, s_out_ref, s_ref):
        i = pl.program_id(1)

        @pl.when(i == 0)
        def _():
            s_ref[...] = jnp.zeros_like(s_ref)

        bgv = bg_ref[...]
        ltri = _tri(c).astype(F32)
        gc = _dot3(ltri, bgv)
        gct = jnp.concatenate([gc, jnp.zeros_like(gc)], axis=0).T
        incl = _tri(c)
        strict = _tri(c, strict=True)
        ng = ng_ref[...]
        for h in range(GDN_VH):
            kh = h // (GDN_VH // GDN_KH)
            qh = q_ref[:, kh * GDN_HD:(kh + 1) * GDN_HD]
            kk = k_ref[:, kh * GDN_HD:(kh + 1) * GDN_HD]
            vh = v_ref[:, h * GDN_HD:(h + 1) * GDN_HD]
            beta = bgv[:, h:h + 1]
            gcol = gc[:, GDN_VH + h:GDN_VH + h + 1]
            grow = gct[GDN_VH + h:GDN_VH + h + 1, :c]
            diff = gcol - grow
            decay = jnp.where(incl, jnp.exp(jnp.where(incl, diff, 0.0)), 0.0)
            kb = kk * beta
            m = jnp.where(strict, _dot_nt(kb, kk) * decay, 0.0)
            tinv = _neumann_inv(m, c)
            egc = jnp.exp(gcol)
            sol = _dot(tinv, jnp.concatenate([vh * beta, kb * egc], axis=1))
            u, w = sol[:, :GDN_HD], sol[:, GDN_HD:]
            aqk = _dot_nt(qh, kk) * decay
            glast = gc[c - 1:c, GDN_VH + h:GDN_VH + h + 1]
            s = s_ref[h]
            v_new = u - _dot(w, s)
            o = _dot(qh * egc, s) + _dot(aqk, v_new)
            s_ref[h] = s * jnp.exp(glast) + _dot_tn(kk * jnp.exp(glast - gcol), v_new)
            o = o * lax.rsqrt(jnp.mean(o * o, -1, keepdims=True) + 1e-6) * ng
            o_ref[:, h * GDN_HD:(h + 1) * GDN_HD] = o * _silu(z_ref[:, h * GDN_HD:(h + 1) * GDN_HD])

        @pl.when(i == nc - 1)
        def _():
            s_out_ref[0] = s_ref[...]

    return pl.pallas_call(
        body, grid=(bsz, nc),
        in_specs=[pl.BlockSpec((c, GDN_QK), lambda b, i: (b * nc + i, 0)),
                  pl.BlockSpec((c, GDN_QK), lambda b, i: (b * nc + i, 1)),
                  pl.BlockSpec((c, GDN_V), lambda b, i: (b * nc + i, 1)),
                  pl.BlockSpec((c, LANES), lambda b, i: (b * nc + i, 0)),
                  pl.BlockSpec((c, GDN_V), lambda b, i: (b * nc + i, z_blk)),
                  pl.BlockSpec((1, GDN_HD), lambda b, i: (0, 0))],
        out_specs=[pl.BlockSpec((c, GDN_V), lambda b, i: (b * nc + i, 0)),
                   pl.BlockSpec((1, GDN_VH, GDN_HD, GDN_HD), lambda b, i: (b, 0, 0, 0))],
        out_shape=[jax.ShapeDtypeStruct((bsz * t, GDN_V), F32),
                   jax.ShapeDtypeStruct((bsz, GDN_VH, GDN_HD, GDN_HD), F32)],
        scratch_shapes=[pltpu.VMEM((GDN_VH, GDN_HD, GDN_HD), F32)],
        compiler_params=_cp(("parallel", "arbitrary")), name=name,
    )(qkv, qkv, qkv, bg, proj, norm_g.reshape(1, -1))


def _gdn_prompt(x, bsz, t, w_in, conv_w, a_log, dt_bias, norm_g, w_out, ln_g, ln_b):
    m = bsz * t
    tm = min(m, 512)
    proj = _mm(x, w_in, GDN_CONV_DIM + GDN_V, tm, 512, "gdn_proj")
    bg = _gdn_gates(x, w_in, a_log, dt_bias, tm, "gdn_gates")
    qkv, tail = _gdn_conv_prompt(proj, conv_w, bsz, t, min(t, 256), "gdn_conv")
    o, s = _gdn_chunks(qkv, bg, proj, norm_g, bsz, t, "gdn_chunks")
    x1 = _out_ln(x, o, w_out, ln_g, ln_b, tm, "gdn_out")
    return x1, tail[:, SUBLANES - (GDN_CONV_W - 1):], s


def _s5_discretize(a_re, a_im, log_dt, b_re, b_im):
    g, p = a_re.shape
    bt_re = jnp.swapaxes(b_re, 1, 2)
    bt_im = jnp.swapaxes(b_im, 1, 2)

    def body(ar_ref, ai_ref, ldt_ref, br_ref, bi_ref, lr_ref, li_ref, bbr_ref, bbi_ref):
        ar, ai = ar_ref[...], ai_ref[...]
        dt = jnp.exp(ldt_ref[...])
        mag = jnp.exp(ar * dt)
        lr, li = mag * jnp.cos(ai * dt), mag * jnp.sin(ai * dt)
        den = ar * ar + ai * ai
        f_re = ((lr - 1.0) * ar + li * ai) / den
        f_im = (li * ar - (lr - 1.0) * ai) / den
        lr_ref[...] = lr
        li_ref[...] = li
        br, bi = br_ref[...], bi_ref[...]
        fr, fi = f_re[:, None, :], f_im[:, None, :]
        bbr_ref[...] = fr * br - fi * bi
        bbi_ref[...] = fr * bi + fi * br

    args = (a_re, a_im, log_dt.reshape(g, 1), bt_re, bt_im)
    return pl.pallas_call(
        body, grid=(1,), in_specs=[_full_spec(a) for a in args],
        out_specs=[pl.BlockSpec((g, p), lambda i: (0, 0))] * 2 + [pl.BlockSpec(bt_re.shape, lambda i: (0, 0, 0))] * 2,
        out_shape=[jax.ShapeDtypeStruct((g, p), F32)] * 2 + [jax.ShapeDtypeStruct(bt_re.shape, F32)] * 2,
        name="s5_discretize")(*args)


def _blockdiag(a, per):
    g, r, c = a.shape
    a4 = a.reshape(g // per, per, r, c)
    eye = jnp.eye(per, dtype=a.dtype)
    return jnp.einsum("jgrc,gh->jgrhc", a4, eye).reshape(g // per, per * r, per * c)


S5_GPT = LANES // S5_GROUP
S5_NT = S5_GROUPS // S5_GPT
S5_HT = S5_GPT * S5_STATE


def _s5_weights(a_re, a_im, log_dt, b_re, b_im, c_re, c_im):
    lr, li, bbr, bbi = _s5_discretize(a_re, a_im, log_dt, b_re, b_im)
    bcat = jnp.concatenate([_blockdiag(bbr, S5_GPT), _blockdiag(bbi, S5_GPT)], axis=2)
    ccat = jnp.concatenate([_blockdiag(jnp.swapaxes(c_re, 1, 2), S5_GPT),
                            -_blockdiag(jnp.swapaxes(c_im, 1, 2), S5_GPT)], axis=1)
    return lr.reshape(1, S5_H), li.reshape(1, S5_H), bcat, ccat


def _gelu(y):
    return 0.5 * y * (1.0 + jnp.tanh(math.sqrt(2.0 / math.pi) * (y + 0.044715 * (y * y * y))))


def _s5_scan(x3, s_re0, s_im0, lr, li, bcat, ccat, d_skip, tc, name):
    bsz, t, d = x3.shape
    nt = t // tc
    rows = bsz * tc
    tpx = S5_HT // LANES
    nlt = S5_H // LANES
    tq = 8

    def body(x_ref, sr0_ref, si0_ref, lr_ref, li_ref, b_ref, c_ref, d_ref, z_ref, sr_ref, si_ref, hre, him):
        i = pl.program_id(0)

        @pl.when(i == 0)
        def _():
            sr_ref[...] = sr0_ref[...]
            si_ref[...] = si0_ref[...]

        x = x_ref[...].reshape(rows, d)
        for j in range(S5_NT):
            bu = _dot(x[:, j * LANES:(j + 1) * LANES], b_ref[j])
            for k in range(tpx):
                hre[j * tpx + k] = bu[:, k * LANES:(k + 1) * LANES]
                him[j * tpx + k] = bu[:, S5_HT + k * LANES:S5_HT + (k + 1) * LANES]

        for q in range(nlt // tq):
            tiles = range(q * tq, (q + 1) * tq)
            lam_r = [jnp.broadcast_to(lr_ref[:, k * LANES:(k + 1) * LANES], (bsz, LANES)) for k in tiles]
            lam_i = [jnp.broadcast_to(li_ref[:, k * LANES:(k + 1) * LANES], (bsz, LANES)) for k in tiles]

            def step(tt, carry):
                idx = pl.ds(tt, bsz, stride=tc)
                out = []
                for n, k in enumerate(tiles):
                    sr, si = carry[2 * n], carry[2 * n + 1]
                    nr = lam_r[n] * sr - lam_i[n] * si + hre.at[k][idx, :]
                    ni = lam_r[n] * si + lam_i[n] * sr + him.at[k][idx, :]
                    hre.at[k][idx, :] = nr
                    him.at[k][idx, :] = ni
                    out += [nr, ni]
                return tuple(out)

            init = []
            for k in tiles:
                init += [sr_ref[:, k * LANES:(k + 1) * LANES], si_ref[:, k * LANES:(k + 1) * LANES]]
            fin = lax.fori_loop(0, tc, step, tuple(init))
            for n, k in enumerate(tiles):
                sr_ref[:, k * LANES:(k + 1) * LANES] = fin[2 * n]
                si_ref[:, k * LANES:(k + 1) * LANES] = fin[2 * n + 1]

        dsk = d_ref[...]
        for j in range(S5_NT):
            cj = c_ref[j]
            h_re = jnp.concatenate([hre[j * tpx + k] for k in range(tpx)], axis=1)
            h_im = jnp.concatenate([him[j * tpx + k] for k in range(tpx)], axis=1)
            y = _dot(h_re, cj[:S5_HT]) + _dot(h_im, cj[S5_HT:])
            xs = x[:, j * LANES:(j + 1) * LANES]
            z = _gelu(y + dsk[:, j * LANES:(j + 1) * LANES] * xs)
            z_ref[:, :, j * LANES:(j + 1) * LANES] = z.reshape(bsz, tc, LANES)

    fulls = (s_re0, s_im0, lr, li, bcat, ccat, d_skip.reshape(1, d))
    return pl.pallas_call(
        body, grid=(nt,),
        in_specs=[pl.BlockSpec((bsz, tc, d), lambda i: (0, i, 0))] + [_full_spec(a) for a in fulls],
        out_specs=[pl.BlockSpec((bsz, tc, d), lambda i: (0, i, 0)),
                   pl.BlockSpec((bsz, S5_H), lambda i: (0, 0)), pl.BlockSpec((bsz, S5_H), lambda i: (0, 0))],
        out_shape=[jax.ShapeDtypeStruct((bsz, t, d), F32),
                   jax.ShapeDtypeStruct((bsz, S5_H), F32), jax.ShapeDtypeStruct((bsz, S5_H), F32)],
        scratch_shapes=[pltpu.VMEM((nlt, rows, LANES), F32), pltpu.VMEM((nlt, rows, LANES), F32)],
        compiler_params=_cp(("arbitrary",)), name=name)(x3, *fulls)


def _s5_out(x, z, w_o, w_gate, g, b, tm, name):
    def body(x_ref, z_ref, wo_ref, wg_ref, g_ref, b_ref, y_ref):
        z = z_ref[...]
        h = _dot(z, wo_ref[...]) * _sigmoid(_dot(z, wg_ref[...]))
        y_ref[...] = _ln(ALPHA * x_ref[...] + h, g_ref[...], b_ref[...])

    return _rowwise(body, [x, z], [w_o, w_gate, g.reshape(1, -1), b.reshape(1, -1)], [D_MODEL], tm, name)[0]


def _s5_prompt(x, bsz, t, a_re, a_im, log_dt, b_re, b_im, c_re, c_im, d_skip, w_o, w_gate, ln_g, ln_b):
    lr, li, bcat, ccat = _s5_weights(a_re, a_im, log_dt, b_re, b_im, c_re, c_im)
    zero = jnp.zeros((bsz, S5_H), F32)
    z, s_re, s_im = _s5_scan(x.reshape(bsz, t, D_MODEL), zero, zero, lr, li, bcat, ccat, d_skip, min(t, 64), "s5_scan")
    x1 = _s5_out(x, z.reshape(bsz * t, D_MODEL), w_o, w_gate, ln_g, ln_b, min(bsz * t, 512), "s5_out")
    return x1, s_re.reshape(bsz, S5_GROUPS, S5_STATE), s_im.reshape(bsz, S5_GROUPS, S5_STATE)


GLA_PROJ = 2 * GLA_QK + 2 * GLA_V


def _gla_logd(x, w_in, w_gk2, b_gk, tm, name):
    w1 = jnp.pad(w_in[:, GLA_PROJ:], ((0, 0), (0, LANES - GLA_RANK)))
    w2 = jnp.pad(w_gk2, ((0, LANES - GLA_RANK), (0, 0)))

    def body(x_ref, w1_ref, w2_ref, b_ref, o_ref):
        y = _dot(_dot(x_ref[...], w1_ref[...]), w2_ref[...]) + b_ref[...]
        o_ref[...] = -_softplus(-y) * (1.0 / GLA_GATE_NORM)

    return _rowwise(body, [x], [w1, w2, b_gk.reshape(1, -1)], [GLA_QK], tm, name)[0]


def _gla_chunks(proj, logd, norm_g, bsz, t, name):
    c = GLA_CHUNK
    rows = 64
    nr = t // rows
    scale = GLA_DK ** -0.5

    def body(q_ref, k_ref, v_ref, gate_ref, ld_ref, ng_ref, o_ref, s_out_ref, st_ref):
        i = pl.program_id(1)

        @pl.when(i == 0)
        def _():
            st_ref[...] = jnp.zeros_like(st_ref)

        lblk = _tri(rows, block=c).astype(F32)
        bc_all = _dot3(lblk, ld_ref[...])
        incl = _tri(c)
        ng = ng_ref[...]
        for sub in range(rows // c):
            rs = slice(sub * c, (sub + 1) * c)
            for h in range(GLA_HEADS):
                ks = slice(h * GLA_DK, (h + 1) * GLA_DK)
                vs = slice(h * GLA_DV, (h + 1) * GLA_DV)
                bc = bc_all[rs, ks]
                q = q_ref[rs, ks] * scale
                k = k_ref[rs, ks]
                v = v_ref[rs, vs]
                q_in = q * jnp.exp(bc)
                a = jnp.where(incl, _dot_nt(q_in, k * jnp.exp(-bc)), 0.0)
                bcl = bc[c - 1:c]
                st = st_ref[h]
                o = _dot_nt(q_in, st) + _dot(a, v)
                st_ref[h] = st * jnp.exp(bcl) + _dot_tn(v, k * jnp.exp(bcl - bc))
                o = o * lax.rsqrt(jnp.mean(o * o, -1, keepdims=True) + 1e-6) * ng
                o_ref[rs, vs] = o * _silu(gate_ref[rs, vs])

        @pl.when(i == nr - 1)
        def _():
            for h in range(GLA_HEADS):
                s_out_ref[0, h] = st_ref[h].T

    return pl.pallas_call(
        body, grid=(bsz, nr),
        in_specs=[pl.BlockSpec((rows, GLA_QK), lambda b, i: (b * nr + i, 0)),
                  pl.BlockSpec((rows, GLA_QK), lambda b, i: (b * nr + i, 1)),
                  pl.BlockSpec((rows, GLA_V), lambda b, i: (b * nr + i, 1)),
                  pl.BlockSpec((rows, GLA_V), lambda b, i: (b * nr + i, 2)),
                  pl.BlockSpec((rows, GLA_QK), lambda b, i: (b * nr + i, 0)),
                  pl.BlockSpec((1, GLA_DV), lambda b, i: (0, 0))],
        out_specs=[pl.BlockSpec((rows, GLA_V), lambda b, i: (b * nr + i, 0)),
                   pl.BlockSpec((1, GLA_HEADS, GLA_DK, GLA_DV), lambda b, i: (b, 0, 0, 0))],
        out_shape=[jax.ShapeDtypeStruct((bsz * t, GLA_V), F32),
                   jax.ShapeDtypeStruct((bsz, GLA_HEADS, GLA_DK, GLA_DV), F32)],
        scratch_shapes=[pltpu.VMEM((GLA_HEADS, GLA_DV, GLA_DK), F32)],
        compiler_params=_cp(("parallel", "arbitrary")), name=name,
    )(proj, proj, proj, proj, logd, norm_g.reshape(1, -1))


def _gla_prompt(x, bsz, t, w_in, w_gk2, b_gk, norm_g, w_out, ln_g, ln_b):
    m = bsz * t
    tm = min(m, 512)
    proj = _mm(x, w_in, GLA_PROJ, tm, 512, "gla_proj")
    logd = _gla_logd(x, w_in, w_gk2, b_gk, tm, "gla_logd")
    o, s = _gla_chunks(proj, logd, norm_g, bsz, t, "gla_chunks")
    x1 = _out_ln(x, o, w_out, ln_g, ln_b, tm, "gla_out")
    return x1, s


def _rwkv_rkv(x, xp, mu, w_rkv, tm, name):
    def body(x_ref, xp_ref, mu_ref, w_ref, r_ref, k_ref, v_ref):
        x = x_ref[...]
        dx = xp_ref[...] - x
        for s, o_ref in enumerate((r_ref, k_ref, v_ref)):
            o_ref[...] = _dot(x + dx * mu_ref[s:s + 1], w_ref[s])

    return _rowwise(body, [x, xp], [mu, w_rkv], [D_MODEL] * 3, tm, name)


def _rwkv_lora(x, xp, mu, w0, w_w1, w_w2, a0, w_a1, w_a2, w_g1, w_g2, tm, name):
    def body(x_ref, xp_ref, mu_ref, w0_ref, ww1, ww2, a0_ref, wa1, wa2, wg1, wg2, lw_ref, a_ref, g_ref):
        x = x_ref[...]
        dx = xp_ref[...] - x
        xs = lambda s: x + dx * mu_ref[s:s + 1]
        w_log = -_softplus(-(w0_ref[...] + _dot(jnp.tanh(_dot(xs(3), ww1[...])), ww2[...]))) - 0.5
        lw_ref[...] = -jnp.exp(w_log)
        a_ref[...] = _sigmoid(a0_ref[...] + _dot(_dot(xs(4), wa1[...]), wa2[...]))
        g_ref[...] = _dot(_sigmoid(_dot(xs(5), wg1[...])), wg2[...])

    fulls = [mu, w0.reshape(1, -1), w_w1, w_w2, a0.reshape(1, -1), w_a1, w_a2, w_g1, w_g2]
    return _rowwise(body, [x, xp], fulls, [D_MODEL] * 3, tm, name)


def _rwkv_head_inputs(r, k, v, a, kk_w, ka_w, sl):
    kraw = k[:, sl]
    kkn = _l2n(kraw * kk_w[:, sl], 1.0)
    ah = a[:, sl]
    kh = kraw * (1.0 + (ah - 1.0) * ka_w[:, sl])
    return r[:, sl], kh, v[:, sl], kkn, kkn * ah


def _rwkv_head_out(y, rh, kh, vh, g, rk_w, lng, lnb, sl):
    yc = y - jnp.mean(y, -1, keepdims=True)
    yn = yc * lax.rsqrt(jnp.mean(yc * yc, -1, keepdims=True) + RW_GN_EPS) * lng[:, sl] + lnb[:, sl]
    bonus = jnp.sum(rh * kh * rk_w[:, sl], -1, keepdims=True) * vh
    return (yn + bonus) * g[:, sl]


def _rwkv_chunks(r, k, v, lw, a, g, k_k, k_a, r_k, ln_g, ln_b, bsz, t, name):
    c = RW_CHUNK
    nc = t // c
    hd = RW_HD

    def body(r_ref, k_ref, v_ref, lw_ref, a_ref, g_ref, kk_ref, ka_ref, rk_ref, lng_ref, lnb_ref,
             o_ref, s_out_ref, s_ref):
        i = pl.program_id(1)

        @pl.when(i == 0)
        def _():
            s_ref[...] = jnp.zeros_like(s_ref)

        lw = lw_ref[...]
        gam = _dot3(_tri(c).astype(F32), lw)
        gprev = gam - lw
        r, k, v, a, g = r_ref[...], k_ref[...], v_ref[...], a_ref[...], g_ref[...]
        kk_w, ka_w, rk_w, lng, lnb = kk_ref[...], ka_ref[...], rk_ref[...], lng_ref[...], lnb_ref[...]
        strict = _tri(c, strict=True)
        incl = _tri(c)
        for h in range(RW_H):
            sl = slice(h * hd, (h + 1) * hd)
            rh, kh, vh, kkn, bh = _rwkv_head_inputs(r, k, v, a, kk_w, ka_w, sl)
            gm, gp = gam[:, sl], gprev[:, sl]
            glast = gm[c - 1:c]
            e_neg = jnp.exp(-gm)
            e_out = jnp.exp(glast - gm)
            ar = jnp.concatenate([kkn * jnp.exp(gp), rh * jnp.exp(gm)], axis=0)
            gmat = _dot_nt(ar, jnp.concatenate([bh * e_neg, kh * e_neg], axis=0))
            s = s_ref[h]
            a_s = _dot_nt(ar, s)
            mab = jnp.where(strict, gmat[:c, :c], 0.0)
            mak = jnp.where(strict, gmat[:c, c:], 0.0)
            u = _dot(_neumann_inv(mab, c), -a_s[:c] - _dot(mak, vh))
            uv = jnp.concatenate([u, vh], axis=0)
            rbk = jnp.concatenate([jnp.where(incl, gmat[c:, :c], 0.0), jnp.where(incl, gmat[c:, c:], 0.0)], axis=1)
            y = a_s[c:] + _dot(rbk, uv)
            s_ref[h] = s * jnp.exp(glast) + _dot_tn(uv, jnp.concatenate([bh * e_out, kh * e_out], axis=0))
            o_ref[:, sl] = _rwkv_head_out(y, rh, kh, vh, g, rk_w, lng, lnb, sl)

        @pl.when(i == nc - 1)
        def _():
            s_out_ref[0] = s_ref[...]

    row = lambda w: w.reshape(1, D_MODEL)
    blk = pl.BlockSpec((c, D_MODEL), lambda b, i: (b * nc + i, 0))
    par = pl.BlockSpec((1, D_MODEL), lambda b, i: (0, 0))
    return pl.pallas_call(
        body, grid=(bsz, nc),
        in_specs=[blk] * 6 + [par] * 5,
        out_specs=[blk, pl.BlockSpec((1, RW_H, hd, hd), lambda b, i: (b, 0, 0, 0))],
        out_shape=[jax.ShapeDtypeStruct((bsz * t, D_MODEL), F32), jax.ShapeDtypeStruct((bsz, RW_H, hd, hd), F32)],
        scratch_shapes=[pltpu.VMEM((RW_H, hd, hd), F32)],
        compiler_params=_cp(("parallel", "arbitrary")), name=name,
    )(r, k, v, lw, a, g, row(k_k), row(k_a), row(r_k), row(ln_g), row(ln_b))


def _rwkv_prompt(x, bsz, t, mu, w_rkv, w0, w_w1, w_w2, a0, w_a1, w_a2, w_g1, w_g2, k_k, k_a, r_k, gn_g, gn_b, w_o,
                 ln_g, ln_b):
    m = bsz * t
    x3 = x.reshape(bsz, t, D_MODEL)
    xp = jnp.concatenate([jnp.zeros((bsz, 1, D_MODEL), F32), x3[:, :-1]], axis=1).reshape(m, D_MODEL)
    tm = min(m, 256)
    r, k, v = _rwkv_rkv(x, xp, mu, w_rkv, tm, "rwkv_rkv")
    lw, a, g = _rwkv_lora(x, xp, mu, w0, w_w1, w_w2, a0, w_a1, w_a2, w_g1, w_g2, tm, "rwkv_lora")
    y, s = _rwkv_chunks(r, k, v, lw, a, g, k_k, k_a, r_k, gn_g, gn_b, bsz, t, "rwkv_chunks")
    x1 = _out_ln(x, y, w_o, ln_g, ln_b, min(m, 512), "rwkv_out")
    return x1, x3[:, -1], s


def _eye(n):
    return _iota2((n, n), 0) == _iota2((n, n), 1)


def _to_col(row, eye):
    return jnp.sum(jnp.where(eye, row, 0.0), axis=1, keepdims=True)


def _to_row(col, eye):
    return jnp.sum(jnp.where(eye, col, 0.0), axis=0, keepdims=True)


def _row3(a):
    return a.reshape(a.shape[0], 1, a.shape[1])


def _rows_spec(width, col_block=0):
    return pl.BlockSpec((1, 1, width), lambda b: (b, 0, col_block))


def _gdn_conv_step(proj, buf, conv_w, name):
    def body(p_ref, b0, b1, b2, w_ref, o_ref):
        w = w_ref[...]
        y = p_ref[:, :GDN_CONV_DIM] * w[3:4] + b2[...] * w[2:3] + b1[...] * w[1:2] + b0[...] * w[0:1]
        _gdn_act(y, o_ref)

    rows = [proj, buf[:, 0], buf[:, 1], buf[:, 2]]
    return _rowwise(body, rows, [conv_w], [GDN_CONV_DIM], proj.shape[0], name)[0]


def _gdn_step(qkv, bg, proj, s0, norm_g, name):
    bsz = qkv.shape[0]
    rep = GDN_VH // GDN_KH

    def body(q_ref, k_ref, v_ref, bg_ref, z_ref, ng_ref, s_ref, o_ref, so_ref):
        eye = _eye(GDN_HD)
        bgv = bg_ref[0]
        ng = ng_ref[...]
        for kh in range(GDN_KH):
            sl = slice(kh * GDN_HD, (kh + 1) * GDN_HD)
            qrow, krow = q_ref[0, :, sl], k_ref[0, :, sl]
            qcol, kcol = _to_col(qrow, eye), _to_col(krow, eye)
            qk = jnp.sum(qrow * krow, axis=1, keepdims=True)
            for h in range(kh * rep, (kh + 1) * rep):
                vs = slice(h * GDN_HD, (h + 1) * GDN_HD)
                beta = bgv[:, h:h + 1]
                eg = jnp.exp(bgv[:, GDN_VH + h:GDN_VH + h + 1])
                s = s_ref[0, h]
                ks = jnp.sum(kcol * s, axis=0, keepdims=True)
                qs = jnp.sum(qcol * s, axis=0, keepdims=True)
                v_new = beta * (v_ref[0, :, vs] - eg * ks)
                o = eg * qs + qk * v_new
                so_ref[0, h] = s * eg + kcol * v_new
                o = o * lax.rsqrt(jnp.mean(o * o, -1, keepdims=True) + 1e-6) * ng
                o_ref[0, :, vs] = o * _silu(z_ref[0, :, vs])

    st_spec = pl.BlockSpec((1, GDN_VH, GDN_HD, GDN_HD), lambda b: (b, 0, 0, 0))
    o, s = pl.pallas_call(
        body, grid=(bsz,),
        in_specs=[_rows_spec(GDN_QK, 0), _rows_spec(GDN_QK, 1), _rows_spec(GDN_V, 1), _rows_spec(LANES),
                  _rows_spec(GDN_V, GDN_CONV_DIM // GDN_V), pl.BlockSpec((1, GDN_HD), lambda b: (0, 0)), st_spec],
        out_specs=[_rows_spec(GDN_V), st_spec],
        out_shape=[jax.ShapeDtypeStruct((bsz, 1, GDN_V), F32), jax.ShapeDtypeStruct(s0.shape, F32)],
        compiler_params=_cp(("parallel",)), name=name,
    )(_row3(qkv), _row3(qkv), _row3(qkv), _row3(bg), _row3(proj), norm_g.reshape(1, -1), s0)
    return o.reshape(bsz, GDN_V), s


def _gdn_sample(x, buf, s0, w_in, conv_w, a_log, dt_bias, norm_g, w_out, ln_g, ln_b):
    m = x.shape[0]
    proj = _mm(x, w_in, GDN_CONV_DIM + GDN_V, m, 512, "gdn_proj_s")
    bg = _gdn_gates(x, w_in, a_log, dt_bias, m, "gdn_gates_s")
    qkv = _gdn_conv_step(proj, buf, conv_w, "gdn_conv_s")
    o, s = _gdn_step(qkv, bg, proj, s0, norm_g, "gdn_step_s")
    x1 = _out_ln(x, o, w_out, ln_g, ln_b, m, "gdn_out_s")
    new_buf = jnp.concatenate([buf[:, 1:], proj[:, None, :GDN_CONV_DIM]], axis=1)
    return x1, new_buf, s


def _s5_step(x, s_re, s_im, lr, li, bcat, ccat, d_skip, name):
    def body(x_ref, sr_ref, si_ref, lr_ref, li_ref, b_ref, c_ref, d_ref, z_ref, hr_ref, hi_ref):
        x = x_ref[...]
        dsk = d_ref[...]
        for j in range(S5_NT):
            hs = slice(j * S5_HT, (j + 1) * S5_HT)
            xs = x[:, j * LANES:(j + 1) * LANES]
            bu = _dot(xs, b_ref[j])
            lam_r, lam_i = lr_ref[:, hs], li_ref[:, hs]
            sr, si = sr_ref[:, hs], si_ref[:, hs]
            h_re = lam_r * sr - lam_i * si + bu[:, :S5_HT]
            h_im = lam_r * si + lam_i * sr + bu[:, S5_HT:]
            hr_ref[:, hs] = h_re
            hi_ref[:, hs] = h_im
            cj = c_ref[j]
            y = _dot(h_re, cj[:S5_HT]) + _dot(h_im, cj[S5_HT:])
            z_ref[:, j * LANES:(j + 1) * LANES] = _gelu(y + dsk[:, j * LANES:(j + 1) * LANES] * xs)

    fulls = [lr, li, bcat, ccat, d_skip.reshape(1, -1)]
    return _rowwise(body, [x, s_re, s_im], fulls, [D_MODEL, S5_H, S5_H], x.shape[0], name)


def _s5_sample(x, s_re0, s_im0, a_re, a_im, log_dt, b_re, b_im, c_re, c_im, d_skip, w_o, w_gate, ln_g, ln_b):
    m = x.shape[0]
    lr, li, bcat, ccat = _s5_weights(a_re, a_im, log_dt, b_re, b_im, c_re, c_im)
    z, h_re, h_im = _s5_step(x, s_re0.reshape(m, S5_H), s_im0.reshape(m, S5_H), lr, li, bcat, ccat, d_skip, "s5_step_s")
    x1 = _s5_out(x, z, w_o, w_gate, ln_g, ln_b, m, "s5_out_s")
    return x1, h_re.reshape(m, S5_GROUPS, S5_STATE), h_im.reshape(m, S5_GROUPS, S5_STATE)


def _gla_step(proj, logd, s0, norm_g, name):
    bsz = proj.shape[0]
    scale = GLA_DK ** -0.5

    def body(q_ref, k_ref, v_ref, gate_ref, ld_ref, ng_ref, s_ref, o_ref, so_ref):
        eye = _eye(GLA_DK)
        ng = ng_ref[...]
        for h in range(GLA_HEADS):
            ks = slice(h * GLA_DK, (h + 1) * GLA_DK)
            vs = slice(h * GLA_DV, (h + 1) * GLA_DV)
            bc = ld_ref[0, :, ks]
            k = k_ref[0, :, ks]
            v = v_ref[0, :, vs]
            q_in = q_ref[0, :, ks] * scale * jnp.exp(bc)
            a = jnp.sum(q_in * (k * jnp.exp(-bc)), axis=1, keepdims=True)
            s = s_ref[0, h]
            o = jnp.sum(_to_col(q_in, eye) * s, axis=0, keepdims=True) + a * v
            so_ref[0, h] = s * _to_col(jnp.exp(bc), eye) + _to_col(k, eye) * v
            o = o * lax.rsqrt(jnp.mean(o * o, -1, keepdims=True) + 1e-6) * ng
            o_ref[0, :, vs] = o * _silu(gate_ref[0, :, vs])

    st_spec = pl.BlockSpec((1, GLA_HEADS, GLA_DK, GLA_DV), lambda b: (b, 0, 0, 0))
    o, s = pl.pallas_call(
        body, grid=(bsz,),
        in_specs=[_rows_spec(GLA_QK, 0), _rows_spec(GLA_QK, 1), _rows_spec(GLA_V, 1), _rows_spec(GLA_V, 2),
                  _rows_spec(GLA_QK), pl.BlockSpec((1, GLA_DV), lambda b: (0, 0)), st_spec],
        out_specs=[_rows_spec(GLA_V), st_spec],
        out_shape=[jax.ShapeDtypeStruct((bsz, 1, GLA_V), F32), jax.ShapeDtypeStruct(s0.shape, F32)],
        compiler_params=_cp(("parallel",)), name=name,
    )(_row3(proj), _row3(proj), _row3(proj), _row3(proj), _row3(logd), norm_g.reshape(1, -1), s0)
    return o.reshape(bsz, GLA_V), s


def _gla_sample(x, s0, w_in, w_gk2, b_gk, norm_g, w_out, ln_g, ln_b):
    m = x.shape[0]
    proj = _mm(x, w_in, GLA_PROJ, m, 512, "gla_proj_s")
    logd = _gla_logd(x, w_in, w_gk2, b_gk, m, "gla_logd_s")
    o, s = _gla_step(proj, logd, s0, norm_g, "gla_step_s")
    x1 = _out_ln(x, o, w_out, ln_g, ln_b, m, "gla_out_s")
    return x1, s


def _rwkv_step(r, k, v, lw, a, g, s0, k_k, k_a, r_k, ln_g, ln_b, name):
    bsz = r.shape[0]
    hd = RW_HD

    def body(r_ref, k_ref, v_ref, lw_ref, a_ref, g_ref, kk_ref, ka_ref, rk_ref, lng_ref, lnb_ref, s_ref,
             o_ref, so_ref):
        eye = _eye(hd)
        r, k, v, a, g = r_ref[0], k_ref[0], v_ref[0], a_ref[0], g_ref[0]
        dec = jnp.exp(lw_ref[0])
        kk_w, ka_w, rk_w, lng, lnb = kk_ref[...], ka_ref[...], rk_ref[...], lng_ref[...], lnb_ref[...]
        for h in range(RW_H):
            sl = slice(h * hd, (h + 1) * hd)
            rh, kh, vh, kkn, bh = _rwkv_head_inputs(r, k, v, a, kk_w, ka_w, sl)
            s = s_ref[0, h]
            sa = -jnp.sum(s * kkn, axis=1, keepdims=True)
            s_new = s * dec[:, sl] + sa * bh + _to_col(vh, eye) * kh
            so_ref[0, h] = s_new
            y = _to_row(jnp.sum(s_new * rh, axis=1, keepdims=True), eye)
            o_ref[0, :, sl] = _rwkv_head_out(y, rh, kh, vh, g, rk_w, lng, lnb, sl)

    row = lambda w: w.reshape(1, D_MODEL)
    par = pl.BlockSpec((1, D_MODEL), lambda b: (0, 0))
    st_spec = pl.BlockSpec((1, RW_H, hd, hd), lambda b: (b, 0, 0, 0))
    o, s = pl.pallas_call(
        body, grid=(bsz,),
        in_specs=[_rows_spec(D_MODEL)] * 6 + [par] * 5 + [st_spec],
        out_specs=[_rows_spec(D_MODEL), st_spec],
        out_shape=[jax.ShapeDtypeStruct((bsz, 1, D_MODEL), F32), jax.ShapeDtypeStruct(s0.shape, F32)],
        compiler_params=_cp(("parallel",)), name=name,
    )(*[_row3(t) for t in (r, k, v, lw, a, g)], row(k_k), row(k_a), row(r_k), row(ln_g), row(ln_b), s0)
    return o.reshape(bsz, D_MODEL), s


def _rwkv_sample(x, shift0, s0, mu, w_rkv, w0, w_w1, w_w2, a0, w_a1, w_a2, w_g1, w_g2, k_k, k_a, r_k, gn_g, gn_b, w_o,
                 ln_g, ln_b):
    m = x.shape[0]
    r, k, v = _rwkv_rkv(x, shift0, mu, w_rkv, m, "rwkv_rkv_s")
    lw, a, g = _rwkv_lora(x, shift0, mu, w0, w_w1, w_w2, a0, w_a1, w_a2, w_g1, w_g2, m, "rwkv_lora_s")
    y, s = _rwkv_step(r, k, v, lw, a, g, s0, k_k, k_a, r_k, gn_g, gn_b, "rwkv_step_s")
    x1 = _out_ln(x, y, w_o, ln_g, ln_b, m, "rwkv_out_s")
    return x1, x, s


def kernel(x_prompt, x_sample, state_gdn_conv, state_gdn, state_s5_re, state_s5_im, state_gla, state_rwkv_shift,
           state_rwkv, p_prompt, p_sample, gdn_w_in, gdn_conv_w, gdn_a_log, gdn_dt_bias, gdn_norm_g, gdn_w_out,
           s5_a_re, s5_a_im, s5_log_dt, s5_b_re, s5_b_im, s5_c_re, s5_c_im, s5_d, s5_w_o, s5_w_gate,
           gla_w_in, gla_w_gk2, gla_b_gk, gla_norm_g, gla_w_out,
           rwkv_mu, rwkv_w_rkv, rwkv_w0, rwkv_w_w1, rwkv_w_w2, rwkv_a0, rwkv_w_a1, rwkv_w_a2, rwkv_w_g1, rwkv_w_g2,
           rwkv_k_k, rwkv_k_a, rwkv_r_k, rwkv_ln_g, rwkv_ln_b, rwkv_w_o,
           ln_mix_g, ln_mix_b, ln_ffn_g, ln_ffn_b, mlp_w1, mlp_w2, ple_w, ple_gate_w):
    bsz, t, d = x_prompt.shape
    bs = x_sample.shape[0]
    gdn_w = (gdn_w_in[0], gdn_conv_w[0], gdn_a_log[0], gdn_dt_bias[0], gdn_norm_g[0], gdn_w_out[0])
    s5_w = (s5_a_re[0], s5_a_im[0], s5_log_dt[0], s5_b_re[0], s5_b_im[0], s5_c_re[0], s5_c_im[0], s5_d[0],
            s5_w_o[0], s5_w_gate[0])
    gla_w = (gla_w_in[0], gla_w_gk2[0], gla_b_gk[0], gla_norm_g[0], gla_w_out[0])
    rwkv_w = (rwkv_mu[0], rwkv_w_rkv[0], rwkv_w0[0], rwkv_w_w1[0], rwkv_w_w2[0], rwkv_a0[0], rwkv_w_a1[0],
              rwkv_w_a2[0], rwkv_w_g1[0], rwkv_w_g2[0], rwkv_k_k[0], rwkv_k_a[0], rwkv_r_k[0], rwkv_ln_g[0],
              rwkv_ln_b[0], rwkv_w_o[0])

    def ffn(x, p, i, tm, tag):
        return _mlp_ple(x, p, mlp_w1[i], mlp_w2[i], ln_ffn_g[i], ln_ffn_b[i], ple_w[i], ple_gate_w[i], tm, 512,
                        f"mlp{i}_{tag}")

    xp = x_prompt.reshape(bsz * t, d)
    pp = p_prompt.reshape(DEPTH, bsz * t, D_PLE)
    tm_p = min(bsz * t, 1024)
    xp, gc_p, gs_p = _gdn_prompt(xp, bsz, t, *gdn_w, ln_mix_g[0], ln_mix_b[0])
    xp = ffn(xp, pp[0], 0, tm_p, "p")
    xp, sr_p, si_p = _s5_prompt(xp, bsz, t, *s5_w, ln_mix_g[1], ln_mix_b[1])
    xp = ffn(xp, pp[1], 1, tm_p, "p")
    xp, la_p = _gla_prompt(xp, bsz, t, *gla_w, ln_mix_g[2], ln_mix_b[2])
    xp = ffn(xp, pp[2], 2, tm_p, "p")
    xp, sh_p, rs_p = _rwkv_prompt(xp, bsz, t, *rwkv_w, ln_mix_g[3], ln_mix_b[3])
    xp = ffn(xp, pp[3], 3, tm_p, "p")

    xs = x_sample.reshape(bs, d)
    ps = p_sample.reshape(DEPTH, bs, D_PLE)
    xs, gc_s, gs_s = _gdn_sample(xs, state_gdn_conv[0], state_gdn[0], *gdn_w, ln_mix_g[0], ln_mix_b[0])
    xs = ffn(xs, ps[0], 0, bs, "s")
    xs, sr_s, si_s = _s5_sample(xs, state_s5_re[0], state_s5_im[0], *s5_w, ln_mix_g[1], ln_mix_b[1])
    xs = ffn(xs, ps[1], 1, bs, "s")
    xs, la_s = _gla_sample(xs, state_gla[0], *gla_w, ln_mix_g[2], ln_mix_b[2])
    xs = ffn(xs, ps[2], 2, bs, "s")
    xs, sh_s, rs_s = _rwkv_sample(xs, state_rwkv_shift[0], state_rwkv[0], *rwkv_w, ln_mix_g[3], ln_mix_b[3])
    xs = ffn(xs, ps[3], 3, bs, "s")

    e = lambda a: a[None]
    return (xp.reshape(bsz, t, d), xs.reshape(bs, 1, d), e(gc_p), e(gc_s), e(gs_p), e(gs_s), e(sr_p), e(sr_s),
            e(si_p), e(si_s), e(la_p), e(la_s), e(sh_p), e(sh_s), e(rs_p), e(rs_s))
```

```python
import functools
import math

import jax
import jax.numpy as jnp
from jax import lax
from jax.experimental import pallas as pl
from jax.experimental.pallas import tpu as pltpu

F32 = jnp.float32
BF16 = jnp.bfloat16

D_MODEL = 1024
DEPTH = 4
D_PLE = 256
D_FF = 4 * D_MODEL
LN_EPS = 1e-5
ALPHA = (2.0 * DEPTH) ** 0.25

GDN_HD = 128
GDN_KH = 8
GDN_VH = 16
GDN_QK = GDN_KH * GDN_HD
GDN_V = GDN_VH * GDN_HD
GDN_CONV_DIM = 2 * GDN_QK + GDN_V
GDN_CONV_W = 4
GDN_CHUNK = 64

S5_GROUP = 16
S5_GROUPS = D_MODEL // S5_GROUP
S5_STATE = 64
S5_H = S5_GROUPS * S5_STATE

GLA_HEADS = 4
GLA_DK = 128
GLA_DV = 256
GLA_QK = GLA_HEADS * GLA_DK
GLA_V = GLA_HEADS * GLA_DV
GLA_RANK = 16
GLA_GATE_NORM = 16.0
GLA_CHUNK = 16

RW_HD = 64
RW_H = D_MODEL // RW_HD
RW_GN_EPS = 64e-5
RW_CHUNK = 64

LANES = 128
SUBLANES = 8
VMEM_LIMIT = 56 * 1024 * 1024


def _cp(sem, vmem=VMEM_LIMIT):
    return pltpu.CompilerParams(dimension_semantics=sem, vmem_limit_bytes=vmem)


def _dot(a, b):
    return jnp.dot(a, b, preferred_element_type=F32)


def _dot_nt(a, b):
    return lax.dot_general(a, b, (((1,), (1,)), ((), ())), preferred_element_type=F32)


def _dot_tn(a, b):
    return lax.dot_general(a, b, (((0,), (0,)), ((), ())), preferred_element_type=F32)


def _dot3(l, x):
    hi = x.astype(BF16).astype(F32)
    r1 = x - hi
    mid = r1.astype(BF16).astype(F32)
    lo = r1 - mid
    return _dot(l, hi) + _dot(l, mid) + _dot(l, lo)


def _softplus(x):
    return jnp.maximum(x, 0.0) + jnp.log(1.0 + jnp.exp(-jnp.abs(x)))


def _sigmoid(x):
    return 1.0 / (1.0 + jnp.exp(-x))


def _silu(x):
    return x * _sigmoid(x)


def _ln(x, g, b):
    xc = x - jnp.mean(x, -1, keepdims=True)
    var = jnp.mean(xc * xc, -1, keepdims=True)
    return xc * lax.rsqrt(var + LN_EPS) * g + b


def _iota2(shape, axis):
    return lax.broadcasted_iota(jnp.int32, shape, axis)


def _tri(n, strict=False, block=None):
    r = _iota2((n, n), 0)
    c = _iota2((n, n), 1)
    m = (r > c) if strict else (r >= c)
    if block is not None:
        m = m & ((r // block) == (c // block))
    return m


def _full_spec(a):
    n = a.ndim
    return pl.BlockSpec(a.shape, lambda *_: (0,) * n)


def _mm(x, w, n_cols, tm, tn, name):
    m, k = x.shape

    def body(x_ref, w_ref, o_ref):
        o_ref[...] = _dot(x_ref[...], w_ref[...])

    return pl.pallas_call(
        body, grid=(m // tm, n_cols // tn),
        in_specs=[pl.BlockSpec((tm, k), lambda i, j: (i, 0)), pl.BlockSpec((k, tn), lambda i, j: (0, j))],
        out_specs=pl.BlockSpec((tm, tn), lambda i, j: (i, j)),
        out_shape=jax.ShapeDtypeStruct((m, n_cols), F32),
        compiler_params=_cp(("parallel", "arbitrary")), name=name)(x, w)


def _rowwise(body, rows, fulls, out_cols, tm, name):
    m = rows[0].shape[0]
    in_specs = [pl.BlockSpec((tm, r.shape[1]), lambda i: (i, 0)) for r in rows] + [_full_spec(f) for f in fulls]
    out_specs = [pl.BlockSpec((tm, c), lambda i: (i, 0)) for c in out_cols]
    out_shape = [jax.ShapeDtypeStruct((m, c), F32) for c in out_cols]
    res = pl.pallas_call(body, grid=(m // tm,), in_specs=in_specs, out_specs=out_specs, out_shape=out_shape,
                         compiler_params=_cp(("parallel",)), name=name)(*rows, *fulls)
    return res


def _out_ln(x, o, w_out, g, b, tm, name):
    def body(x_ref, o_ref, w_ref, g_ref, b_ref, y_ref):
        y_ref[...] = _ln(ALPHA * x_ref[...] + _dot(o_ref[...], w_ref[...]), g_ref[...], b_ref[...])

    return _rowwise(body, [x, o], [w_out, g.reshape(1, -1), b.reshape(1, -1)], [D_MODEL], tm, name)[0]


def _mlp_ple(x1, p, w1, w2, g, b, ple_w, gate_w, tm, tf, name):
    m = x1.shape[0]
    nf = D_FF // tf

    def body(x_ref, p_ref, w1_ref, w2_ref, g_ref, b_ref, pw_ref, gw_ref, o_ref, acc_ref):
        f = pl.program_id(1)
        x = x_ref[...]
        h = jnp.maximum(_dot(x, w1_ref[...]), 0.0)
        c = _dot(h * h, w2_ref[...])

        @pl.when(f == 0)
        def _():
            acc_ref[...] = c

        @pl.when(f > 0)
        def _():
            acc_ref[...] += c

        @pl.when(f == nf - 1)
        def _():
            x2 = _ln(ALPHA * x + acc_ref[...], g_ref[...], b_ref[...])
            o_ref[...] = x2 + _dot(p_ref[...], pw_ref[...]) * _sigmoid(_dot(x2, gw_ref[...]))

    return pl.pallas_call(
        body, grid=(m // tm, nf),
        in_specs=[pl.BlockSpec((tm, D_MODEL), lambda i, f: (i, 0)),
                  pl.BlockSpec((tm, D_PLE), lambda i, f: (i, 0)),
                  pl.BlockSpec((D_MODEL, tf), lambda i, f: (0, f)),
                  pl.BlockSpec((tf, D_MODEL), lambda i, f: (f, 0)),
                  pl.BlockSpec((1, D_MODEL), lambda i, f: (0, 0)),
                  pl.BlockSpec((1, D_MODEL), lambda i, f: (0, 0)),
                  pl.BlockSpec((D_PLE, D_MODEL), lambda i, f: (0, 0)),
                  pl.BlockSpec((D_MODEL, D_MODEL), lambda i, f: (0, 0))],
        out_specs=pl.BlockSpec((tm, D_MODEL), lambda i, f: (i, 0)),
        out_shape=jax.ShapeDtypeStruct((m, D_MODEL), F32),
        scratch_shapes=[pltpu.VMEM((tm, D_MODEL), F32)],
        compiler_params=_cp(("parallel", "arbitrary")), name=name,
    )(x1, p, w1, w2, g.reshape(1, -1), b.reshape(1, -1), ple_w, gate_w)


def _gdn_gates(x, w_in, a_log, dt_bias, tm, name):
    w_ba = jnp.pad(w_in[:, GDN_CONV_DIM + GDN_V:], ((0, 0), (0, LANES - 2 * GDN_VH)))
    pad = lambda v: jnp.pad(v.reshape(1, -1), ((0, 0), (GDN_VH, LANES - 2 * GDN_VH)))

    def body(x_ref, w_ref, al_ref, dt_ref, o_ref):
        y = _dot(x_ref[...], w_ref[...])
        lane = _iota2(y.shape, 1)
        g = -jnp.exp(al_ref[...]) * _softplus(y + dt_ref[...])
        o_ref[...] = jnp.where(lane < GDN_VH, _sigmoid(y), g)

    return _rowwise(body, [x], [w_ba, pad(a_log), pad(dt_bias)], [LANES], tm, name)[0]


def _l2n(x, scale):
    return x * (lax.rsqrt(jnp.sum(x * x, -1, keepdims=True) + 1e-6) * scale)


def _gdn_act(y, o_ref):
    y = _silu(y)
    for h in range(GDN_KH):
        sl = slice(h * GDN_HD, (h + 1) * GDN_HD)
        o_ref[:, sl] = _l2n(y[:, sl], GDN_HD ** -0.5)
        sl = slice(GDN_QK + h * GDN_HD, GDN_QK + (h + 1) * GDN_HD)
        o_ref[:, sl] = _l2n(y[:, sl], 1.0)
    o_ref[:, 2 * GDN_QK:] = y[:, 2 * GDN_QK:]


def _gdn_conv_prompt(proj, conv_w, bsz, t, tt, name):
    nt = t // tt
    c = GDN_CONV_DIM

    def body(x_ref, w_ref, o_ref, tail_ref, carry_ref):
        i = pl.program_id(1)

        @pl.when(i == 0)
        def _():
            carry_ref[...] = jnp.zeros_like(carry_ref)

        x = x_ref[...]
        w = w_ref[...]
        y = x * w[3:4]
        x8 = x[:SUBLANES]
        c8 = carry_ref[...]
        row8 = _iota2(x8.shape, 0)
        y8 = x8 * w[3:4]
        for j in range(1, GDN_CONV_W):
            wj = w[3 - j:4 - j]
            y = y + pltpu.roll(x, j, 0) * wj
            y8 = y8 + jnp.where(row8 < j, pltpu.roll(c8, j, 0), pltpu.roll(x8, j, 0)) * wj
        _gdn_act(y, o_ref)
        _gdn_act(y8, o_ref.at[pl.ds(0, SUBLANES)])
        carry_ref[...] = x[tt - SUBLANES:]
        tail_ref[0] = x[tt - SUBLANES:]

    return pl.pallas_call(
        body, grid=(bsz, nt),
        in_specs=[pl.BlockSpec((tt, c), lambda b, i: (b * nt + i, 0)), _full_spec(conv_w)],
        out_specs=[pl.BlockSpec((tt, c), lambda b, i: (b * nt + i, 0)),
                   pl.BlockSpec((1, SUBLANES, c), lambda b, i: (b, 0, 0))],
        out_shape=[jax.ShapeDtypeStruct((bsz * t, c), F32), jax.ShapeDtypeStruct((bsz, SUBLANES, c), F32)],
        scratch_shapes=[pltpu.VMEM((SUBLANES, c), F32)],
        compiler_params=_cp(("parallel", "arbitrary")), name=name)(proj, conv_w)


def _neumann_inv(ms, n):
    eye = (_iota2((n, n), 0) == _iota2((n, n), 1)).astype(F32)
    ts = [eye - m for m in ms]
    ps = [_dot(m, m) for m in ms]
    k = 2
    while True:
        ts = [t + _dot(t, p) for t, p in zip(ts, ps)]
        k *= 2
        if k >= n:
            return ts
        ps = [_dot(p, p) for p in ps]


def _gdn_chunks(qkv, bg, proj, norm_g, bsz, t, name):
    c = GDN_CHUNK
    nc = t // c
    z_blk = GDN_CONV_DIM // GDN_V

    def body(q_ref, k_ref, v_ref, bg_ref, z_ref, ng_ref, o_ref, s_out_ref, s_ref):
        i = pl.program_id(1)

        @pl.when(i == 0)
        def _():
            s_ref[...] = jnp.zeros_like(s_ref)

        bgv = bg_ref[...]
        ltri = _tri(c).astype(F32)
        gc = _dot3(ltri, bgv)
        gct = jnp.concatenate([gc, jnp.zeros_like(gc)], axis=0).T
        incl = _tri(c)
        strict = _tri(c, strict=True)
        ng = ng_ref[...]
        rep = GDN_VH // GDN_KH
        heads = range(GDN_VH)
        hsl = lambda n: slice(n * GDN_HD, (n + 1) * GDN_HD)
        gram = [_dot_nt(jnp.concatenate([k_ref[:, hsl(n)], q_ref[:, hsl(n)]], axis=0), k_ref[:, hsl(n)])
                for n in range(GDN_KH)]
        ms, aqks, rhss, q_ins, k_outs, g_ends = [], [], [], [], [], []
        for h in heads:
            kh = h // rep
            kk = k_ref[:, hsl(kh)]
            beta = bgv[:, h:h + 1]
            gcol = gc[:, GDN_VH + h:GDN_VH + h + 1]
            grow = gct[GDN_VH + h:GDN_VH + h + 1, :c]
            decay = jnp.where(incl, jnp.exp(jnp.where(incl, gcol - grow, 0.0)), 0.0)
            ms.append(jnp.where(strict, gram[kh][:c] * beta * decay, 0.0))
            aqks.append(gram[kh][c:] * decay)
            egc = jnp.exp(gcol)
            rhss.append(jnp.concatenate([v_ref[:, hsl(h)] * beta, kk * (beta * egc)], axis=1))
            q_ins.append(q_ref[:, hsl(kh)] * egc)
            glast = gc[c - 1:c, GDN_VH + h:GDN_VH + h + 1]
            k_outs.append(kk * jnp.exp(glast - gcol))
            g_ends.append(jnp.exp(glast))
        tinvs = _neumann_inv(ms, c)
        sols = [_dot(tinvs[h], rhss[h]) for h in heads]
        wqs = [_dot(jnp.concatenate([sols[h][:, GDN_HD:], q_ins[h]], axis=0), s_ref[h]) for h in heads]
        v_news = [sols[h][:, :GDN_HD] - wqs[h][:c] for h in heads]
        avs = [_dot(aqks[h], v_news[h]) for h in heads]
        kvs = [_dot_tn(k_outs[h], v_news[h]) for h in heads]
        for h in heads:
            s_ref[h] = s_ref[h] * g_ends[h] + kvs[h]
            o = wqs[h][c:] + avs[h]
            o = o * lax.rsqrt(jnp.mean(o * o, -1, keepdims=True) + 1e-6) * ng
            o_ref[:, hsl(h)] = o * _silu(z_ref[:, hsl(h)])

        @pl.when(i == nc - 1)
        def _():
            s_out_ref[0] = s_ref[...]

    return pl.pallas_call(
        body, grid=(bsz, nc),
        in_specs=[pl.BlockSpec((c, GDN_QK), lambda b, i: (b * nc + i, 0)),
                  pl.BlockSpec((c, GDN_QK), lambda b, i: (b * nc + i, 1)),
                  pl.BlockSpec((c, GDN_V), lambda b, i: (b * nc + i, 1)),
                  pl.BlockSpec((c, LANES), lambda b, i: (b * nc + i, 0)),
                  pl.BlockSpec((c, GDN_V), lambda b, i: (b * nc + i, z_blk)),
                  pl.BlockSpec((1, GDN_HD), lambda b, i: (0, 0))],
        out_specs=[pl.BlockSpec((c, GDN_V), lambda b, i: (b * nc + i, 0)),
                   pl.BlockSpec((1, GDN_VH, GDN_HD, GDN_HD), lambda b, i: (b, 0, 0, 0))],
        out_shape=[jax.ShapeDtypeStruct((bsz * t, GDN_V), F32),
                   jax.ShapeDtypeStruct((bsz, GDN_VH, GDN_HD, GDN_HD), F32)],
        scratch_shapes=[pltpu.VMEM((GDN_VH, GDN_HD, GDN_HD), F32)],
        compiler_params=_cp(("parallel", "arbitrary")), name=name,
    )(qkv, qkv, qkv, bg, proj, norm_g.reshape(1, -1))


def _gdn_prompt(x, bsz, t, w_in, conv_w, a_log, dt_bias, norm_g, w_out, ln_g, ln_b):
    m = bsz * t
    tm = min(m, 512)
    proj = _mm(x, w_in, GDN_CONV_DIM + GDN_V, min(m, 2048), 512, "gdn_proj")
    bg = _gdn_gates(x, w_in, a_log, dt_bias, tm, "gdn_gates")
    qkv, tail = _gdn_conv_prompt(proj, conv_w, bsz, t, min(t, 256), "gdn_conv")
    o, s = _gdn_chunks(qkv, bg, proj, norm_g, bsz, t, "gdn_chunks")
    x1 = _out_ln(x, o, w_out, ln_g, ln_b, tm, "gdn_out")
    return x1, tail[:, SUBLANES - (GDN_CONV_W - 1):], s


def _s5_discretize(a_re, a_im, log_dt, b_re, b_im):
    g, p = a_re.shape
    bt_re = jnp.swapaxes(b_re, 1, 2)
    bt_im = jnp.swapaxes(b_im, 1, 2)

    def body(ar_ref, ai_ref, ldt_ref, br_ref, bi_ref, lr_ref, li_ref, bbr_ref, bbi_ref):
        ar, ai = ar_ref[...], ai_ref[...]
        dt = jnp.exp(ldt_ref[...])
        mag = jnp.exp(ar * dt)
        lr, li = mag * jnp.cos(ai * dt), mag * jnp.sin(ai * dt)
        den = ar * ar + ai * ai
        f_re = ((lr - 1.0) * ar + li * ai) / den
        f_im = (li * ar - (lr - 1.0) * ai) / den
        lr_ref[...] = lr
        li_ref[...] = li
        br, bi = br_ref[...], bi_ref[...]
        fr, fi = f_re[:, None, :], f_im[:, None, :]
        bbr_ref[...] = fr * br - fi * bi
        bbi_ref[...] = fr * bi + fi * br

    args = (a_re, a_im, log_dt.reshape(g, 1), bt_re, bt_im)
    return pl.pallas_call(
        body, grid=(1,), in_specs=[_full_spec(a) for a in args],
        out_specs=[pl.BlockSpec((g, p), lambda i: (0, 0))] * 2 + [pl.BlockSpec(bt_re.shape, lambda i: (0, 0, 0))] * 2,
        out_shape=[jax.ShapeDtypeStruct((g, p), F32)] * 2 + [jax.ShapeDtypeStruct(bt_re.shape, F32)] * 2,
        name="s5_discretize")(*args)


def _blockdiag(a, per):
    g, r, c = a.shape
    a4 = a.reshape(g // per, per, r, c)
    eye = jnp.eye(per, dtype=a.dtype)
    return jnp.einsum("jgrc,gh->jgrhc", a4, eye).reshape(g // per, per * r, per * c)


S5_GPT = LANES // S5_GROUP
S5_NT = S5_GROUPS // S5_GPT
S5_HT = S5_GPT * S5_STATE


def _s5_weights(a_re, a_im, log_dt, b_re, b_im, c_re, c_im):
    lr, li, bbr, bbi = _s5_discretize(a_re, a_im, log_dt, b_re, b_im)
    bcat = jnp.concatenate([_blockdiag(bbr, S5_GPT), _blockdiag(bbi, S5_GPT)], axis=2)
    ccat = jnp.concatenate([_blockdiag(jnp.swapaxes(c_re, 1, 2), S5_GPT),
                            -_blockdiag(jnp.swapaxes(c_im, 1, 2), S5_GPT)], axis=1)
    return lr.reshape(1, S5_H), li.reshape(1, S5_H), bcat, ccat


def _gelu(y):
    return 0.5 * y * (1.0 + jnp.tanh(math.sqrt(2.0 / math.pi) * (y + 0.044715 * (y * y * y))))


def _s5_scan(x, bsz, t, s_re0, s_im0, lr, li, bcat, ccat, d_skip, tc, name):
    d = x.shape[1]
    nt = t // tc
    rows = bsz * tc
    lq = 1024

    def body(x_ref, sr0_ref, si0_ref, lr_ref, li_ref, b_ref, c_ref, d_ref, z_ref, sr_ref, si_ref, hre, him):
        i = pl.program_id(0)

        @pl.when(i == 0)
        def _():
            sr_ref[...] = sr0_ref[...]
            si_ref[...] = si0_ref[...]

        def x_tile(j):
            return jnp.concatenate([x_ref[:, s * d + j * LANES:s * d + (j + 1) * LANES] for s in range(tc)], axis=0)

        for j in range(S5_NT):
            bu = _dot(x_tile(j), b_ref[j])
            hre[:, j * S5_HT:(j + 1) * S5_HT] = bu[:, :S5_HT]
            him[:, j * S5_HT:(j + 1) * S5_HT] = bu[:, S5_HT:]

        for q in range(S5_H // lq):
            ls = slice(q * lq, (q + 1) * lq)
            lam_r = jnp.broadcast_to(lr_ref[:, ls], (bsz, lq))
            lam_i = jnp.broadcast_to(li_ref[:, ls], (bsz, lq))

            def step(tt, carry):
                sr, si = carry
                idx = pl.ds(pl.multiple_of(tt * bsz, bsz), bsz)
                nr = lam_r * sr - lam_i * si + hre[idx, ls]
                ni = lam_r * si + lam_i * sr + him[idx, ls]
                hre[idx, ls] = nr
                him[idx, ls] = ni
                return nr, ni

            sr, si = lax.fori_loop(0, tc, step, (sr_ref[:, ls], si_ref[:, ls]), unroll=8)
            sr_ref[:, ls] = sr
            si_ref[:, ls] = si

        dsk = d_ref[...]
        for j in range(S5_NT):
            hs = slice(j * S5_HT, (j + 1) * S5_HT)
            cj = c_ref[j]
            y = _dot(hre[:, hs], cj[:S5_HT]) + _dot(him[:, hs], cj[S5_HT:])
            z_ref[:, j * LANES:(j + 1) * LANES] = _gelu(y + dsk[:, j * LANES:(j + 1) * LANES] * x_tile(j))

    fulls = (s_re0, s_im0, lr, li, bcat, ccat, d_skip.reshape(1, d))
    return pl.pallas_call(
        body, grid=(nt,),
        in_specs=[pl.BlockSpec((bsz, tc * d), lambda i: (0, i))] + [_full_spec(a) for a in fulls],
        out_specs=[pl.BlockSpec((rows, d), lambda i: (i, 0)),
                   pl.BlockSpec((bsz, S5_H), lambda i: (0, 0)), pl.BlockSpec((bsz, S5_H), lambda i: (0, 0))],
        out_shape=[jax.ShapeDtypeStruct((t * bsz, d), F32),
                   jax.ShapeDtypeStruct((bsz, S5_H), F32), jax.ShapeDtypeStruct((bsz, S5_H), F32)],
        scratch_shapes=[pltpu.VMEM((rows, S5_H), F32), pltpu.VMEM((rows, S5_H), F32)],
        compiler_params=_cp(("arbitrary",)), name=name)(x.reshape(bsz, t * d), *fulls)


def _s5_out_tm(x, z_tm, bsz, t, w_o, w_gate, g, b, tt, name):
    d = x.shape[1]
    nt = t // tt

    def body(x_ref, z_ref, wo_ref, wg_ref, g_ref, b_ref, y_ref):
        z = z_ref[...]
        h = _dot(z, wo_ref[...]) * _sigmoid(_dot(z, wg_ref[...]))
        y_ref[...] = _ln(ALPHA * x_ref[...] + h, g_ref[...], b_ref[...])

    fulls = (w_o, w_gate, g.reshape(1, -1), b.reshape(1, -1))
    return pl.pallas_call(
        body, grid=(bsz, nt),
        in_specs=[pl.BlockSpec((tt, d), lambda bb, i: (bb * nt + i, 0)),
                  pl.BlockSpec((tt, d), lambda bb, i: (i, bb))] + [_full_spec(a) for a in fulls],
        out_specs=pl.BlockSpec((tt, d), lambda bb, i: (bb * nt + i, 0)),
        out_shape=jax.ShapeDtypeStruct(x.shape, F32),
        compiler_params=_cp(("parallel", "parallel")), name=name)(x, z_tm.reshape(t, bsz * d), *fulls)


def _s5_out(x, z, w_o, w_gate, g, b, tm, name):
    def body(x_ref, z_ref, wo_ref, wg_ref, g_ref, b_ref, y_ref):
        z = z_ref[...]
        h = _dot(z, wo_ref[...]) * _sigmoid(_dot(z, wg_ref[...]))
        y_ref[...] = _ln(ALPHA * x_ref[...] + h, g_ref[...], b_ref[...])

    return _rowwise(body, [x, z], [w_o, w_gate, g.reshape(1, -1), b.reshape(1, -1)], [D_MODEL], tm, name)[0]


def _s5_prompt(x, bsz, t, a_re, a_im, log_dt, b_re, b_im, c_re, c_im, d_skip, w_o, w_gate, ln_g, ln_b):
    lr, li, bcat, ccat = _s5_weights(a_re, a_im, log_dt, b_re, b_im, c_re, c_im)
    zero = jnp.zeros((bsz, S5_H), F32)
    z_tm, s_re, s_im = _s5_scan(x, bsz, t, zero, zero, lr, li, bcat, ccat, d_skip, min(t, 64), "s5_scan")
    x1 = _s5_out_tm(x, z_tm, bsz, t, w_o, w_gate, ln_g, ln_b, min(t, 512), "s5_out")
    return x1, s_re.reshape(bsz, S5_GROUPS, S5_STATE), s_im.reshape(bsz, S5_GROUPS, S5_STATE)


GLA_PROJ = 2 * GLA_QK + 2 * GLA_V


def _gla_logd(x, w_in, w_gk2, b_gk, tm, name):
    w1 = jnp.pad(w_in[:, GLA_PROJ:], ((0, 0), (0, LANES - GLA_RANK)))
    w2 = jnp.pad(w_gk2, ((0, LANES - GLA_RANK), (0, 0)))

    def body(x_ref, w1_ref, w2_ref, b_ref, o_ref):
        y = _dot(_dot(x_ref[...], w1_ref[...]), w2_ref[...]) + b_ref[...]
        o_ref[...] = -_softplus(-y) * (1.0 / GLA_GATE_NORM)

    return _rowwise(body, [x], [w1, w2, b_gk.reshape(1, -1)], [GLA_QK], tm, name)[0]


def _gla_chunks(proj, logd, norm_g, bsz, t, name):
    c = GLA_CHUNK
    rows = 64
    nr = t // rows
    scale = GLA_DK ** -0.5

    def body(q_ref, k_ref, v_ref, gate_ref, ld_ref, ng_ref, o_ref, s_out_ref, st_ref):
        i = pl.program_id(1)

        @pl.when(i == 0)
        def _():
            st_ref[...] = jnp.zeros_like(st_ref)

        lblk = _tri(rows, block=c).astype(F32)
        bc_all = _dot3(lblk, ld_ref[...])
        incl = _tri(c)
        ng = ng_ref[...]
        subs = range(rows // c)
        heads = range(GLA_HEADS)
        rsl = lambda s: slice(s * c, (s + 1) * c)
        ksl = lambda h: slice(h * GLA_DK, (h + 1) * GLA_DK)
        vsl = lambda h: slice(h * GLA_DV, (h + 1) * GLA_DV)
        q_ins, k_outs, g_ends, a_s = {}, {}, {}, {}
        for s in subs:
            for h in heads:
                bc = bc_all[rsl(s), ksl(h)]
                k = k_ref[rsl(s), ksl(h)]
                bcl = bc[c - 1:c]
                q_ins[s, h] = q_ref[rsl(s), ksl(h)] * scale * jnp.exp(bc)
                k_outs[s, h] = k * jnp.exp(bcl - bc)
                g_ends[s, h] = jnp.exp(bcl)
                a_s[s, h] = jnp.where(incl, _dot_nt(q_ins[s, h], k * jnp.exp(-bc)), 0.0)
        o_intra = {(s, h): _dot(a_s[s, h], v_ref[rsl(s), vsl(h)]) for s in subs for h in heads}
        for s in subs:
            sts = [st_ref[h] for h in heads]
            o_inter = [_dot_nt(q_ins[s, h], sts[h]) for h in heads]
            kvs = [_dot_tn(v_ref[rsl(s), vsl(h)], k_outs[s, h]) for h in heads]
            for h in heads:
                st_ref[h] = sts[h] * g_ends[s, h] + kvs[h]
                o = o_inter[h] + o_intra[s, h]
                o = o * lax.rsqrt(jnp.mean(o * o, -1, keepdims=True) + 1e-6) * ng
                o_ref[rsl(s), vsl(h)] = o * _silu(gate_ref[rsl(s), vsl(h)])

        @pl.when(i == nr - 1)
        def _():
            for h in range(GLA_HEADS):
                s_out_ref[0, h] = st_ref[h].T

    return pl.pallas_call(
        body, grid=(bsz, nr),
        in_specs=[pl.BlockSpec((rows, GLA_QK), lambda b, i: (b * nr + i, 0)),
                  pl.BlockSpec((rows, GLA_QK), lambda b, i: (b * nr + i, 1)),
                  pl.BlockSpec((rows, GLA_V), lambda b, i: (b * nr + i, 1)),
                  pl.BlockSpec((rows, GLA_V), lambda b, i: (b * nr + i, 2)),
                  pl.BlockSpec((rows, GLA_QK), lambda b, i: (b * nr + i, 0)),
                  pl.BlockSpec((1, GLA_DV), lambda b, i: (0, 0))],
        out_specs=[pl.BlockSpec((rows, GLA_V), lambda b, i: (b * nr + i, 0)),
                   pl.BlockSpec((1, GLA_HEADS, GLA_DK, GLA_DV), lambda b, i: (b, 0, 0, 0))],
        out_shape=[jax.ShapeDtypeStruct((bsz * t, GLA_V), F32),
                   jax.ShapeDtypeStruct((bsz, GLA_HEADS, GLA_DK, GLA_DV), F32)],
        scratch_shapes=[pltpu.VMEM((GLA_HEADS, GLA_DV, GLA_DK), F32)],
        compiler_params=_cp(("parallel", "arbitrary")), name=name,
    )(proj, proj, proj, proj, logd, norm_g.reshape(1, -1))


def _gla_prompt(x, bsz, t, w_in, w_gk2, b_gk, norm_g, w_out, ln_g, ln_b):
    m = bsz * t
    tm = min(m, 512)
    proj = _mm(x, w_in, GLA_PROJ, min(m, 2048), 512, "gla_proj")
    logd = _gla_logd(x, w_in, w_gk2, b_gk, tm, "gla_logd")
    o, s = _gla_chunks(proj, logd, norm_g, bsz, t, "gla_chunks")
    x1 = _out_ln(x, o, w_out, ln_g, ln_b, tm, "gla_out")
    return x1, s


def _rwkv_rkv(x, xp, mu, w_rkv, tm, name):
    def body(x_ref, xp_ref, mu_ref, w_ref, r_ref, k_ref, v_ref):
        x = x_ref[...]
        dx = xp_ref[...] - x
        for s, o_ref in enumerate((r_ref, k_ref, v_ref)):
            o_ref[...] = _dot(x + dx * mu_ref[s:s + 1], w_ref[s])

    return _rowwise(body, [x, xp], [mu, w_rkv], [D_MODEL] * 3, tm, name)


def _rwkv_lora(x, xp, mu, w0, w_w1, w_w2, a0, w_a1, w_a2, w_g1, w_g2, tm, name):
    def body(x_ref, xp_ref, mu_ref, w0_ref, ww1, ww2, a0_ref, wa1, wa2, wg1, wg2, lw_ref, a_ref, g_ref):
        x = x_ref[...]
        dx = xp_ref[...] - x
        xs = lambda s: x + dx * mu_ref[s:s + 1]
        w_log = -_softplus(-(w0_ref[...] + _dot(jnp.tanh(_dot(xs(3), ww1[...])), ww2[...]))) - 0.5
        lw_ref[...] = -jnp.exp(w_log)
        a_ref[...] = _sigmoid(a0_ref[...] + _dot(_dot(xs(4), wa1[...]), wa2[...]))
        g_ref[...] = _dot(_sigmoid(_dot(xs(5), wg1[...])), wg2[...])

    fulls = [mu, w0.reshape(1, -1), w_w1, w_w2, a0.reshape(1, -1), w_a1, w_a2, w_g1, w_g2]
    return _rowwise(body, [x, xp], fulls, [D_MODEL] * 3, tm, name)


def _rwkv_head_inputs(r, k, v, a, kk_w, ka_w, sl):
    kraw = k[:, sl]
    kkn = _l2n(kraw * kk_w[:, sl], 1.0)
    ah = a[:, sl]
    kh = kraw * (1.0 + (ah - 1.0) * ka_w[:, sl])
    return r[:, sl], kh, v[:, sl], kkn, kkn * ah


def _rwkv_head_out(y, rh, kh, vh, g, rk_w, lng, lnb, sl):
    yc = y - jnp.mean(y, -1, keepdims=True)
    yn = yc * lax.rsqrt(jnp.mean(yc * yc, -1, keepdims=True) + RW_GN_EPS) * lng[:, sl] + lnb[:, sl]
    bonus = jnp.sum(rh * kh * rk_w[:, sl], -1, keepdims=True) * vh
    return (yn + bonus) * g[:, sl]


def _rwkv_chunks(r, k, v, lw, a, g, k_k, k_a, r_k, ln_g, ln_b, bsz, t, name):
    c = RW_CHUNK
    nc = t // c
    hd = RW_HD

    def body(r_ref, k_ref, v_ref, lw_ref, a_ref, g_ref, kk_ref, ka_ref, rk_ref, lng_ref, lnb_ref,
             o_ref, s_out_ref, s_ref):
        i = pl.program_id(1)

        @pl.when(i == 0)
        def _():
            s_ref[...] = jnp.zeros_like(s_ref)

        lw = lw_ref[...]
        gam = _dot3(_tri(c).astype(F32), lw)
        gprev = gam - lw
        r, k, v, a, g = r_ref[...], k_ref[...], v_ref[...], a_ref[...], g_ref[...]
        kk_w, ka_w, rk_w, lng, lnb = kk_ref[...], ka_ref[...], rk_ref[...], lng_ref[...], lnb_ref[...]
        strict = _tri(c, strict=True)
        incl = _tri(c)
        heads = range(RW_H)
        hsl = lambda n: slice(n * hd, (n + 1) * hd)
        hin, ars, bks, bkos, g_ends = [], [], [], [], []
        for h in heads:
            rh, kh, vh, kkn, bh = _rwkv_head_inputs(r, k, v, a, kk_w, ka_w, hsl(h))
            hin.append((rh, kh, vh))
            gm, gp = gam[:, hsl(h)], gprev[:, hsl(h)]
            glast = gm[c - 1:c]
            e_neg = jnp.exp(-gm)
            e_out = jnp.exp(glast - gm)
            ars.append(jnp.concatenate([kkn * jnp.exp(gp), rh * jnp.exp(gm)], axis=0))
            bks.append(jnp.concatenate([bh * e_neg, kh * e_neg], axis=0))
            bkos.append(jnp.concatenate([bh * e_out, kh * e_out], axis=0))
            g_ends.append(jnp.exp(glast))
        gmats = [_dot_nt(ars[h], bks[h]) for h in heads]
        tinvs = _neumann_inv([jnp.where(strict, gm_[:c, :c], 0.0) for gm_ in gmats], c)
        makvs = [_dot(jnp.where(strict, gmats[h][:c, c:], 0.0), hin[h][2]) for h in heads]
        rbks = [jnp.concatenate([jnp.where(incl, gm_[c:, :c], 0.0), jnp.where(incl, gm_[c:, c:], 0.0)], axis=1)
                for gm_ in gmats]
        a_ss = [_dot_nt(ars[h], s_ref[h]) for h in heads]
        uvs = [jnp.concatenate([_dot(tinvs[h], -a_ss[h][:c] - makvs[h]), hin[h][2]], axis=0) for h in heads]
        ys = [a_ss[h][c:] + _dot(rbks[h], uvs[h]) for h in heads]
        svs = [_dot_tn(uvs[h], bkos[h]) for h in heads]
        for h in heads:
            s_ref[h] = s_ref[h] * g_ends[h] + svs[h]
            rh, kh, vh = hin[h]
            o_ref[:, hsl(h)] = _rwkv_head_out(ys[h], rh, kh, vh, g, rk_w, lng, lnb, hsl(h))

        @pl.when(i == nc - 1)
        def _():
            s_out_ref[0] = s_ref[...]

    row = lambda w: w.reshape(1, D_MODEL)
    blk = pl.BlockSpec((c, D_MODEL), lambda b, i: (b * nc + i, 0))
    par = pl.BlockSpec((1, D_MODEL), lambda b, i: (0, 0))
    return pl.pallas_call(
        body, grid=(bsz, nc),
        in_specs=[blk] * 6 + [par] * 5,
        out_specs=[blk, pl.BlockSpec((1, RW_H, hd, hd), lambda b, i: (b, 0, 0, 0))],
        out_shape=[jax.ShapeDtypeStruct((bsz * t, D_MODEL), F32), jax.ShapeDtypeStruct((bsz, RW_H, hd, hd), F32)],
        scratch_shapes=[pltpu.VMEM((RW_H, hd, hd), F32)],
        compiler_params=_cp(("parallel", "arbitrary")), name=name,
    )(r, k, v, lw, a, g, row(k_k), row(k_a), row(r_k), row(ln_g), row(ln_b))


def _rwkv_prompt(x, bsz, t, mu, w_rkv, w0, w_w1, w_w2, a0, w_a1, w_a2, w_g1, w_g2, k_k, k_a, r_k, gn_g, gn_b, w_o,
                 ln_g, ln_b):
    m = bsz * t
    x3 = x.reshape(bsz, t, D_MODEL)
    xp = jnp.concatenate([jnp.zeros((bsz, 1, D_MODEL), F32), x3[:, :-1]], axis=1).reshape(m, D_MODEL)
    tm = min(m, 256)
    r, k, v = _rwkv_rkv(x, xp, mu, w_rkv, tm, "rwkv_rkv")
    lw, a, g = _rwkv_lora(x, xp, mu, w0, w_w1, w_w2, a0, w_a1, w_a2, w_g1, w_g2, tm, "rwkv_lora")
    y, s = _rwkv_chunks(r, k, v, lw, a, g, k_k, k_a, r_k, gn_g, gn_b, bsz, t, "rwkv_chunks")
    x1 = _out_ln(x, y, w_o, ln_g, ln_b, min(m, 512), "rwkv_out")
    return x1, x3[:, -1], s


def _eye(n):
    return _iota2((n, n), 0) == _iota2((n, n), 1)


def _to_col(row, eye):
    return jnp.sum(jnp.where(eye, row, 0.0), axis=1, keepdims=True)


def _to_row(col, eye):
    return jnp.sum(jnp.where(eye, col, 0.0), axis=0, keepdims=True)


def _row3(a):
    return a.reshape(a.shape[0], 1, a.shape[1])


def _rows_spec(width, col_block=0):
    return pl.BlockSpec((1, 1, width), lambda b: (b, 0, col_block))


def _gdn_conv_step(proj, buf, conv_w, name):
    def body(p_ref, b0, b1, b2, w_ref, o_ref):
        w = w_ref[...]
        y = p_ref[:, :GDN_CONV_DIM] * w[3:4] + b2[...] * w[2:3] + b1[...] * w[1:2] + b0[...] * w[0:1]
        _gdn_act(y, o_ref)

    rows = [proj, buf[:, 0], buf[:, 1], buf[:, 2]]
    return _rowwise(body, rows, [conv_w], [GDN_CONV_DIM], proj.shape[0], name)[0]


def _gdn_step(qkv, bg, proj, s0, norm_g, name):
    bsz = qkv.shape[0]
    rep = GDN_VH // GDN_KH

    def body(q_ref, k_ref, v_ref, bg_ref, z_ref, ng_ref, s_ref, o_ref, so_ref):
        eye = _eye(GDN_HD)
        bgv = bg_ref[0]
        ng = ng_ref[...]
        for kh in range(GDN_KH):
            sl = slice(kh * GDN_HD, (kh + 1) * GDN_HD)
            qrow, krow = q_ref[0, :, sl], k_ref[0, :, sl]
            qcol, kcol = _to_col(qrow, eye), _to_col(krow, eye)
            qk = jnp.sum(qrow * krow, axis=1, keepdims=True)
            for h in range(kh * rep, (kh + 1) * rep):
                vs = slice(h * GDN_HD, (h + 1) * GDN_HD)
                beta = bgv[:, h:h + 1]
                eg = jnp.exp(bgv[:, GDN_VH + h:GDN_VH + h + 1])
                s = s_ref[0, h]
                ks = jnp.sum(kcol * s, axis=0, keepdims=True)
                qs = jnp.sum(qcol * s, axis=0, keepdims=True)
                v_new = beta * (v_ref[0, :, vs] - eg * ks)
                o = eg * qs + qk * v_new
                so_ref[0, h] = s * eg + kcol * v_new
                o = o * lax.rsqrt(jnp.mean(o * o, -1, keepdims=True) + 1e-6) * ng
                o_ref[0, :, vs] = o * _silu(z_ref[0, :, vs])

    st_spec = pl.BlockSpec((1, GDN_VH, GDN_HD, GDN_HD), lambda b: (b, 0, 0, 0))
    o, s = pl.pallas_call(
        body, grid=(bsz,),
        in_specs=[_rows_spec(GDN_QK, 0), _rows_spec(GDN_QK, 1), _rows_spec(GDN_V, 1), _rows_spec(LANES),
                  _rows_spec(GDN_V, GDN_CONV_DIM // GDN_V), pl.BlockSpec((1, GDN_HD), lambda b: (0, 0)), st_spec],
        out_specs=[_rows_spec(GDN_V), st_spec],
        out_shape=[jax.ShapeDtypeStruct((bsz, 1, GDN_V), F32), jax.ShapeDtypeStruct(s0.shape, F32)],
        compiler_params=_cp(("parallel",)), name=name,
    )(_row3(qkv), _row3(qkv), _row3(qkv), _row3(bg), _row3(proj), norm_g.reshape(1, -1), s0)
    return o.reshape(bsz, GDN_V), s


def _gdn_sample(x, buf, s0, w_in, conv_w, a_log, dt_bias, norm_g, w_out, ln_g, ln_b):
    m = x.shape[0]
    proj = _mm(x, w_in, GDN_CONV_DIM + GDN_V, m, 512, "gdn_proj_s")
    bg = _gdn_gates(x, w_in, a_log, dt_bias, m, "gdn_gates_s")
    qkv = _gdn_conv_step(proj, buf, conv_w, "gdn_conv_s")
    o, s = _gdn_step(qkv, bg, proj, s0, norm_g, "gdn_step_s")
    x1 = _out_ln(x, o, w_out, ln_g, ln_b, m, "gdn_out_s")
    new_buf = jnp.concatenate([buf[:, 1:], proj[:, None, :GDN_CONV_DIM]], axis=1)
    return x1, new_buf, s


def _s5_step(x, s_re, s_im, lr, li, bcat, ccat, d_skip, name):
    def body(x_ref, sr_ref, si_ref, lr_ref, li_ref, b_ref, c_ref, d_ref, z_ref, hr_ref, hi_ref):
        x = x_ref[...]
        dsk = d_ref[...]
        for j in range(S5_NT):
            hs = slice(j * S5_HT, (j + 1) * S5_HT)
            xs = x[:, j * LANES:(j + 1) * LANES]
            bu = _dot(xs, b_ref[j])
            lam_r, lam_i = lr_ref[:, hs], li_ref[:, hs]
            sr, si = sr_ref[:, hs], si_ref[:, hs]
            h_re = lam_r * sr - lam_i * si + bu[:, :S5_HT]
            h_im = lam_r * si + lam_i * sr + bu[:, S5_HT:]
            hr_ref[:, hs] = h_re
            hi_ref[:, hs] = h_im
            cj = c_ref[j]
            y = _dot(h_re, cj[:S5_HT]) + _dot(h_im, cj[S5_HT:])
            z_ref[:, j * LANES:(j + 1) * LANES] = _gelu(y + dsk[:, j * LANES:(j + 1) * LANES] * xs)

    fulls = [lr, li, bcat, ccat, d_skip.reshape(1, -1)]
    return _rowwise(body, [x, s_re, s_im], fulls, [D_MODEL, S5_H, S5_H], x.shape[0], name)


def _s5_sample(x, s_re0, s_im0, a_re, a_im, log_dt, b_re, b_im, c_re, c_im, d_skip, w_o, w_gate, ln_g, ln_b):
    m = x.shape[0]
    lr, li, bcat, ccat = _s5_weights(a_re, a_im, log_dt, b_re, b_im, c_re, c_im)
    z, h_re, h_im = _s5_step(x, s_re0.reshape(m, S5_H), s_im0.reshape(m, S5_H), lr, li, bcat, ccat, d_skip, "s5_step_s")
    x1 = _s5_out(x, z, w_o, w_gate, ln_g, ln_b, m, "s5_out_s")
    return x1, h_re.reshape(m, S5_GROUPS, S5_STATE), h_im.reshape(m, S5_GROUPS, S5_STATE)


def _gla_step(proj, logd, s0, norm_g, name):
    bsz = proj.shape[0]
    scale = GLA_DK ** -0.5

    def body(q_ref, k_ref, v_ref, gate_ref, ld_ref, ng_ref, s_ref, o_ref, so_ref):
        eye = _eye(GLA_DK)
        ng = ng_ref[...]
        for h in range(GLA_HEADS):
            ks = slice(h * GLA_DK, (h + 1) * GLA_DK)
            vs = slice(h * GLA_DV, (h + 1) * GLA_DV)
            bc = ld_ref[0, :, ks]
            k = k_ref[0, :, ks]
            v = v_ref[0, :, vs]
            q_in = q_ref[0, :, ks] * scale * jnp.exp(bc)
            a = jnp.sum(q_in * (k * jnp.exp(-bc)), axis=1, keepdims=True)
            s = s_ref[0, h]
            o = jnp.sum(_to_col(q_in, eye) * s, axis=0, keepdims=True) + a * v
            so_ref[0, h] = s * _to_col(jnp.exp(bc), eye) + _to_col(k, eye) * v
            o = o * lax.rsqrt(jnp.mean(o * o, -1, keepdims=True) + 1e-6) * ng
            o_ref[0, :, vs] = o * _silu(gate_ref[0, :, vs])

    st_spec = pl.BlockSpec((1, GLA_HEADS, GLA_DK, GLA_DV), lambda b: (b, 0, 0, 0))
    o, s = pl.pallas_call(
        body, grid=(bsz,),
        in_specs=[_rows_spec(GLA_QK, 0), _rows_spec(GLA_QK, 1), _rows_spec(GLA_V, 1), _rows_spec(GLA_V, 2),
                  _rows_spec(GLA_QK), pl.BlockSpec((1, GLA_DV), lambda b: (0, 0)), st_spec],
        out_specs=[_rows_spec(GLA_V), st_spec],
        out_shape=[jax.ShapeDtypeStruct((bsz, 1, GLA_V), F32), jax.ShapeDtypeStruct(s0.shape, F32)],
        compiler_params=_cp(("parallel",)), name=name,
    )(_row3(proj), _row3(proj), _row3(proj), _row3(proj), _row3(logd), norm_g.reshape(1, -1), s0)
    return o.reshape(bsz, GLA_V), s


def _gla_sample(x, s0, w_in, w_gk2, b_gk, norm_g, w_out, ln_g, ln_b):
    m = x.shape[0]
    proj = _mm(x, w_in, GLA_PROJ, m, 512, "gla_proj_s")
    logd = _gla_logd(x, w_in, w_gk2, b_gk, m, "gla_logd_s")
    o, s = _gla_step(proj, logd, s0, norm_g, "gla_step_s")
    x1 = _out_ln(x, o, w_out, ln_g, ln_b, m, "gla_out_s")
    return x1, s


def _rwkv_step(r, k, v, lw, a, g, s0, k_k, k_a, r_k, ln_g, ln_b, name):
    bsz = r.shape[0]
    hd = RW_HD

    def body(r_ref, k_ref, v_ref, lw_ref, a_ref, g_ref, kk_ref, ka_ref, rk_ref, lng_ref, lnb_ref, s_ref,
             o_ref, so_ref):
        eye = _eye(hd)
        r, k, v, a, g = r_ref[0], k_ref[0], v_ref[0], a_ref[0], g_ref[0]
        dec = jnp.exp(lw_ref[0])
        kk_w, ka_w, rk_w, lng, lnb = kk_ref[...], ka_ref[...], rk_ref[...], lng_ref[...], lnb_ref[...]
        heads = range(RW_H)
        hsl = lambda n: slice(n * hd, (n + 1) * hd)
        hin = [_rwkv_head_inputs(r, k, v, a, kk_w, ka_w, hsl(h)) for h in heads]
        ss = [s_ref[0, h] for h in heads]
        sas = [-jnp.sum(ss[h] * hin[h][3], axis=1, keepdims=True) for h in heads]
        vcols = [_to_col(hin[h][2], eye) for h in heads]
        s_news = [ss[h] * dec[:, hsl(h)] + sas[h] * hin[h][4] + vcols[h] * hin[h][1] for h in heads]
        ycols = [jnp.sum(s_news[h] * hin[h][0], axis=1, keepdims=True) for h in heads]
        yrows = [_to_row(ycols[h], eye) for h in heads]
        for h in heads:
            rh, kh, vh, _, _ = hin[h]
            so_ref[0, h] = s_news[h]
            o_ref[0, :, hsl(h)] = _rwkv_head_out(yrows[h], rh, kh, vh, g, rk_w, lng, lnb, hsl(h))

    row = lambda w: w.reshape(1, D_MODEL)
    par = pl.BlockSpec((1, D_MODEL), lambda b: (0, 0))
    st_spec = pl.BlockSpec((1, RW_H, hd, hd), lambda b: (b, 0, 0, 0))
    o, s = pl.pallas_call(
        body, grid=(bsz,),
        in_specs=[_rows_spec(D_MODEL)] * 6 + [par] * 5 + [st_spec],
        out_specs=[_rows_spec(D_MODEL), st_spec],
        out_shape=[jax.ShapeDtypeStruct((bsz, 1, D_MODEL), F32), jax.ShapeDtypeStruct(s0.shape, F32)],
        compiler_params=_cp(("parallel",)), name=name,
    )(*[_row3(t) for t in (r, k, v, lw, a, g)], row(k_k), row(k_a), row(r_k), row(ln_g), row(ln_b), s0)
    return o.reshape(bsz, D_MODEL), s


def _rwkv_sample(x, shift0, s0, mu, w_rkv, w0, w_w1, w_w2, a0, w_a1, w_a2, w_g1, w_g2, k_k, k_a, r_k, gn_g, gn_b, w_o,
                 ln_g, ln_b):
    m = x.shape[0]
    r, k, v = _rwkv_rkv(x, shift0, mu, w_rkv, m, "rwkv_rkv_s")
    lw, a, g = _rwkv_lora(x, shift0, mu, w0, w_w1, w_w2, a0, w_a1, w_a2, w_g1, w_g2, m, "rwkv_lora_s")
    y, s = _rwkv_step(r, k, v, lw, a, g, s0, k_k, k_a, r_k, gn_g, gn_b, "rwkv_step_s")
    x1 = _out_ln(x, y, w_o, ln_g, ln_b, m, "rwkv_out_s")
    return x1, x, s


def kernel(x_prompt, x_sample, state_gdn_conv, state_gdn, state_s5_re, state_s5_im, state_gla, state_rwkv_shift,
           state_rwkv, p_prompt, p_sample, gdn_w_in, gdn_conv_w, gdn_a_log, gdn_dt_bias, gdn_norm_g, gdn_w_out,
           s5_a_re, s5_a_im, s5_log_dt, s5_b_re, s5_b_im, s5_c_re, s5_c_im, s5_d, s5_w_o, s5_w_gate,
           gla_w_in, gla_w_gk2, gla_b_gk, gla_norm_g, gla_w_out,
           rwkv_mu, rwkv_w_rkv, rwkv_w0, rwkv_w_w1, rwkv_w_w2, rwkv_a0, rwkv_w_a1, rwkv_w_a2, rwkv_w_g1, rwkv_w_g2,
           rwkv_k_k, rwkv_k_a, rwkv_r_k, rwkv_ln_g, rwkv_ln_b, rwkv_w_o,
           ln_mix_g, ln_mix_b, ln_ffn_g, ln_ffn_b, mlp_w1, mlp_w2, ple_w, ple_gate_w):
    bsz, t, d = x_prompt.shape
    bs = x_sample.shape[0]
    gdn_w = (gdn_w_in[0], gdn_conv_w[0], gdn_a_log[0], gdn_dt_bias[0], gdn_norm_g[0], gdn_w_out[0])
    s5_w = (s5_a_re[0], s5_a_im[0], s5_log_dt[0], s5_b_re[0], s5_b_im[0], s5_c_re[0], s5_c_im[0], s5_d[0],
            s5_w_o[0], s5_w_gate[0])
    gla_w = (gla_w_in[0], gla_w_gk2[0], gla_b_gk[0], gla_norm_g[0], gla_w_out[0])
    rwkv_w = (rwkv_mu[0], rwkv_w_rkv[0], rwkv_w0[0], rwkv_w_w1[0], rwkv_w_w2[0], rwkv_a0[0], rwkv_w_a1[0],
              rwkv_w_a2[0], rwkv_w_g1[0], rwkv_w_g2[0], rwkv_k_k[0], rwkv_k_a[0], rwkv_r_k[0], rwkv_ln_g[0],
              rwkv_ln_b[0], rwkv_w_o[0])

    def ffn(x, p, i, tm, tag):
        return _mlp_ple(x, p, mlp_w1[i], mlp_w2[i], ln_ffn_g[i], ln_ffn_b[i], ple_w[i], ple_gate_w[i], tm, 512,
                        f"mlp{i}_{tag}")

    xp = x_prompt.reshape(bsz * t, d)
    pp = p_prompt.reshape(DEPTH, bsz * t, D_PLE)
    tm_p = min(bsz * t, 1024)
    xp, gc_p, gs_p = _gdn_prompt(xp, bsz, t, *gdn_w, ln_mix_g[0], ln_mix_b[0])
    xp = ffn(xp, pp[0], 0, tm_p, "p")
    xp, sr_p, si_p = _s5_prompt(xp, bsz, t, *s5_w, ln_mix_g[1], ln_mix_b[1])
    xp = ffn(xp, pp[1], 1, tm_p, "p")
    xp, la_p = _gla_prompt(xp, bsz, t, *gla_w, ln_mix_g[2], ln_mix_b[2])
    xp = ffn(xp, pp[2], 2, tm_p, "p")
    xp, sh_p, rs_p = _rwkv_prompt(xp, bsz, t, *rwkv_w, ln_mix_g[3], ln_mix_b[3])
    xp = ffn(xp, pp[3], 3, tm_p, "p")

    xs = x_sample.reshape(bs, d)
    ps = p_sample.reshape(DEPTH, bs, D_PLE)
    xs, gc_s, gs_s = _gdn_sample(xs, state_gdn_conv[0], state_gdn[0], *gdn_w, ln_mix_g[0], ln_mix_b[0])
    xs = ffn(xs, ps[0], 0, bs, "s")
    xs, sr_s, si_s = _s5_sample(xs, state_s5_re[0], state_s5_im[0], *s5_w, ln_mix_g[1], ln_mix_b[1])
    xs = ffn(xs, ps[1], 1, bs, "s")
    xs, la_s = _gla_sample(xs, state_gla[0], *gla_w, ln_mix_g[2], ln_mix_b[2])
    xs = ffn(xs, ps[2], 2, bs, "s")
    xs, sh_s, rs_s = _rwkv_sample(xs, state_rwkv_shift[0], state_rwkv[0], *rwkv_w, ln_mix_g[3], ln_mix_b[3])
    xs = ffn(xs, ps[3], 3, bs, "s")

    e = lambda a: a[None]
    return (xp.reshape(bsz, t, d), xs.reshape(bs, 1, d), e(gc_p), e(gc_s), e(gs_p), e(gs_s), e(sr_p), e(sr_s),
            e(si_p), e(si_s), e(la_p), e(la_s), e(sh_p), e(sh_s), e(rs_p), e(rs_s))
```

```python
import functools
import math

import jax
import jax.numpy as jnp
from jax import lax
from jax.experimental import pallas as pl
from jax.experimental.pallas import tpu as pltpu

F32 = jnp.float32
BF16 = jnp.bfloat16

D_MODEL = 1024
DEPTH = 4
D_PLE = 256
D_FF = 4 * D_MODEL
LN_EPS = 1e-5
ALPHA = (2.0 * DEPTH) ** 0.25

GDN_HD = 128
GDN_KH = 8
GDN_VH = 16
GDN_QK = GDN_KH * GDN_HD
GDN_V = GDN_VH * GDN_HD
GDN_CONV_DIM = 2 * GDN_QK + GDN_V
GDN_CONV_W = 4
GDN_CHUNK = 64

S5_GROUP = 16
S5_GROUPS = D_MODEL // S5_GROUP
S5_STATE = 64
S5_H = S5_GROUPS * S5_STATE

GLA_HEADS = 4
GLA_DK = 128
GLA_DV = 256
GLA_QK = GLA_HEADS * GLA_DK
GLA_V = GLA_HEADS * GLA_DV
GLA_RANK = 16
GLA_GATE_NORM = 16.0
GLA_CHUNK = 16

RW_HD = 64
RW_H = D_MODEL // RW_HD
RW_GN_EPS = 64e-5
RW_CHUNK = 64

LANES = 128
SUBLANES = 8
VMEM_LIMIT = 56 * 1024 * 1024


def _cp(sem, vmem=VMEM_LIMIT):
    return pltpu.CompilerParams(dimension_semantics=sem, vmem_limit_bytes=vmem)


def _dot(a, b):
    return jnp.dot(a, b, preferred_element_type=F32)


def _dot_nt(a, b):
    return lax.dot_general(a, b, (((1,), (1,)), ((), ())), preferred_element_type=F32)


def _dot_tn(a, b):
    return lax.dot_general(a, b, (((0,), (0,)), ((), ())), preferred_element_type=F32)


def _dot3(l, x):
    hi = x.astype(BF16).astype(F32)
    r1 = x - hi
    mid = r1.astype(BF16).astype(F32)
    lo = r1 - mid
    return _dot(l, hi) + _dot(l, mid) + _dot(l, lo)


def _softplus(x):
    return jnp.maximum(x, 0.0) + jnp.log(1.0 + jnp.exp(-jnp.abs(x)))


def _sigmoid(x):
    return 1.0 / (1.0 + jnp.exp(-x))


def _silu(x):
    return x * _sigmoid(x)


def _ln(x, g, b):
    xc = x - jnp.mean(x, -1, keepdims=True)
    var = jnp.mean(xc * xc, -1, keepdims=True)
    return xc * lax.rsqrt(var + LN_EPS) * g + b


def _iota2(shape, axis):
    return lax.broadcasted_iota(jnp.int32, shape, axis)


def _tri(n, strict=False, block=None):
    r = _iota2((n, n), 0)
    c = _iota2((n, n), 1)
    m = (r > c) if strict else (r >= c)
    if block is not None:
        m = m & ((r // block) == (c // block))
    return m


def _full_spec(a):
    n = a.ndim
    return pl.BlockSpec(a.shape, lambda *_: (0,) * n)


def _mm(x, w, n_cols, tm, tn, name):
    m, k = x.shape

    def body(x_ref, w_ref, o_ref):
        o_ref[...] = _dot(x_ref[...], w_ref[...])

    return pl.pallas_call(
        body, grid=(m // tm, n_cols // tn),
        in_specs=[pl.BlockSpec((tm, k), lambda i, j: (i, 0)), pl.BlockSpec((k, tn), lambda i, j: (0, j))],
        out_specs=pl.BlockSpec((tm, tn), lambda i, j: (i, j)),
        out_shape=jax.ShapeDtypeStruct((m, n_cols), F32),
        compiler_params=_cp(("parallel", "arbitrary")), name=name)(x, w)


def _rowwise(body, rows, fulls, out_cols, tm, name):
    m = rows[0].shape[0]
    in_specs = [pl.BlockSpec((tm, r.shape[1]), lambda i: (i, 0)) for r in rows] + [_full_spec(f) for f in fulls]
    out_specs = [pl.BlockSpec((tm, c), lambda i: (i, 0)) for c in out_cols]
    out_shape = [jax.ShapeDtypeStruct((m, c), F32) for c in out_cols]
    res = pl.pallas_call(body, grid=(m // tm,), in_specs=in_specs, out_specs=out_specs, out_shape=out_shape,
                         compiler_params=_cp(("parallel",)), name=name)(*rows, *fulls)
    return res


def _out_ln(x, o, w_out, g, b, tm, name):
    def body(x_ref, o_ref, w_ref, g_ref, b_ref, y_ref):
        y_ref[...] = _ln(ALPHA * x_ref[...] + _dot(o_ref[...], w_ref[...]), g_ref[...], b_ref[...])

    return _rowwise(body, [x, o], [w_out, g.reshape(1, -1), b.reshape(1, -1)], [D_MODEL], tm, name)[0]


def _resident_spec(a):
    n = a.ndim
    return pl.BlockSpec(a.shape, lambda *_: (0,) * n, pipeline_mode=pl.Buffered(1))


def _mlp_ple(x1, p, w1, w2, g, b, ple_w, gate_w, tm, tf, name):
    m = x1.shape[0]
    nf = D_FF // tf

    def body(x_ref, p_ref, w1_ref, w2_ref, g_ref, b_ref, pw_ref, gw_ref, o_ref, h_ref):
        x = x_ref[...]
        xb = x.astype(BF16)
        for f in range(nf):
            h = jnp.maximum(_dot(xb, w1_ref[:, f * tf:(f + 1) * tf]), 0.0)
            h_ref[:, f * tf:(f + 1) * tf] = (h * h).astype(BF16)
        x2 = _ln(ALPHA * x + _dot(h_ref[...], w2_ref[...]), g_ref[...], b_ref[...])
        gate = _sigmoid(_dot(x2.astype(BF16), gw_ref[...]))
        o_ref[...] = x2 + _dot(p_ref[...].astype(BF16), pw_ref[...]) * gate

    fulls = (w1.astype(BF16), w2.astype(BF16), g.reshape(1, -1), b.reshape(1, -1), ple_w.astype(BF16),
             gate_w.astype(BF16))
    return pl.pallas_call(
        body, grid=(m // tm,),
        in_specs=[pl.BlockSpec((tm, D_MODEL), lambda i: (i, 0)), pl.BlockSpec((tm, D_PLE), lambda i: (i, 0))]
        + [_resident_spec(a) for a in fulls],
        out_specs=pl.BlockSpec((tm, D_MODEL), lambda i: (i, 0)),
        out_shape=jax.ShapeDtypeStruct((m, D_MODEL), F32),
        scratch_shapes=[pltpu.VMEM((tm, D_FF), BF16)],
        compiler_params=_cp(("parallel",)), name=name,
    )(x1, p, *fulls)


def _gdn_gates(x, w_in, a_log, dt_bias, tm, name):
    w_ba = jnp.pad(w_in[:, GDN_CONV_DIM + GDN_V:], ((0, 0), (0, LANES - 2 * GDN_VH)))
    pad = lambda v: jnp.pad(v.reshape(1, -1), ((0, 0), (GDN_VH, LANES - 2 * GDN_VH)))

    def body(x_ref, w_ref, al_ref, dt_ref, o_ref):
        y = _dot(x_ref[...], w_ref[...])
        lane = _iota2(y.shape, 1)
        g = -jnp.exp(al_ref[...]) * _softplus(y + dt_ref[...])
        o_ref[...] = jnp.where(lane < GDN_VH, _sigmoid(y), g)

    return _rowwise(body, [x], [w_ba, pad(a_log), pad(dt_bias)], [LANES], tm, name)[0]


def _l2n(x, scale):
    return x * (lax.rsqrt(jnp.sum(x * x, -1, keepdims=True) + 1e-6) * scale)


def _gdn_act(y, o_ref):
    y = _silu(y)
    for h in range(GDN_KH):
        sl = slice(h * GDN_HD, (h + 1) * GDN_HD)
        o_ref[:, sl] = _l2n(y[:, sl], GDN_HD ** -0.5)
        sl = slice(GDN_QK + h * GDN_HD, GDN_QK + (h + 1) * GDN_HD)
        o_ref[:, sl] = _l2n(y[:, sl], 1.0)
    o_ref[:, 2 * GDN_QK:] = y[:, 2 * GDN_QK:]


def _gdn_conv_prompt(proj, conv_w, bsz, t, tt, name):
    nt = t // tt
    c = GDN_CONV_DIM

    def body(x_ref, w_ref, o_ref, tail_ref, carry_ref):
        i = pl.program_id(1)

        @pl.when(i == 0)
        def _():
            carry_ref[...] = jnp.zeros_like(carry_ref)

        x = x_ref[...]
        w = w_ref[...]
        y = x * w[3:4]
        x8 = x[:SUBLANES]
        c8 = carry_ref[...]
        row8 = _iota2(x8.shape, 0)
        y8 = x8 * w[3:4]
        for j in range(1, GDN_CONV_W):
            wj = w[3 - j:4 - j]
            y = y + pltpu.roll(x, j, 0) * wj
            y8 = y8 + jnp.where(row8 < j, pltpu.roll(c8, j, 0), pltpu.roll(x8, j, 0)) * wj
        _gdn_act(y, o_ref)
        _gdn_act(y8, o_ref.at[pl.ds(0, SUBLANES)])
        carry_ref[...] = x[tt - SUBLANES:]
        tail_ref[0] = x[tt - SUBLANES:]

    return pl.pallas_call(
        body, grid=(bsz, nt),
        in_specs=[pl.BlockSpec((tt, c), lambda b, i: (b * nt + i, 0)), _full_spec(conv_w)],
        out_specs=[pl.BlockSpec((tt, c), lambda b, i: (b * nt + i, 0)),
                   pl.BlockSpec((1, SUBLANES, c), lambda b, i: (b, 0, 0))],
        out_shape=[jax.ShapeDtypeStruct((bsz * t, c), F32), jax.ShapeDtypeStruct((bsz, SUBLANES, c), F32)],
        scratch_shapes=[pltpu.VMEM((SUBLANES, c), F32)],
        compiler_params=_cp(("parallel", "arbitrary")), name=name)(proj, conv_w)


def _neumann_inv(ms, n):
    eye = (_iota2((n, n), 0) == _iota2((n, n), 1)).astype(F32)
    ts = [eye - m for m in ms]
    ps = [_dot(m, m) for m in ms]
    k = 2
    while True:
        ts = [t + _dot(t, p) for t, p in zip(ts, ps)]
        k *= 2
        if k >= n:
            return ts
        ps = [_dot(p, p) for p in ps]


def _gdn_chunks(qkv, bg, proj, norm_g, bsz, t, name):
    c = GDN_CHUNK
    nc = t // c
    z_blk = GDN_CONV_DIM // GDN_V

    def body(q_ref, k_ref, v_ref, bg_ref, z_ref, ng_ref, o_ref, s_out_ref, s_ref):
        i = pl.program_id(1)

        @pl.when(i == 0)
        def _():
            s_ref[...] = jnp.zeros_like(s_ref)

        bgv = bg_ref[...]
        ltri = _tri(c).astype(F32)
        gc = _dot3(ltri, bgv)
        gct = jnp.concatenate([gc, jnp.zeros_like(gc)], axis=0).T
        incl = _tri(c)
        strict = _tri(c, strict=True)
        ng = ng_ref[...]
        rep = GDN_VH // GDN_KH
        heads = range(GDN_VH)
        hsl = lambda n: slice(n * GDN_HD, (n + 1) * GDN_HD)
        gram = [_dot_nt(jnp.concatenate([k_ref[:, hsl(n)], q_ref[:, hsl(n)]], axis=0), k_ref[:, hsl(n)])
                for n in range(GDN_KH)]
        ms, aqks, rhss, q_ins, k_outs, g_ends = [], [], [], [], [], []
        for h in heads:
            kh = h // rep
            kk = k_ref[:, hsl(kh)]
            beta = bgv[:, h:h + 1]
            gcol = gc[:, GDN_VH + h:GDN_VH + h + 1]
            grow = gct[GDN_VH + h:GDN_VH + h + 1, :c]
            decay = jnp.where(incl, jnp.exp(jnp.where(incl, gcol - grow, 0.0)), 0.0)
            ms.append(jnp.where(strict, gram[kh][:c] * beta * decay, 0.0))
            aqks.append(gram[kh][c:] * decay)
            egc = jnp.exp(gcol)
            rhss.append(jnp.concatenate([v_ref[:, hsl(h)] * beta, kk * (beta * egc)], axis=1))
            q_ins.append(q_ref[:, hsl(kh)] * egc)
            glast = gc[c - 1:c, GDN_VH + h:GDN_VH + h + 1]
            k_outs.append(kk * jnp.exp(glast - gcol))
            g_ends.append(jnp.exp(glast))
        tinvs = _neumann_inv(ms, c)
        sols = [_dot(tinvs[h], rhss[h]) for h in heads]
        wqs = [_dot(jnp.concatenate([sols[h][:, GDN_HD:], q_ins[h]], axis=0), s_ref[h]) for h in heads]
        v_news = [sols[h][:, :GDN_HD] - wqs[h][:c] for h in heads]
        avs = [_dot(aqks[h], v_news[h]) for h in heads]
        kvs = [_dot_tn(k_outs[h], v_news[h]) for h in heads]
        for h in heads:
            s_ref[h] = s_ref[h] * g_ends[h] + kvs[h]
            o = wqs[h][c:] + avs[h]
            o = o * lax.rsqrt(jnp.mean(o * o, -1, keepdims=True) + 1e-6) * ng
            o_ref[:, hsl(h)] = o * _silu(z_ref[:, hsl(h)])

        @pl.when(i == nc - 1)
        def _():
            s_out_ref[0] = s_ref[...]

    return pl.pallas_call(
        body, grid=(bsz, nc),
        in_specs=[pl.BlockSpec((c, GDN_QK), lambda b, i: (b * nc + i, 0)),
                  pl.BlockSpec((c, GDN_QK), lambda b, i: (b * nc + i, 1)),
                  pl.BlockSpec((c, GDN_V), lambda b, i: (b * nc + i, 1)),
                  pl.BlockSpec((c, LANES), lambda b, i: (b * nc + i, 0)),
                  pl.BlockSpec((c, GDN_V), lambda b, i: (b * nc + i, z_blk)),
                  pl.BlockSpec((1, GDN_HD), lambda b, i: (0, 0))],
        out_specs=[pl.BlockSpec((c, GDN_V), lambda b, i: (b * nc + i, 0)),
                   pl.BlockSpec((1, GDN_VH, GDN_HD, GDN_HD), lambda b, i: (b, 0, 0, 0))],
        out_shape=[jax.ShapeDtypeStruct((bsz * t, GDN_V), F32),
                   jax.ShapeDtypeStruct((bsz, GDN_VH, GDN_HD, GDN_HD), F32)],
        scratch_shapes=[pltpu.VMEM((GDN_VH, GDN_HD, GDN_HD), F32)],
        compiler_params=_cp(("parallel", "arbitrary")), name=name,
    )(qkv, qkv, qkv, bg, proj, norm_g.reshape(1, -1))


def _gdn_prompt(x, bsz, t, w_in, conv_w, a_log, dt_bias, norm_g, w_out, ln_g, ln_b):
    m = bsz * t
    tm = min(m, 512)
    proj = _mm(x, w_in, GDN_CONV_DIM + GDN_V, min(m, 2048), 512, "gdn_proj")
    bg = _gdn_gates(x, w_in, a_log, dt_bias, tm, "gdn_gates")
    qkv, tail = _gdn_conv_prompt(proj, conv_w, bsz, t, min(t, 256), "gdn_conv")
    o, s = _gdn_chunks(qkv, bg, proj, norm_g, bsz, t, "gdn_chunks")
    x1 = _out_ln(x, o, w_out, ln_g, ln_b, tm, "gdn_out")
    return x1, tail[:, SUBLANES - (GDN_CONV_W - 1):], s


def _s5_discretize(a_re, a_im, log_dt, b_re, b_im):
    g, p = a_re.shape
    bt_re = jnp.swapaxes(b_re, 1, 2)
    bt_im = jnp.swapaxes(b_im, 1, 2)

    def body(ar_ref, ai_ref, ldt_ref, br_ref, bi_ref, lr_ref, li_ref, bbr_ref, bbi_ref):
        ar, ai = ar_ref[...], ai_ref[...]
        dt = jnp.exp(ldt_ref[...])
        mag = jnp.exp(ar * dt)
        lr, li = mag * jnp.cos(ai * dt), mag * jnp.sin(ai * dt)
        den = ar * ar + ai * ai
        f_re = ((lr - 1.0) * ar + li * ai) / den
        f_im = (li * ar - (lr - 1.0) * ai) / den
        lr_ref[...] = lr
        li_ref[...] = li
        br, bi = br_ref[...], bi_ref[...]
        fr, fi = f_re[:, None, :], f_im[:, None, :]
        bbr_ref[...] = fr * br - fi * bi
        bbi_ref[...] = fr * bi + fi * br

    args = (a_re, a_im, log_dt.reshape(g, 1), bt_re, bt_im)
    return pl.pallas_call(
        body, grid=(1,), in_specs=[_full_spec(a) for a in args],
        out_specs=[pl.BlockSpec((g, p), lambda i: (0, 0))] * 2 + [pl.BlockSpec(bt_re.shape, lambda i: (0, 0, 0))] * 2,
        out_shape=[jax.ShapeDtypeStruct((g, p), F32)] * 2 + [jax.ShapeDtypeStruct(bt_re.shape, F32)] * 2,
        name="s5_discretize")(*args)


def _blockdiag(a, per):
    g, r, c = a.shape
    a4 = a.reshape(g // per, per, r, c)
    eye = jnp.eye(per, dtype=a.dtype)
    return jnp.einsum("jgrc,gh->jgrhc", a4, eye).reshape(g // per, per * r, per * c)


S5_GPT = LANES // S5_GROUP
S5_NT = S5_GROUPS // S5_GPT
S5_HT = S5_GPT * S5_STATE


def _s5_weights(a_re, a_im, log_dt, b_re, b_im, c_re, c_im):
    lr, li, bbr, bbi = _s5_discretize(a_re, a_im, log_dt, b_re, b_im)
    bcat = jnp.concatenate([_blockdiag(bbr, S5_GPT), _blockdiag(bbi, S5_GPT)], axis=2)
    ccat = jnp.concatenate([_blockdiag(jnp.swapaxes(c_re, 1, 2), S5_GPT),
                            -_blockdiag(jnp.swapaxes(c_im, 1, 2), S5_GPT)], axis=1)
    return lr.reshape(1, S5_H), li.reshape(1, S5_H), bcat, ccat


def _gelu(y):
    return 0.5 * y * (1.0 + jnp.tanh(math.sqrt(2.0 / math.pi) * (y + 0.044715 * (y * y * y))))


def _s5_scan(x, bsz, t, s_re0, s_im0, lr, li, bcat, ccat, d_skip, tc, name):
    d = x.shape[1]
    nt = t // tc
    rows = bsz * tc
    lq = 1024

    def body(x_ref, sr0_ref, si0_ref, lr_ref, li_ref, b_ref, c_ref, d_ref, z_ref, sr_ref, si_ref, hre, him):
        i = pl.program_id(0)

        @pl.when(i == 0)
        def _():
            sr_ref[...] = sr0_ref[...]
            si_ref[...] = si0_ref[...]

        def x_tile(j):
            return jnp.concatenate([x_ref[:, s * d + j * LANES:s * d + (j + 1) * LANES] for s in range(tc)], axis=0)

        for j in range(S5_NT):
            bu = _dot(x_tile(j), b_ref[j])
            hre[:, j * S5_HT:(j + 1) * S5_HT] = bu[:, :S5_HT]
            him[:, j * S5_HT:(j + 1) * S5_HT] = bu[:, S5_HT:]

        for q in range(S5_H // lq):
            ls = slice(q * lq, (q + 1) * lq)
            lam_r = jnp.broadcast_to(lr_ref[:, ls], (bsz, lq))
            lam_i = jnp.broadcast_to(li_ref[:, ls], (bsz, lq))

            def step(tt, carry):
                sr, si = carry
                idx = pl.ds(pl.multiple_of(tt * bsz, bsz), bsz)
                nr = lam_r * sr - lam_i * si + hre[idx, ls]
                ni = lam_r * si + lam_i * sr + him[idx, ls]
                hre[idx, ls] = nr
                him[idx, ls] = ni
                return nr, ni

            sr, si = lax.fori_loop(0, tc, step, (sr_ref[:, ls], si_ref[:, ls]), unroll=8)
            sr_ref[:, ls] = sr
            si_ref[:, ls] = si

        dsk = d_ref[...]
        for j in range(S5_NT):
            hs = slice(j * S5_HT, (j + 1) * S5_HT)
            cj = c_ref[j]
            y = _dot(hre[:, hs], cj[:S5_HT]) + _dot(him[:, hs], cj[S5_HT:])
            z_ref[:, j * LANES:(j + 1) * LANES] = _gelu(y + dsk[:, j * LANES:(j + 1) * LANES] * x_tile(j))

    fulls = (s_re0, s_im0, lr, li, bcat, ccat, d_skip.reshape(1, d))
    return pl.pallas_call(
        body, grid=(nt,),
        in_specs=[pl.BlockSpec((bsz, tc * d), lambda i: (0, i))] + [_full_spec(a) for a in fulls],
        out_specs=[pl.BlockSpec((rows, d), lambda i: (i, 0)),
                   pl.BlockSpec((bsz, S5_H), lambda i: (0, 0)), pl.BlockSpec((bsz, S5_H), lambda i: (0, 0))],
        out_shape=[jax.ShapeDtypeStruct((t * bsz, d), F32),
                   jax.ShapeDtypeStruct((bsz, S5_H), F32), jax.ShapeDtypeStruct((bsz, S5_H), F32)],
        scratch_shapes=[pltpu.VMEM((rows, S5_H), F32), pltpu.VMEM((rows, S5_H), F32)],
        compiler_params=_cp(("arbitrary",)), name=name)(x.reshape(bsz, t * d), *fulls)


def _s5_out_tm(x, z_tm, bsz, t, w_o, w_gate, g, b, tt, name):
    d = x.shape[1]
    nt = t // tt

    def body(x_ref, z_ref, wo_ref, wg_ref, g_ref, b_ref, y_ref):
        z = z_ref[...]
        h = _dot(z, wo_ref[...]) * _sigmoid(_dot(z, wg_ref[...]))
        y_ref[...] = _ln(ALPHA * x_ref[...] + h, g_ref[...], b_ref[...])

    fulls = (w_o, w_gate, g.reshape(1, -1), b.reshape(1, -1))
    return pl.pallas_call(
        body, grid=(bsz, nt),
        in_specs=[pl.BlockSpec((tt, d), lambda bb, i: (bb * nt + i, 0)),
                  pl.BlockSpec((tt, d), lambda bb, i: (i, bb))] + [_full_spec(a) for a in fulls],
        out_specs=pl.BlockSpec((tt, d), lambda bb, i: (bb * nt + i, 0)),
        out_shape=jax.ShapeDtypeStruct(x.shape, F32),
        compiler_params=_cp(("parallel", "parallel")), name=name)(x, z_tm.reshape(t, bsz * d), *fulls)


def _s5_out(x, z, w_o, w_gate, g, b, tm, name):
    def body(x_ref, z_ref, wo_ref, wg_ref, g_ref, b_ref, y_ref):
        z = z_ref[...]
        h = _dot(z, wo_ref[...]) * _sigmoid(_dot(z, wg_ref[...]))
        y_ref[...] = _ln(ALPHA * x_ref[...] + h, g_ref[...], b_ref[...])

    return _rowwise(body, [x, z], [w_o, w_gate, g.reshape(1, -1), b.reshape(1, -1)], [D_MODEL], tm, name)[0]


def _s5_prompt(x, bsz, t, a_re, a_im, log_dt, b_re, b_im, c_re, c_im, d_skip, w_o, w_gate, ln_g, ln_b):
    lr, li, bcat, ccat = _s5_weights(a_re, a_im, log_dt, b_re, b_im, c_re, c_im)
    zero = jnp.zeros((bsz, S5_H), F32)
    z_tm, s_re, s_im = _s5_scan(x, bsz, t, zero, zero, lr, li, bcat, ccat, d_skip, min(t, 64), "s5_scan")
    x1 = _s5_out_tm(x, z_tm, bsz, t, w_o, w_gate, ln_g, ln_b, min(t, 512), "s5_out")
    return x1, s_re.reshape(bsz, S5_GROUPS, S5_STATE), s_im.reshape(bsz, S5_GROUPS, S5_STATE)


GLA_PROJ = 2 * GLA_QK + 2 * GLA_V


def _gla_logd(x, w_in, w_gk2, b_gk, tm, name):
    w1 = jnp.pad(w_in[:, GLA_PROJ:], ((0, 0), (0, LANES - GLA_RANK)))
    w2 = jnp.pad(w_gk2, ((0, LANES - GLA_RANK), (0, 0)))

    def body(x_ref, w1_ref, w2_ref, b_ref, o_ref):
        y = _dot(_dot(x_ref[...], w1_ref[...]), w2_ref[...]) + b_ref[...]
        o_ref[...] = -_softplus(-y) * (1.0 / GLA_GATE_NORM)

    return _rowwise(body, [x], [w1, w2, b_gk.reshape(1, -1)], [GLA_QK], tm, name)[0]


def _gla_chunks(proj, logd, norm_g, bsz, t, name):
    c = GLA_CHUNK
    rows = 64
    nr = t // rows
    scale = GLA_DK ** -0.5

    def body(q_ref, k_ref, v_ref, gate_ref, ld_ref, ng_ref, o_ref, s_out_ref, st_ref):
        i = pl.program_id(1)

        @pl.when(i == 0)
        def _():
            st_ref[...] = jnp.zeros_like(st_ref)

        lblk = _tri(rows, block=c).astype(F32)
        bc_all = _dot3(lblk, ld_ref[...])
        incl = _tri(c)
        ng = ng_ref[...]
        subs = range(rows // c)
        heads = range(GLA_HEADS)
        rsl = lambda s: slice(s * c, (s + 1) * c)
        ksl = lambda h: slice(h * GLA_DK, (h + 1) * GLA_DK)
        vsl = lambda h: slice(h * GLA_DV, (h + 1) * GLA_DV)
        q_ins, k_outs, g_ends, a_s = {}, {}, {}, {}
        for s in subs:
            for h in heads:
                bc = bc_all[rsl(s), ksl(h)]
                k = k_ref[rsl(s), ksl(h)]
                bcl = bc[c - 1:c]
                q_ins[s, h] = q_ref[rsl(s), ksl(h)] * scale * jnp.exp(bc)
                k_outs[s, h] = k * jnp.exp(bcl - bc)
                g_ends[s, h] = jnp.exp(bcl)
                a_s[s, h] = jnp.where(incl, _dot_nt(q_ins[s, h], k * jnp.exp(-bc)), 0.0)
        o_intra = {(s, h): _dot(a_s[s, h], v_ref[rsl(s), vsl(h)]) for s in subs for h in heads}
        for s in subs:
            sts = [st_ref[h] for h in heads]
            o_inter = [_dot_nt(q_ins[s, h], sts[h]) for h in heads]
            kvs = [_dot_tn(v_ref[rsl(s), vsl(h)], k_outs[s, h]) for h in heads]
            for h in heads:
                st_ref[h] = sts[h] * g_ends[s, h] + kvs[h]
                o = o_inter[h] + o_intra[s, h]
                o = o * lax.rsqrt(jnp.mean(o * o, -1, keepdims=True) + 1e-6) * ng
                o_ref[rsl(s), vsl(h)] = o * _silu(gate_ref[rsl(s), vsl(h)])

        @pl.when(i == nr - 1)
        def _():
            for h in range(GLA_HEADS):
                s_out_ref[0, h] = st_ref[h].T

    return pl.pallas_call(
        body, grid=(bsz, nr),
        in_specs=[pl.BlockSpec((rows, GLA_QK), lambda b, i: (b * nr + i, 0)),
                  pl.BlockSpec((rows, GLA_QK), lambda b, i: (b * nr + i, 1)),
                  pl.BlockSpec((rows, GLA_V), lambda b, i: (b * nr + i, 1)),
                  pl.BlockSpec((rows, GLA_V), lambda b, i: (b * nr + i, 2)),
                  pl.BlockSpec((rows, GLA_QK), lambda b, i: (b * nr + i, 0)),
                  pl.BlockSpec((1, GLA_DV), lambda b, i: (0, 0))],
        out_specs=[pl.BlockSpec((rows, GLA_V), lambda b, i: (b * nr + i, 0)),
                   pl.BlockSpec((1, GLA_HEADS, GLA_DK, GLA_DV), lambda b, i: (b, 0, 0, 0))],
        out_shape=[jax.ShapeDtypeStruct((bsz * t, GLA_V), F32),
                   jax.ShapeDtypeStruct((bsz, GLA_HEADS, GLA_DK, GLA_DV), F32)],
        scratch_shapes=[pltpu.VMEM((GLA_HEADS, GLA_DV, GLA_DK), F32)],
        compiler_params=_cp(("parallel", "arbitrary")), name=name,
    )(proj, proj, proj, proj, logd, norm_g.reshape(1, -1))


def _gla_prompt(x, bsz, t, w_in, w_gk2, b_gk, norm_g, w_out, ln_g, ln_b):
    m = bsz * t
    tm = min(m, 512)
    proj = _mm(x, w_in, GLA_PROJ, min(m, 2048), 512, "gla_proj")
    logd = _gla_logd(x, w_in, w_gk2, b_gk, tm, "gla_logd")
    o, s = _gla_chunks(proj, logd, norm_g, bsz, t, "gla_chunks")
    x1 = _out_ln(x, o, w_out, ln_g, ln_b, tm, "gla_out")
    return x1, s


def _rwkv_rkv(x, xp, mu, w_rkv, tm, name):
    def body(x_ref, xp_ref, mu_ref, w_ref, r_ref, k_ref, v_ref):
        x = x_ref[...]
        dx = xp_ref[...] - x
        for s, o_ref in enumerate((r_ref, k_ref, v_ref)):
            o_ref[...] = _dot(x + dx * mu_ref[s:s + 1], w_ref[s])

    return _rowwise(body, [x, xp], [mu, w_rkv], [D_MODEL] * 3, tm, name)


def _rwkv_lora(x, xp, mu, w0, w_w1, w_w2, a0, w_a1, w_a2, w_g1, w_g2, tm, name):
    def body(x_ref, xp_ref, mu_ref, w0_ref, ww1, ww2, a0_ref, wa1, wa2, wg1, wg2, lw_ref, a_ref, g_ref):
        x = x_ref[...]
        dx = xp_ref[...] - x
        xs = lambda s: x + dx * mu_ref[s:s + 1]
        w_log = -_softplus(-(w0_ref[...] + _dot(jnp.tanh(_dot(xs(3), ww1[...])), ww2[...]))) - 0.5
        lw_ref[...] = -jnp.exp(w_log)
        a_ref[...] = _sigmoid(a0_ref[...] + _dot(_dot(xs(4), wa1[...]), wa2[...]))
        g_ref[...] = _dot(_sigmoid(_dot(xs(5), wg1[...])), wg2[...])

    fulls = [mu, w0.reshape(1, -1), w_w1, w_w2, a0.reshape(1, -1), w_a1, w_a2, w_g1, w_g2]
    return _rowwise(body, [x, xp], fulls, [D_MODEL] * 3, tm, name)


def _rwkv_head_inputs(r, k, v, a, kk_w, ka_w, sl):
    kraw = k[:, sl]
    kkn = _l2n(kraw * kk_w[:, sl], 1.0)
    ah = a[:, sl]
    kh = kraw * (1.0 + (ah - 1.0) * ka_w[:, sl])
    return r[:, sl], kh, v[:, sl], kkn, kkn * ah


def _rwkv_head_out(y, rh, kh, vh, g, rk_w, lng, lnb, sl):
    yc = y - jnp.mean(y, -1, keepdims=True)
    yn = yc * lax.rsqrt(jnp.mean(yc * yc, -1, keepdims=True) + RW_GN_EPS) * lng[:, sl] + lnb[:, sl]
    bonus = jnp.sum(rh * kh * rk_w[:, sl], -1, keepdims=True) * vh
    return (yn + bonus) * g[:, sl]


def _rwkv_chunks(r, k, v, lw, a, g, k_k, k_a, r_k, ln_g, ln_b, bsz, t, name):
    c = RW_CHUNK
    nc = t // c
    hd = RW_HD

    def body(r_ref, k_ref, v_ref, lw_ref, a_ref, g_ref, kk_ref, ka_ref, rk_ref, lng_ref, lnb_ref,
             o_ref, s_out_ref, s_ref):
        i = pl.program_id(1)

        @pl.when(i == 0)
        def _():
            s_ref[...] = jnp.zeros_like(s_ref)

        lw = lw_ref[...]
        gam = _dot3(_tri(c).astype(F32), lw)
        gprev = gam - lw
        r, k, v, a, g = r_ref[...], k_ref[...], v_ref[...], a_ref[...], g_ref[...]
        kk_w, ka_w, rk_w, lng, lnb = kk_ref[...], ka_ref[...], rk_ref[...], lng_ref[...], lnb_ref[...]
        strict = _tri(c, strict=True)
        incl = _tri(c)
        heads = range(RW_H)
        hsl = lambda n: slice(n * hd, (n + 1) * hd)
        hin, ars, bks, bkos, g_ends = [], [], [], [], []
        for h in heads:
            rh, kh, vh, kkn, bh = _rwkv_head_inputs(r, k, v, a, kk_w, ka_w, hsl(h))
            hin.append((rh, kh, vh))
            gm, gp = gam[:, hsl(h)], gprev[:, hsl(h)]
            glast = gm[c - 1:c]
            e_neg = jnp.exp(-gm)
            e_out = jnp.exp(glast - gm)
            ars.append(jnp.concatenate([kkn * jnp.exp(gp), rh * jnp.exp(gm)], axis=0))
            bks.append(jnp.concatenate([bh * e_neg, kh * e_neg], axis=0))
            bkos.append(jnp.concatenate([bh * e_out, kh * e_out], axis=0))
            g_ends.append(jnp.exp(glast))
        gmats = [_dot_nt(ars[h], bks[h]) for h in heads]
        tinvs = _neumann_inv([jnp.where(strict, gm_[:c, :c], 0.0) for gm_ in gmats], c)
        makvs = [_dot(jnp.where(strict, gmats[h][:c, c:], 0.0), hin[h][2]) for h in heads]
        rbks = [jnp.concatenate([jnp.where(incl, gm_[c:, :c], 0.0), jnp.where(incl, gm_[c:, c:], 0.0)], axis=1)
                for gm_ in gmats]
        a_ss = [_dot_nt(ars[h], s_ref[h]) for h in heads]
        uvs = [jnp.concatenate([_dot(tinvs[h], -a_ss[h][:c] - makvs[h]), hin[h][2]], axis=0) for h in heads]
        ys = [a_ss[h][c:] + _dot(rbks[h], uvs[h]) for h in heads]
        svs = [_dot_tn(uvs[h], bkos[h]) for h in heads]
        for h in heads:
            s_ref[h] = s_ref[h] * g_ends[h] + svs[h]
            rh, kh, vh = hin[h]
            o_ref[:, hsl(h)] = _rwkv_head_out(ys[h], rh, kh, vh, g, rk_w, lng, lnb, hsl(h))

        @pl.when(i == nc - 1)
        def _():
            s_out_ref[0] = s_ref[...]

    row = lambda w: w.reshape(1, D_MODEL)
    blk = pl.BlockSpec((c, D_MODEL), lambda b, i: (b * nc + i, 0))
    par = pl.BlockSpec((1, D_MODEL), lambda b, i: (0, 0))
    return pl.pallas_call(
        body, grid=(bsz, nc),
        in_specs=[blk] * 6 + [par] * 5,
        out_specs=[blk, pl.BlockSpec((1, RW_H, hd, hd), lambda b, i: (b, 0, 0, 0))],
        out_shape=[jax.ShapeDtypeStruct((bsz * t, D_MODEL), F32), jax.ShapeDtypeStruct((bsz, RW_H, hd, hd), F32)],
        scratch_shapes=[pltpu.VMEM((RW_H, hd, hd), F32)],
        compiler_params=_cp(("parallel", "arbitrary")), name=name,
    )(r, k, v, lw, a, g, row(k_k), row(k_a), row(r_k), row(ln_g), row(ln_b))


def _rwkv_prompt(x, bsz, t, mu, w_rkv, w0, w_w1, w_w2, a0, w_a1, w_a2, w_g1, w_g2, k_k, k_a, r_k, gn_g, gn_b, w_o,
                 ln_g, ln_b):
    m = bsz * t
    x3 = x.reshape(bsz, t, D_MODEL)
    xp = jnp.concatenate([jnp.zeros((bsz, 1, D_MODEL), F32), x3[:, :-1]], axis=1).reshape(m, D_MODEL)
    tm = min(m, 256)
    r, k, v = _rwkv_rkv(x, xp, mu, w_rkv, tm, "rwkv_rkv")
    lw, a, g = _rwkv_lora(x, xp, mu, w0, w_w1, w_w2, a0, w_a1, w_a2, w_g1, w_g2, tm, "rwkv_lora")
    y, s = _rwkv_chunks(r, k, v, lw, a, g, k_k, k_a, r_k, gn_g, gn_b, bsz, t, "rwkv_chunks")
    x1 = _out_ln(x, y, w_o, ln_g, ln_b, min(m, 512), "rwkv_out")
    return x1, x3[:, -1], s


def _eye(n):
    return _iota2((n, n), 0) == _iota2((n, n), 1)


def _to_col(row, eye):
    return jnp.sum(jnp.where(eye, row, 0.0), axis=1, keepdims=True)


def _to_row(col, eye):
    return jnp.sum(jnp.where(eye, col, 0.0), axis=0, keepdims=True)


def _row3(a):
    return a.reshape(a.shape[0], 1, a.shape[1])


def _rows_spec(width, col_block=0):
    return pl.BlockSpec((1, 1, width), lambda b: (b, 0, col_block))


def _gdn_conv_step(proj, buf, conv_w, name):
    def body(p_ref, b0, b1, b2, w_ref, o_ref):
        w = w_ref[...]
        y = p_ref[:, :GDN_CONV_DIM] * w[3:4] + b2[...] * w[2:3] + b1[...] * w[1:2] + b0[...] * w[0:1]
        _gdn_act(y, o_ref)

    rows = [proj, buf[:, 0], buf[:, 1], buf[:, 2]]
    return _rowwise(body, rows, [conv_w], [GDN_CONV_DIM], proj.shape[0], name)[0]


def _gdn_step(qkv, bg, proj, s0, norm_g, name):
    bsz = qkv.shape[0]
    rep = GDN_VH // GDN_KH

    def body(q_ref, k_ref, v_ref, bg_ref, z_ref, ng_ref, s_ref, o_ref, so_ref):
        eye = _eye(GDN_HD)
        bgv = bg_ref[0]
        ng = ng_ref[...]
        for kh in range(GDN_KH):
            sl = slice(kh * GDN_HD, (kh + 1) * GDN_HD)
            qrow, krow = q_ref[0, :, sl], k_ref[0, :, sl]
            qcol, kcol = _to_col(qrow, eye), _to_col(krow, eye)
            qk = jnp.sum(qrow * krow, axis=1, keepdims=True)
            for h in range(kh * rep, (kh + 1) * rep):
                vs = slice(h * GDN_HD, (h + 1) * GDN_HD)
                beta = bgv[:, h:h + 1]
                eg = jnp.exp(bgv[:, GDN_VH + h:GDN_VH + h + 1])
                s = s_ref[0, h]
                ks = jnp.sum(kcol * s, axis=0, keepdims=True)
                qs = jnp.sum(qcol * s, axis=0, keepdims=True)
                v_new = beta * (v_ref[0, :, vs] - eg * ks)
                o = eg * qs + qk * v_new
                so_ref[0, h] = s * eg + kcol * v_new
                o = o * lax.rsqrt(jnp.mean(o * o, -1, keepdims=True) + 1e-6) * ng
                o_ref[0, :, vs] = o * _silu(z_ref[0, :, vs])

    st_spec = pl.BlockSpec((1, GDN_VH, GDN_HD, GDN_HD), lambda b: (b, 0, 0, 0))
    o, s = pl.pallas_call(
        body, grid=(bsz,),
        in_specs=[_rows_spec(GDN_QK, 0), _rows_spec(GDN_QK, 1), _rows_spec(GDN_V, 1), _rows_spec(LANES),
                  _rows_spec(GDN_V, GDN_CONV_DIM // GDN_V), pl.BlockSpec((1, GDN_HD), lambda b: (0, 0)), st_spec],
        out_specs=[_rows_spec(GDN_V), st_spec],
        out_shape=[jax.ShapeDtypeStruct((bsz, 1, GDN_V), F32), jax.ShapeDtypeStruct(s0.shape, F32)],
        compiler_params=_cp(("parallel",)), name=name,
    )(_row3(qkv), _row3(qkv), _row3(qkv), _row3(bg), _row3(proj), norm_g.reshape(1, -1), s0)
    return o.reshape(bsz, GDN_V), s


def _gdn_sample(x, buf, s0, w_in, conv_w, a_log, dt_bias, norm_g, w_out, ln_g, ln_b):
    m = x.shape[0]
    proj = _mm(x, w_in, GDN_CONV_DIM + GDN_V, m, 512, "gdn_proj_s")
    bg = _gdn_gates(x, w_in, a_log, dt_bias, m, "gdn_gates_s")
    qkv = _gdn_conv_step(proj, buf, conv_w, "gdn_conv_s")
    o, s = _gdn_step(qkv, bg, proj, s0, norm_g, "gdn_step_s")
    x1 = _out_ln(x, o, w_out, ln_g, ln_b, m, "gdn_out_s")
    new_buf = jnp.concatenate([buf[:, 1:], proj[:, None, :GDN_CONV_DIM]], axis=1)
    return x1, new_buf, s


def _s5_step(x, s_re, s_im, lr, li, bcat, ccat, d_skip, name):
    def body(x_ref, sr_ref, si_ref, lr_ref, li_ref, b_ref, c_ref, d_ref, z_ref, hr_ref, hi_ref):
        x = x_ref[...]
        dsk = d_ref[...]
        for j in range(S5_NT):
            hs = slice(j * S5_HT, (j + 1) * S5_HT)
            xs = x[:, j * LANES:(j + 1) * LANES]
            bu = _dot(xs, b_ref[j])
            lam_r, lam_i = lr_ref[:, hs], li_ref[:, hs]
            sr, si = sr_ref[:, hs], si_ref[:, hs]
            h_re = lam_r * sr - lam_i * si + bu[:, :S5_HT]
            h_im = lam_r * si + lam_i * sr + bu[:, S5_HT:]
            hr_ref[:, hs] = h_re
            hi_ref[:, hs] = h_im
            cj = c_ref[j]
            y = _dot(h_re, cj[:S5_HT]) + _dot(h_im, cj[S5_HT:])
            z_ref[:, j * LANES:(j + 1) * LANES] = _gelu(y + dsk[:, j * LANES:(j + 1) * LANES] * xs)

    fulls = [lr, li, bcat, ccat, d_skip.reshape(1, -1)]
    return _rowwise(body, [x, s_re, s_im], fulls, [D_MODEL, S5_H, S5_H], x.shape[0], name)


def _s5_sample(x, s_re0, s_im0, a_re, a_im, log_dt, b_re, b_im, c_re, c_im, d_skip, w_o, w_gate, ln_g, ln_b):
    m = x.shape[0]
    lr, li, bcat, ccat = _s5_weights(a_re, a_im, log_dt, b_re, b_im, c_re, c_im)
    z, h_re, h_im = _s5_step(x, s_re0.reshape(m, S5_H), s_im0.reshape(m, S5_H), lr, li, bcat, ccat, d_skip, "s5_step_s")
    x1 = _s5_out(x, z, w_o, w_gate, ln_g, ln_b, m, "s5_out_s")
    return x1, h_re.reshape(m, S5_GROUPS, S5_STATE), h_im.reshape(m, S5_GROUPS, S5_STATE)


def _gla_step(proj, logd, s0, norm_g, name):
    bsz = proj.shape[0]
    scale = GLA_DK ** -0.5

    def body(q_ref, k_ref, v_ref, gate_ref, ld_ref, ng_ref, s_ref, o_ref, so_ref):
        eye = _eye(GLA_DK)
        ng = ng_ref[...]
        for h in range(GLA_HEADS):
            ks = slice(h * GLA_DK, (h + 1) * GLA_DK)
            vs = slice(h * GLA_DV, (h + 1) * GLA_DV)
            bc = ld_ref[0, :, ks]
            k = k_ref[0, :, ks]
            v = v_ref[0, :, vs]
            q_in = q_ref[0, :, ks] * scale * jnp.exp(bc)
            a = jnp.sum(q_in * (k * jnp.exp(-bc)), axis=1, keepdims=True)
            s = s_ref[0, h]
            o = jnp.sum(_to_col(q_in, eye) * s, axis=0, keepdims=True) + a * v
            so_ref[0, h] = s * _to_col(jnp.exp(bc), eye) + _to_col(k, eye) * v
            o = o * lax.rsqrt(jnp.mean(o * o, -1, keepdims=True) + 1e-6) * ng
            o_ref[0, :, vs] = o * _silu(gate_ref[0, :, vs])

    st_spec = pl.BlockSpec((1, GLA_HEADS, GLA_DK, GLA_DV), lambda b: (b, 0, 0, 0))
    o, s = pl.pallas_call(
        body, grid=(bsz,),
        in_specs=[_rows_spec(GLA_QK, 0), _rows_spec(GLA_QK, 1), _rows_spec(GLA_V, 1), _rows_spec(GLA_V, 2),
                  _rows_spec(GLA_QK), pl.BlockSpec((1, GLA_DV), lambda b: (0, 0)), st_spec],
        out_specs=[_rows_spec(GLA_V), st_spec],
        out_shape=[jax.ShapeDtypeStruct((bsz, 1, GLA_V), F32), jax.ShapeDtypeStruct(s0.shape, F32)],
        compiler_params=_cp(("parallel",)), name=name,
    )(_row3(proj), _row3(proj), _row3(proj), _row3(proj), _row3(logd), norm_g.reshape(1, -1), s0)
    return o.reshape(bsz, GLA_V), s


def _gla_sample(x, s0, w_in, w_gk2, b_gk, norm_g, w_out, ln_g, ln_b):
    m = x.shape[0]
    proj = _mm(x, w_in, GLA_PROJ, m, 512, "gla_proj_s")
    logd = _gla_logd(x, w_in, w_gk2, b_gk, m, "gla_logd_s")
    o, s = _gla_step(proj, logd, s0, norm_g, "gla_step_s")
    x1 = _out_ln(x, o, w_out, ln_g, ln_b, m, "gla_out_s")
    return x1, s


def _dot3_r(x, l):
    hi = x.astype(BF16).astype(F32)
    r1 = x - hi
    mid = r1.astype(BF16).astype(F32)
    lo = r1 - mid
    return _dot(hi, l) + _dot(mid, l) + _dot(lo, l)


def _head_ones():
    return jnp.kron(jnp.eye(RW_H, dtype=F32), jnp.ones((RW_HD, RW_HD), F32))


def _rwkv_step_prep(r, k, v, lw, a, k_k, k_a, name):
    bsz = r.shape[0]

    def body(r_ref, k_ref, v_ref, lw_ref, a_ref, kk_ref, ka_ref, ones_ref, rt, kt, vt, kkt, bt, dt, kh_ref):
        k, a = k_ref[...], a_ref[...]
        kk = k * kk_ref[...]
        kkn = kk * lax.rsqrt(_dot3_r(kk * kk, ones_ref[...]) + 1e-6)
        kh = k * (1.0 + (a - 1.0) * ka_ref[...])
        kh_ref[...] = kh
        rt[...] = r_ref[...].T
        kt[...] = kh.T
        vt[...] = v_ref[...].T
        kkt[...] = kkn.T
        bt[...] = (kkn * a).T
        dt[...] = jnp.exp(lw_ref[...]).T

    args = (r, k, v, lw, a, k_k.reshape(1, -1), k_a.reshape(1, -1), _head_ones())
    tshape = jax.ShapeDtypeStruct((D_MODEL, bsz), F32)
    return pl.pallas_call(
        body, grid=(1,), in_specs=[_full_spec(t) for t in args],
        out_specs=[pl.BlockSpec((D_MODEL, bsz), lambda i: (0, 0))] * 6 + [pl.BlockSpec((bsz, D_MODEL), lambda i: (0, 0))],
        out_shape=[tshape] * 6 + [jax.ShapeDtypeStruct((bsz, D_MODEL), F32)],
        compiler_params=_cp(("arbitrary",)), name=name)(*args)


def _rwkv_step_lanes(rt, kt, vt, kkt, bt, dt, s_t, name):
    hd = RW_HD
    bsz = s_t.shape[-1]

    def body(rt_ref, kt_ref, vt_ref, kkt_ref, bt_ref, dt_ref, s_ref, y_ref, so_ref):
        hs = pl.ds(pl.multiple_of(pl.program_id(0) * hd, hd), hd)
        r_h, k_h, v_h, kk_h, b_h, d_h = (ref[hs, :] for ref in (rt_ref, kt_ref, vt_ref, kkt_ref, bt_ref, dt_ref))
        ys = []
        for vi in range(hd):
            s = s_ref[0, vi]
            sa = -jnp.sum(s * kk_h, axis=0, keepdims=True)
            s_new = s * d_h + sa * b_h + v_h[vi:vi + 1] * k_h
            so_ref[0, vi] = s_new
            ys.append(jnp.sum(s_new * r_h, axis=0, keepdims=True))
        y_ref[...] = jnp.concatenate(ys, axis=0)

    vec = pl.BlockSpec((D_MODEL, bsz), lambda h: (0, 0))
    st = pl.BlockSpec((1, hd, hd, bsz), lambda h: (h, 0, 0, 0))
    return pl.pallas_call(
        body, grid=(RW_H,), in_specs=[vec] * 6 + [st],
        out_specs=[pl.BlockSpec((hd, bsz), lambda h: (h, 0)), st],
        out_shape=[jax.ShapeDtypeStruct((D_MODEL, bsz), F32), jax.ShapeDtypeStruct(s_t.shape, F32)],
        compiler_params=_cp(("parallel",)), name=name)(rt, kt, vt, kkt, bt, dt, s_t)


def _rwkv_step_out(x, yt, r, kh, v, g, r_k, gn_g, gn_b, w_o, ln_g, ln_b, name):
    def body(x_ref, yt_ref, r_ref, k_ref, v_ref, g_ref, rk_ref, gg_ref, gb_ref, ones_ref, wo_ref, lg_ref, lb_ref,
             o_ref):
        ones = ones_ref[...]
        y = yt_ref[...].T
        yc = y - _dot3_r(y, ones) * (1.0 / RW_HD)
        yn = yc * lax.rsqrt(_dot3_r(yc * yc, ones) * (1.0 / RW_HD) + RW_GN_EPS) * gg_ref[...] + gb_ref[...]
        bonus = _dot3_r(r_ref[...] * k_ref[...] * rk_ref[...], ones) * v_ref[...]
        o = (yn + bonus) * g_ref[...]
        o_ref[...] = _ln(ALPHA * x_ref[...] + _dot(o, wo_ref[...]), lg_ref[...], lb_ref[...])

    row = lambda w: w.reshape(1, D_MODEL)
    args = (x, yt, r, kh, v, g, row(r_k), row(gn_g), row(gn_b), _head_ones(), w_o, row(ln_g), row(ln_b))
    return pl.pallas_call(
        body, grid=(1,), in_specs=[_full_spec(t) for t in args],
        out_specs=pl.BlockSpec(x.shape, lambda i: (0, 0)),
        out_shape=jax.ShapeDtypeStruct(x.shape, F32),
        compiler_params=_cp(("arbitrary",)), name=name)(*args)


def _rwkv_sample(x, shift0, s0, mu, w_rkv, w0, w_w1, w_w2, a0, w_a1, w_a2, w_g1, w_g2, k_k, k_a, r_k, gn_g, gn_b, w_o,
                 ln_g, ln_b):
    m = x.shape[0]
    r, k, v = _rwkv_rkv(x, shift0, mu, w_rkv, m, "rwkv_rkv_s")
    lw, a, g = _rwkv_lora(x, shift0, mu, w0, w_w1, w_w2, a0, w_a1, w_a2, w_g1, w_g2, m, "rwkv_lora_s")
    rt, kt, vt, kkt, bt, dt, kh = _rwkv_step_prep(r, k, v, lw, a, k_k, k_a, "rwkv_prep_s")
    yt, s_t = _rwkv_step_lanes(rt, kt, vt, kkt, bt, dt, jnp.transpose(s0, (1, 2, 3, 0)), "rwkv_step_s")
    x1 = _rwkv_step_out(x, yt, r, kh, v, g, r_k, gn_g, gn_b, w_o, ln_g, ln_b, "rwkv_out_s")
    return x1, x, jnp.transpose(s_t, (3, 0, 1, 2))


def kernel(x_prompt, x_sample, state_gdn_conv, state_gdn, state_s5_re, state_s5_im, state_gla, state_rwkv_shift,
           state_rwkv, p_prompt, p_sample, gdn_w_in, gdn_conv_w, gdn_a_log, gdn_dt_bias, gdn_norm_g, gdn_w_out,
           s5_a_re, s5_a_im, s5_log_dt, s5_b_re, s5_b_im, s5_c_re, s5_c_im, s5_d, s5_w_o, s5_w_gate,
           gla_w_in, gla_w_gk2, gla_b_gk, gla_norm_g, gla_w_out,
           rwkv_mu, rwkv_w_rkv, rwkv_w0, rwkv_w_w1, rwkv_w_w2, rwkv_a0, rwkv_w_a1, rwkv_w_a2, rwkv_w_g1, rwkv_w_g2,
           rwkv_k_k, rwkv_k_a, rwkv_r_k, rwkv_ln_g, rwkv_ln_b, rwkv_w_o,
           ln_mix_g, ln_mix_b, ln_ffn_g, ln_ffn_b, mlp_w1, mlp_w2, ple_w, ple_gate_w):
    bsz, t, d = x_prompt.shape
    bs = x_sample.shape[0]
    gdn_w = (gdn_w_in[0], gdn_conv_w[0], gdn_a_log[0], gdn_dt_bias[0], gdn_norm_g[0], gdn_w_out[0])
    s5_w = (s5_a_re[0], s5_a_im[0], s5_log_dt[0], s5_b_re[0], s5_b_im[0], s5_c_re[0], s5_c_im[0], s5_d[0],
            s5_w_o[0], s5_w_gate[0])
    gla_w = (gla_w_in[0], gla_w_gk2[0], gla_b_gk[0], gla_norm_g[0], gla_w_out[0])
    rwkv_w = (rwkv_mu[0], rwkv_w_rkv[0], rwkv_w0[0], rwkv_w_w1[0], rwkv_w_w2[0], rwkv_a0[0], rwkv_w_a1[0],
              rwkv_w_a2[0], rwkv_w_g1[0], rwkv_w_g2[0], rwkv_k_k[0], rwkv_k_a[0], rwkv_r_k[0], rwkv_ln_g[0],
              rwkv_ln_b[0], rwkv_w_o[0])

    def ffn(x, p, i, tm, tag):
        return _mlp_ple(x, p, mlp_w1[i], mlp_w2[i], ln_ffn_g[i], ln_ffn_b[i], ple_w[i], ple_gate_w[i], tm, 1024,
                        f"mlp{i}_{tag}")

    xp = x_prompt.reshape(bsz * t, d)
    pp = p_prompt.reshape(DEPTH, bsz * t, D_PLE)
    tm_p = min(bsz * t, 512)
    xp, gc_p, gs_p = _gdn_prompt(xp, bsz, t, *gdn_w, ln_mix_g[0], ln_mix_b[0])
    xp = ffn(xp, pp[0], 0, tm_p, "p")
    xp, sr_p, si_p = _s5_prompt(xp, bsz, t, *s5_w, ln_mix_g[1], ln_mix_b[1])
    xp = ffn(xp, pp[1], 1, tm_p, "p")
    xp, la_p = _gla_prompt(xp, bsz, t, *gla_w, ln_mix_g[2], ln_mix_b[2])
    xp = ffn(xp, pp[2], 2, tm_p, "p")
    xp, sh_p, rs_p = _rwkv_prompt(xp, bsz, t, *rwkv_w, ln_mix_g[3], ln_mix_b[3])
    xp = ffn(xp, pp[3], 3, tm_p, "p")

    xs = x_sample.reshape(bs, d)
    ps = p_sample.reshape(DEPTH, bs, D_PLE)
    xs, gc_s, gs_s = _gdn_sample(xs, state_gdn_conv[0], state_gdn[0], *gdn_w, ln_mix_g[0], ln_mix_b[0])
    xs = ffn(xs, ps[0], 0, bs, "s")
    xs, sr_s, si_s = _s5_sample(xs, state_s5_re[0], state_s5_im[0], *s5_w, ln_mix_g[1], ln_mix_b[1])
    xs = ffn(xs, ps[1], 1, bs, "s")
    xs, la_s = _gla_sample(xs, state_gla[0], *gla_w, ln_mix_g[2], ln_mix_b[2])
    xs = ffn(xs, ps[2], 2, bs, "s")
    xs, sh_s, rs_s = _rwkv_sample(xs, state_rwkv_shift[0], state_rwkv[0], *rwkv_w, ln_mix_g[3], ln_mix_b[3])
    xs = ffn(xs, ps[3], 3, bs, "s")

    e = lambda a: a[None]
    return (xp.reshape(bsz, t, d), xs.reshape(bs, 1, d), e(gc_p), e(gc_s), e(gs_p), e(gs_s), e(sr_p), e(sr_s),
            e(si_p), e(si_s), e(la_p), e(la_s), e(sh_p), e(sh_s), e(rs_p), e(rs_s))
```

```python
import functools
import math

import jax
import jax.numpy as jnp
from jax import lax
from jax.experimental import pallas as pl
from jax.experimental.pallas import tpu as pltpu

F32 = jnp.float32
BF16 = jnp.bfloat16

D_MODEL = 1024
DEPTH = 4
D_PLE = 256
D_FF = 4 * D_MODEL
LN_EPS = 1e-5
ALPHA = (2.0 * DEPTH) ** 0.25

GDN_HD = 128
GDN_KH = 8
GDN_VH = 16
GDN_QK = GDN_KH * GDN_HD
GDN_V = GDN_VH * GDN_HD
GDN_CONV_DIM = 2 * GDN_QK + GDN_V
GDN_CONV_W = 4
GDN_CHUNK = 64

S5_GROUP = 16
S5_GROUPS = D_MODEL // S5_GROUP
S5_STATE = 64
S5_H = S5_GROUPS * S5_STATE

GLA_HEADS = 4
GLA_DK = 128
GLA_DV = 256
GLA_QK = GLA_HEADS * GLA_DK
GLA_V = GLA_HEADS * GLA_DV
GLA_RANK = 16
GLA_GATE_NORM = 16.0
GLA_CHUNK = 16

RW_HD = 64
RW_H = D_MODEL // RW_HD
RW_GN_EPS = 64e-5
RW_CHUNK = 64

LANES = 128
SUBLANES = 8
VMEM_LIMIT = 56 * 1024 * 1024


def _cp(sem, vmem=VMEM_LIMIT):
    return pltpu.CompilerParams(dimension_semantics=sem, vmem_limit_bytes=vmem)


def _dot(a, b):
    return jnp.dot(a, b, preferred_element_type=F32)


def _dot_nt(a, b):
    return lax.dot_general(a, b, (((1,), (1,)), ((), ())), preferred_element_type=F32)


def _dot_tn(a, b):
    return lax.dot_general(a, b, (((0,), (0,)), ((), ())), preferred_element_type=F32)


def _dot3(l, x):
    hi = x.astype(BF16).astype(F32)
    r1 = x - hi
    mid = r1.astype(BF16).astype(F32)
    lo = r1 - mid
    return _dot(l, hi) + _dot(l, mid) + _dot(l, lo)


def _softplus(x):
    return jnp.maximum(x, 0.0) + jnp.log(1.0 + jnp.exp(-jnp.abs(x)))


def _sigmoid(x):
    return 1.0 / (1.0 + jnp.exp(-x))


def _silu(x):
    return x * _sigmoid(x)


def _ln(x, g, b):
    xc = x - jnp.mean(x, -1, keepdims=True)
    var = jnp.mean(xc * xc, -1, keepdims=True)
    return xc * lax.rsqrt(var + LN_EPS) * g + b


def _iota2(shape, axis):
    return lax.broadcasted_iota(jnp.int32, shape, axis)


def _tri(n, strict=False, block=None):
    r = _iota2((n, n), 0)
    c = _iota2((n, n), 1)
    m = (r > c) if strict else (r >= c)
    if block is not None:
        m = m & ((r // block) == (c // block))
    return m


def _full_spec(a):
    n = a.ndim
    return pl.BlockSpec(a.shape, lambda *_: (0,) * n)


def _mm(x, w, n_cols, tm, tn, name):
    m, k = x.shape

    def body(x_ref, w_ref, o_ref):
        o_ref[...] = _dot(x_ref[...], w_ref[...])

    return pl.pallas_call(
        body, grid=(m // tm, n_cols // tn),
        in_specs=[pl.BlockSpec((tm, k), lambda i, j: (i, 0)), pl.BlockSpec((k, tn), lambda i, j: (0, j))],
        out_specs=pl.BlockSpec((tm, tn), lambda i, j: (i, j)),
        out_shape=jax.ShapeDtypeStruct((m, n_cols), F32),
        compiler_params=_cp(("parallel", "arbitrary")), name=name)(x, w)


def _rowwise(body, rows, fulls, out_cols, tm, name):
    m = rows[0].shape[0]
    in_specs = [pl.BlockSpec((tm, r.shape[1]), lambda i: (i, 0)) for r in rows] + [_full_spec(f) for f in fulls]
    out_specs = [pl.BlockSpec((tm, c), lambda i: (i, 0)) for c in out_cols]
    out_shape = [jax.ShapeDtypeStruct((m, c), F32) for c in out_cols]
    res = pl.pallas_call(body, grid=(m // tm,), in_specs=in_specs, out_specs=out_specs, out_shape=out_shape,
                         compiler_params=_cp(("parallel",)), name=name)(*rows, *fulls)
    return res


def _out_ln(x, o, w_out, g, b, tm, name):
    def body(x_ref, o_ref, w_ref, g_ref, b_ref, y_ref):
        y_ref[...] = _ln(ALPHA * x_ref[...] + _dot(o_ref[...], w_ref[...]), g_ref[...], b_ref[...])

    return _rowwise(body, [x, o], [w_out, g.reshape(1, -1), b.reshape(1, -1)], [D_MODEL], tm, name)[0]


def _resident_spec(a):
    n = a.ndim
    return pl.BlockSpec(a.shape, lambda *_: (0,) * n, pipeline_mode=pl.Buffered(1))


def _mlp_ple(x1, p, w1, w2, g, b, ple_w, gate_w, tm, tf, name):
    m = x1.shape[0]
    nf = D_FF // tf

    def body(x_ref, p_ref, w1_ref, w2_ref, g_ref, b_ref, pw_ref, gw_ref, o_ref, h_ref):
        x = x_ref[...]
        xb = x.astype(BF16)
        for f in range(nf):
            h = jnp.maximum(_dot(xb, w1_ref[:, f * tf:(f + 1) * tf]), 0.0)
            h_ref[:, f * tf:(f + 1) * tf] = (h * h).astype(BF16)
        x2 = _ln(ALPHA * x + _dot(h_ref[...], w2_ref[...]), g_ref[...], b_ref[...])
        gate = _sigmoid(_dot(x2.astype(BF16), gw_ref[...]))
        o_ref[...] = x2 + _dot(p_ref[...].astype(BF16), pw_ref[...]) * gate

    fulls = (w1.astype(BF16), w2.astype(BF16), g.reshape(1, -1), b.reshape(1, -1), ple_w.astype(BF16),
             gate_w.astype(BF16))
    return pl.pallas_call(
        body, grid=(m // tm,),
        in_specs=[pl.BlockSpec((tm, D_MODEL), lambda i: (i, 0)), pl.BlockSpec((tm, D_PLE), lambda i: (i, 0))]
        + [_resident_spec(a) for a in fulls],
        out_specs=pl.BlockSpec((tm, D_MODEL), lambda i: (i, 0)),
        out_shape=jax.ShapeDtypeStruct((m, D_MODEL), F32),
        scratch_shapes=[pltpu.VMEM((tm, D_FF), BF16)],
        compiler_params=_cp(("parallel",)), name=name,
    )(x1, p, *fulls)


def _gdn_gates(x, w_in, a_log, dt_bias, tm, name):
    w_ba = jnp.pad(w_in[:, GDN_CONV_DIM + GDN_V:], ((0, 0), (0, LANES - 2 * GDN_VH)))
    pad = lambda v: jnp.pad(v.reshape(1, -1), ((0, 0), (GDN_VH, LANES - 2 * GDN_VH)))

    def body(x_ref, w_ref, al_ref, dt_ref, o_ref):
        y = _dot(x_ref[...], w_ref[...])
        lane = _iota2(y.shape, 1)
        g = -jnp.exp(al_ref[...]) * _softplus(y + dt_ref[...])
        o_ref[...] = jnp.where(lane < GDN_VH, _sigmoid(y), g)

    return _rowwise(body, [x], [w_ba, pad(a_log), pad(dt_bias)], [LANES], tm, name)[0]


def _l2n(x, scale):
    return x * (lax.rsqrt(jnp.sum(x * x, -1, keepdims=True) + 1e-6) * scale)


def _gdn_act(y, o_ref):
    y = _silu(y)
    for h in range(GDN_KH):
        sl = slice(h * GDN_HD, (h + 1) * GDN_HD)
        o_ref[:, sl] = _l2n(y[:, sl], GDN_HD ** -0.5)
        sl = slice(GDN_QK + h * GDN_HD, GDN_QK + (h + 1) * GDN_HD)
        o_ref[:, sl] = _l2n(y[:, sl], 1.0)
    o_ref[:, 2 * GDN_QK:] = y[:, 2 * GDN_QK:]


def _gdn_conv_prompt(proj, conv_w, bsz, t, tt, name):
    nt = t // tt
    c = GDN_CONV_DIM

    def body(x_ref, w_ref, o_ref, tail_ref, buf_ref):
        i = pl.program_id(1)

        @pl.when(i == 0)
        def _():
            buf_ref[pl.ds(0, SUBLANES)] = jnp.zeros((SUBLANES, c), F32)

        x = x_ref[...]
        w = w_ref[...]
        buf_ref[pl.ds(SUBLANES, tt)] = x
        y = x * w[3:4]
        for j in range(1, GDN_CONV_W):
            y = y + buf_ref[pl.ds(SUBLANES - j, tt)] * w[3 - j:4 - j]
        _gdn_act(y, o_ref)
        buf_ref[pl.ds(0, SUBLANES)] = x[tt - SUBLANES:]
        tail_ref[0] = x[tt - SUBLANES:]

    return pl.pallas_call(
        body, grid=(bsz, nt),
        in_specs=[pl.BlockSpec((tt, c), lambda b, i: (b * nt + i, 0)), _full_spec(conv_w)],
        out_specs=[pl.BlockSpec((tt, c), lambda b, i: (b * nt + i, 0)),
                   pl.BlockSpec((1, SUBLANES, c), lambda b, i: (b, 0, 0))],
        out_shape=[jax.ShapeDtypeStruct((bsz * t, c), F32), jax.ShapeDtypeStruct((bsz, SUBLANES, c), F32)],
        scratch_shapes=[pltpu.VMEM((SUBLANES + tt, c), F32)],
        compiler_params=_cp(("parallel", "arbitrary")), name=name)(proj, conv_w)


def _neumann_inv(ms, n):
    eye = (_iota2((n, n), 0) == _iota2((n, n), 1)).astype(F32)
    ts = [eye - m for m in ms]
    ps = [_dot(m, m) for m in ms]
    k = 2
    while True:
        ts = [t + _dot(t, p) for t, p in zip(ts, ps)]
        k *= 2
        if k >= n:
            return ts
        ps = [_dot(p, p) for p in ps]


def _gdn_chunks(qkv, bg, proj, norm_g, bsz, t, name):
    c = GDN_CHUNK
    nc = t // c
    z_blk = GDN_CONV_DIM // GDN_V

    def body(q_ref, k_ref, v_ref, bg_ref, z_ref, ng_ref, o_ref, s_out_ref, s_ref):
        i = pl.program_id(1)

        @pl.when(i == 0)
        def _():
            s_ref[...] = jnp.zeros_like(s_ref)

        bgv = bg_ref[...]
        ltri = _tri(c).astype(F32)
        gc = _dot3(ltri, bgv)
        gct = jnp.concatenate([gc, jnp.zeros_like(gc)], axis=0).T
        incl = _tri(c)
        strict = _tri(c, strict=True)
        ng = ng_ref[...]
        rep = GDN_VH // GDN_KH
        heads = range(GDN_VH)
        hsl = lambda n: slice(n * GDN_HD, (n + 1) * GDN_HD)
        gram = [_dot_nt(jnp.concatenate([k_ref[:, hsl(n)], q_ref[:, hsl(n)]], axis=0), k_ref[:, hsl(n)])
                for n in range(GDN_KH)]
        ms, aqks, rhss, q_ins, k_outs, g_ends = [], [], [], [], [], []
        for h in heads:
            kh = h // rep
            kk = k_ref[:, hsl(kh)]
            beta = bgv[:, h:h + 1]
            gcol = gc[:, GDN_VH + h:GDN_VH + h + 1]
            grow = gct[GDN_VH + h:GDN_VH + h + 1, :c]
            decay = jnp.where(incl, jnp.exp(jnp.where(incl, gcol - grow, 0.0)), 0.0)
            ms.append(jnp.where(strict, gram[kh][:c] * beta * decay, 0.0))
            aqks.append(gram[kh][c:] * decay)
            egc = jnp.exp(gcol)
            rhss.append(jnp.concatenate([v_ref[:, hsl(h)] * beta, kk * (beta * egc)], axis=1))
            q_ins.append(q_ref[:, hsl(kh)] * egc)
            glast = gc[c - 1:c, GDN_VH + h:GDN_VH + h + 1]
            k_outs.append(kk * jnp.exp(glast - gcol))
            g_ends.append(jnp.exp(glast))
        tinvs = _neumann_inv(ms, c)
        sols = [_dot(tinvs[h], rhss[h]) for h in heads]
        wqs = [_dot(jnp.concatenate([sols[h][:, GDN_HD:], q_ins[h]], axis=0), s_ref[h]) for h in heads]
        v_news = [sols[h][:, :GDN_HD] - wqs[h][:c] for h in heads]
        avs = [_dot(aqks[h], v_news[h]) for h in heads]
        kvs = [_dot_tn(k_outs[h], v_news[h]) for h in heads]
        for h in heads:
            s_ref[h] = s_ref[h] * g_ends[h] + kvs[h]
            o = wqs[h][c:] + avs[h]
            o = o * lax.rsqrt(jnp.mean(o * o, -1, keepdims=True) + 1e-6) * ng
            o_ref[:, hsl(h)] = o * _silu(z_ref[:, hsl(h)])

        @pl.when(i == nc - 1)
        def _():
            s_out_ref[0] = s_ref[...]

    return pl.pallas_call(
        body, grid=(bsz, nc),
        in_specs=[pl.BlockSpec((c, GDN_QK), lambda b, i: (b * nc + i, 0)),
                  pl.BlockSpec((c, GDN_QK), lambda b, i: (b * nc + i, 1)),
                  pl.BlockSpec((c, GDN_V), lambda b, i: (b * nc + i, 1)),
                  pl.BlockSpec((c, LANES), lambda b, i: (b * nc + i, 0)),
                  pl.BlockSpec((c, GDN_V), lambda b, i: (b * nc + i, z_blk)),
                  pl.BlockSpec((1, GDN_HD), lambda b, i: (0, 0))],
        out_specs=[pl.BlockSpec((c, GDN_V), lambda b, i: (b * nc + i, 0)),
                   pl.BlockSpec((1, GDN_VH, GDN_HD, GDN_HD), lambda b, i: (b, 0, 0, 0))],
        out_shape=[jax.ShapeDtypeStruct((bsz * t, GDN_V), F32),
                   jax.ShapeDtypeStruct((bsz, GDN_VH, GDN_HD, GDN_HD), F32)],
        scratch_shapes=[pltpu.VMEM((GDN_VH, GDN_HD, GDN_HD), F32)],
        compiler_params=_cp(("parallel", "arbitrary")), name=name,
    )(qkv, qkv, qkv, bg, proj, norm_g.reshape(1, -1))


def _gdn_prompt(x, bsz, t, w_in, conv_w, a_log, dt_bias, norm_g, w_out, ln_g, ln_b):
    m = bsz * t
    tm = min(m, 512)
    proj = _mm(x, w_in, GDN_CONV_DIM + GDN_V, min(m, 2048), 512, "gdn_proj")
    bg = _gdn_gates(x, w_in, a_log, dt_bias, tm, "gdn_gates")
    qkv, tail = _gdn_conv_prompt(proj, conv_w, bsz, t, min(t, 256), "gdn_conv")
    o, s = _gdn_chunks(qkv, bg, proj, norm_g, bsz, t, "gdn_chunks")
    x1 = _out_ln(x, o, w_out, ln_g, ln_b, tm, "gdn_out")
    return x1, tail[:, SUBLANES - (GDN_CONV_W - 1):], s


def _s5_discretize(a_re, a_im, log_dt, b_re, b_im):
    g, p = a_re.shape
    bt_re = jnp.swapaxes(b_re, 1, 2)
    bt_im = jnp.swapaxes(b_im, 1, 2)

    def body(ar_ref, ai_ref, ldt_ref, br_ref, bi_ref, lr_ref, li_ref, bbr_ref, bbi_ref):
        ar, ai = ar_ref[...], ai_ref[...]
        dt = jnp.exp(ldt_ref[...])
        mag = jnp.exp(ar * dt)
        lr, li = mag * jnp.cos(ai * dt), mag * jnp.sin(ai * dt)
        den = ar * ar + ai * ai
        f_re = ((lr - 1.0) * ar + li * ai) / den
        f_im = (li * ar - (lr - 1.0) * ai) / den
        lr_ref[...] = lr
        li_ref[...] = li
        br, bi = br_ref[...], bi_ref[...]
        fr, fi = f_re[:, None, :], f_im[:, None, :]
        bbr_ref[...] = fr * br - fi * bi
        bbi_ref[...] = fr * bi + fi * br

    args = (a_re, a_im, log_dt.reshape(g, 1), bt_re, bt_im)
    return pl.pallas_call(
        body, grid=(1,), in_specs=[_full_spec(a) for a in args],
        out_specs=[pl.BlockSpec((g, p), lambda i: (0, 0))] * 2 + [pl.BlockSpec(bt_re.shape, lambda i: (0, 0, 0))] * 2,
        out_shape=[jax.ShapeDtypeStruct((g, p), F32)] * 2 + [jax.ShapeDtypeStruct(bt_re.shape, F32)] * 2,
        name="s5_discretize")(*args)


def _blockdiag(a, per):
    g, r, c = a.shape
    a4 = a.reshape(g // per, per, r, c)
    eye = jnp.eye(per, dtype=a.dtype)
    return jnp.einsum("jgrc,gh->jgrhc", a4, eye).reshape(g // per, per * r, per * c)


S5_GPT = LANES // S5_GROUP
S5_NT = S5_GROUPS // S5_GPT
S5_HT = S5_GPT * S5_STATE


def _s5_weights(a_re, a_im, log_dt, b_re, b_im, c_re, c_im):
    lr, li, bbr, bbi = _s5_discretize(a_re, a_im, log_dt, b_re, b_im)
    bcat = jnp.concatenate([_blockdiag(bbr, S5_GPT), _blockdiag(bbi, S5_GPT)], axis=2)
    ccat = jnp.concatenate([_blockdiag(jnp.swapaxes(c_re, 1, 2), S5_GPT),
                            -_blockdiag(jnp.swapaxes(c_im, 1, 2), S5_GPT)], axis=1)
    return lr.reshape(1, S5_H), li.reshape(1, S5_H), bcat, ccat


def _gelu(y):
    return 0.5 * y * (1.0 + jnp.tanh(math.sqrt(2.0 / math.pi) * (y + 0.044715 * (y * y * y))))


def _s5_scan(x, bsz, t, s_re0, s_im0, lr, li, bcat, ccat, d_skip, tc, name):
    d = x.shape[1]
    nt = t // tc
    rows = bsz * tc
    lq = 1024

    def body(x_ref, sr0_ref, si0_ref, lr_ref, li_ref, b_ref, c_ref, d_ref, z_ref, sr_ref, si_ref, hre, him, xs, zs):
        i = pl.program_id(0)

        @pl.when(i == 0)
        def _():
            sr_ref[...] = sr0_ref[...]
            si_ref[...] = si0_ref[...]

        xs[...] = jnp.swapaxes(x_ref[...], 0, 1).reshape(rows, d)

        def x_tile(j):
            return xs[:, j * LANES:(j + 1) * LANES]

        for j in range(S5_NT):
            bu = _dot(x_tile(j), b_ref[j])
            hre[:, j * S5_HT:(j + 1) * S5_HT] = bu[:, :S5_HT]
            him[:, j * S5_HT:(j + 1) * S5_HT] = bu[:, S5_HT:]

        for q in range(S5_H // lq):
            ls = slice(q * lq, (q + 1) * lq)
            lam_r = jnp.broadcast_to(lr_ref[:, ls], (bsz, lq))
            lam_i = jnp.broadcast_to(li_ref[:, ls], (bsz, lq))

            def step(tt, carry):
                sr, si = carry
                idx = pl.ds(pl.multiple_of(tt * bsz, bsz), bsz)
                nr = lam_r * sr - lam_i * si + hre[idx, ls]
                ni = lam_r * si + lam_i * sr + him[idx, ls]
                hre[idx, ls] = nr
                him[idx, ls] = ni
                return nr, ni

            sr, si = lax.fori_loop(0, tc, step, (sr_ref[:, ls], si_ref[:, ls]), unroll=8)
            sr_ref[:, ls] = sr
            si_ref[:, ls] = si

        dsk = d_ref[...]
        for j in range(S5_NT):
            hs = slice(j * S5_HT, (j + 1) * S5_HT)
            cj = c_ref[j]
            y = _dot(hre[:, hs], cj[:S5_HT]) + _dot(him[:, hs], cj[S5_HT:])
            zs[:, j * LANES:(j + 1) * LANES] = _gelu(y + dsk[:, j * LANES:(j + 1) * LANES] * x_tile(j))
        z_ref[...] = jnp.swapaxes(zs[...].reshape(tc, bsz, d), 0, 1)

    fulls = (s_re0, s_im0, lr, li, bcat, ccat, d_skip.reshape(1, d))
    z, s_re, s_im = pl.pallas_call(
        body, grid=(nt,),
        in_specs=[pl.BlockSpec((bsz, tc, d), lambda i: (0, i, 0))] + [_full_spec(a) for a in fulls],
        out_specs=[pl.BlockSpec((bsz, tc, d), lambda i: (0, i, 0)),
                   pl.BlockSpec((bsz, S5_H), lambda i: (0, 0)), pl.BlockSpec((bsz, S5_H), lambda i: (0, 0))],
        out_shape=[jax.ShapeDtypeStruct((bsz, t, d), F32),
                   jax.ShapeDtypeStruct((bsz, S5_H), F32), jax.ShapeDtypeStruct((bsz, S5_H), F32)],
        scratch_shapes=[pltpu.VMEM((rows, S5_H), F32), pltpu.VMEM((rows, S5_H), F32),
                        pltpu.VMEM((rows, d), F32), pltpu.VMEM((rows, d), F32)],
        compiler_params=_cp(("arbitrary",)), name=name)(x.reshape(bsz, t, d), *fulls)
    return z.reshape(bsz * t, d), s_re, s_im


def _s5_out(x, z, w_o, w_gate, g, b, tm, name):
    def body(x_ref, z_ref, wo_ref, wg_ref, g_ref, b_ref, y_ref):
        z = z_ref[...]
        h = _dot(z, wo_ref[...]) * _sigmoid(_dot(z, wg_ref[...]))
        y_ref[...] = _ln(ALPHA * x_ref[...] + h, g_ref[...], b_ref[...])

    return _rowwise(body, [x, z], [w_o, w_gate, g.reshape(1, -1), b.reshape(1, -1)], [D_MODEL], tm, name)[0]


def _s5_prompt(x, bsz, t, a_re, a_im, log_dt, b_re, b_im, c_re, c_im, d_skip, w_o, w_gate, ln_g, ln_b):
    lr, li, bcat, ccat = _s5_weights(a_re, a_im, log_dt, b_re, b_im, c_re, c_im)
    zero = jnp.zeros((bsz, S5_H), F32)
    z, s_re, s_im = _s5_scan(x, bsz, t, zero, zero, lr, li, bcat, ccat, d_skip, min(t, 64), "s5_scan")
    x1 = _s5_out(x, z, w_o, w_gate, ln_g, ln_b, min(bsz * t, 512), "s5_out")
    return x1, s_re.reshape(bsz, S5_GROUPS, S5_STATE), s_im.reshape(bsz, S5_GROUPS, S5_STATE)


GLA_PROJ = 2 * GLA_QK + 2 * GLA_V


def _gla_logd(x, w_in, w_gk2, b_gk, tm, name):
    w1 = jnp.pad(w_in[:, GLA_PROJ:], ((0, 0), (0, LANES - GLA_RANK)))
    w2 = jnp.pad(w_gk2, ((0, LANES - GLA_RANK), (0, 0)))

    def body(x_ref, w1_ref, w2_ref, b_ref, o_ref):
        y = _dot(_dot(x_ref[...], w1_ref[...]), w2_ref[...]) + b_ref[...]
        o_ref[...] = -_softplus(-y) * (1.0 / GLA_GATE_NORM)

    return _rowwise(body, [x], [w1, w2, b_gk.reshape(1, -1)], [GLA_QK], tm, name)[0]


def _gla_chunks(proj, logd, norm_g, bsz, t, name):
    c = GLA_CHUNK
    rows = 64
    nr = t // rows
    scale = GLA_DK ** -0.5

    def body(q_ref, k_ref, v_ref, gate_ref, ld_ref, ng_ref, o_ref, s_out_ref, st_ref):
        i = pl.program_id(1)

        @pl.when(i == 0)
        def _():
            st_ref[...] = jnp.zeros_like(st_ref)

        lblk = _tri(rows, block=c).astype(F32)
        bc_all = _dot3(lblk, ld_ref[...])
        incl = _tri(c)
        ng = ng_ref[...]
        subs = range(rows // c)
        heads = range(GLA_HEADS)
        rsl = lambda s: slice(s * c, (s + 1) * c)
        ksl = lambda h: slice(h * GLA_DK, (h + 1) * GLA_DK)
        vsl = lambda h: slice(h * GLA_DV, (h + 1) * GLA_DV)
        q_ins, k_outs, g_ends, a_s = {}, {}, {}, {}
        for s in subs:
            for h in heads:
                bc = bc_all[rsl(s), ksl(h)]
                k = k_ref[rsl(s), ksl(h)]
                bcl = bc[c - 1:c]
                q_ins[s, h] = q_ref[rsl(s), ksl(h)] * scale * jnp.exp(bc)
                k_outs[s, h] = k * jnp.exp(bcl - bc)
                g_ends[s, h] = jnp.exp(bcl)
                a_s[s, h] = jnp.where(incl, _dot_nt(q_ins[s, h], k * jnp.exp(-bc)), 0.0)
        o_intra = {(s, h): _dot(a_s[s, h], v_ref[rsl(s), vsl(h)]) for s in subs for h in heads}
        for s in subs:
            sts = [st_ref[h] for h in heads]
            o_inter = [_dot_nt(q_ins[s, h], sts[h]) for h in heads]
            kvs = [_dot_tn(v_ref[rsl(s), vsl(h)], k_outs[s, h]) for h in heads]
            for h in heads:
                st_ref[h] = sts[h] * g_ends[s, h] + kvs[h]
                o = o_inter[h] + o_intra[s, h]
                o = o * lax.rsqrt(jnp.mean(o * o, -1, keepdims=True) + 1e-6) * ng
                o_ref[rsl(s), vsl(h)] = o * _silu(gate_ref[rsl(s), vsl(h)])

        @pl.when(i == nr - 1)
        def _():
            for h in range(GLA_HEADS):
                s_out_ref[0, h] = st_ref[h].T

    return pl.pallas_call(
        body, grid=(bsz, nr),
        in_specs=[pl.BlockSpec((rows, GLA_QK), lambda b, i: (b * nr + i, 0)),
                  pl.BlockSpec((rows, GLA_QK), lambda b, i: (b * nr + i, 1)),
                  pl.BlockSpec((rows, GLA_V), lambda b, i: (b * nr + i, 1)),
                  pl.BlockSpec((rows, GLA_V), lambda b, i: (b * nr + i, 2)),
                  pl.BlockSpec((rows, GLA_QK), lambda b, i: (b * nr + i, 0)),
                  pl.BlockSpec((1, GLA_DV), lambda b, i: (0, 0))],
        out_specs=[pl.BlockSpec((rows, GLA_V), lambda b, i: (b * nr + i, 0)),
                   pl.BlockSpec((1, GLA_HEADS, GLA_DK, GLA_DV), lambda b, i: (b, 0, 0, 0))],
        out_shape=[jax.ShapeDtypeStruct((bsz * t, GLA_V), F32),
                   jax.ShapeDtypeStruct((bsz, GLA_HEADS, GLA_DK, GLA_DV), F32)],
        scratch_shapes=[pltpu.VMEM((GLA_HEADS, GLA_DV, GLA_DK), F32)],
        compiler_params=_cp(("parallel", "arbitrary")), name=name,
    )(proj, proj, proj, proj, logd, norm_g.reshape(1, -1))


def _gla_prompt(x, bsz, t, w_in, w_gk2, b_gk, norm_g, w_out, ln_g, ln_b):
    m = bsz * t
    tm = min(m, 512)
    proj = _mm(x, w_in, GLA_PROJ, min(m, 2048), 512, "gla_proj")
    logd = _gla_logd(x, w_in, w_gk2, b_gk, tm, "gla_logd")
    o, s = _gla_chunks(proj, logd, norm_g, bsz, t, "gla_chunks")
    x1 = _out_ln(x, o, w_out, ln_g, ln_b, tm, "gla_out")
    return x1, s


def _rwkv_rkv(x, xp, mu, w_rkv, tm, name):
    def body(x_ref, xp_ref, mu_ref, w_ref, r_ref, k_ref, v_ref):
        x = x_ref[...]
        dx = xp_ref[...] - x
        for s, o_ref in enumerate((r_ref, k_ref, v_ref)):
            o_ref[...] = _dot(x + dx * mu_ref[s:s + 1], w_ref[s])

    return _rowwise(body, [x, xp], [mu, w_rkv], [D_MODEL] * 3, tm, name)


def _rwkv_lora(x, xp, mu, w0, w_w1, w_w2, a0, w_a1, w_a2, w_g1, w_g2, tm, name):
    def body(x_ref, xp_ref, mu_ref, w0_ref, ww1, ww2, a0_ref, wa1, wa2, wg1, wg2, lw_ref, a_ref, g_ref):
        x = x_ref[...]
        dx = xp_ref[...] - x
        xs = lambda s: x + dx * mu_ref[s:s + 1]
        w_log = -_softplus(-(w0_ref[...] + _dot(jnp.tanh(_dot(xs(3), ww1[...])), ww2[...]))) - 0.5
        lw_ref[...] = -jnp.exp(w_log)
        a_ref[...] = _sigmoid(a0_ref[...] + _dot(_dot(xs(4), wa1[...]), wa2[...]))
        g_ref[...] = _dot(_sigmoid(_dot(xs(5), wg1[...])), wg2[...])

    fulls = [mu, w0.reshape(1, -1), w_w1, w_w2, a0.reshape(1, -1), w_a1, w_a2, w_g1, w_g2]
    return _rowwise(body, [x, xp], fulls, [D_MODEL] * 3, tm, name)


def _rwkv_head_inputs(r, k, v, a, kk_w, ka_w, sl):
    kraw = k[:, sl]
    kkn = _l2n(kraw * kk_w[:, sl], 1.0)
    ah = a[:, sl]
    kh = kraw * (1.0 + (ah - 1.0) * ka_w[:, sl])
    return r[:, sl], kh, v[:, sl], kkn, kkn * ah


def _rwkv_head_out(y, rh, kh, vh, g, rk_w, lng, lnb, sl):
    yc = y - jnp.mean(y, -1, keepdims=True)
    yn = yc * lax.rsqrt(jnp.mean(yc * yc, -1, keepdims=True) + RW_GN_EPS) * lng[:, sl] + lnb[:, sl]
    bonus = jnp.sum(rh * kh * rk_w[:, sl], -1, keepdims=True) * vh
    return (yn + bonus) * g[:, sl]


def _rwkv_chunks(r, k, v, lw, a, g, k_k, k_a, r_k, ln_g, ln_b, bsz, t, name):
    c = RW_CHUNK
    nc = t // c
    hd = RW_HD
    nb = 1
    d = D_MODEL

    def body(r_ref, k_ref, v_ref, lw_ref, a_ref, g_ref, kk_ref, ka_ref, rk_ref, lng_ref, lnb_ref,
             o_ref, s_out_ref, s_ref):
        i = pl.program_id(1)

        @pl.when(i == 0)
        def _():
            s_ref[...] = jnp.zeros_like(s_ref)

        kk_w, ka_w, rk_w, lng, lnb = kk_ref[...], ka_ref[...], rk_ref[...], lng_ref[...], lnb_ref[...]
        tri = _tri(c).astype(F32)
        strict = _tri(c, strict=True)
        incl = _tri(c)
        hsl = lambda n: slice(n * hd, (n + 1) * hd)
        inst = [(m, h) for m in range(nb) for h in range(RW_H)]
        prep = []
        for m in range(nb):
            lw = lw_ref[m]
            gam = _dot3(tri, lw)
            glast = gam[c - 1:c]
            r, k, a = r_ref[m], k_ref[m], a_ref[m]
            kk = k * kk_w
            scale = jnp.concatenate(
                [jnp.broadcast_to(lax.rsqrt(jnp.sum(kk[:, hsl(h)] * kk[:, hsl(h)], -1, keepdims=True) + 1e-6), (c, hd))
                 for h in range(RW_H)], axis=1)
            kkn = kk * scale
            kh = k * (1.0 + (a - 1.0) * ka_w)
            bh = kkn * a
            e_neg = jnp.exp(-gam)
            e_out = jnp.exp(glast - gam)
            prep.append(dict(a1=kkn * jnp.exp(gam - lw), r1=r * jnp.exp(gam), b1=bh * e_neg, k1=kh * e_neg,
                             b1o=bh * e_out, k1o=kh * e_out, g_end=jnp.exp(glast), r=r, kh=kh, v=v_ref[m],
                             g=g_ref[m]))
        ars = [jnp.concatenate([prep[m]["a1"][:, hsl(h)], prep[m]["r1"][:, hsl(h)]], axis=0) for m, h in inst]
        bks = [jnp.concatenate([prep[m]["b1"][:, hsl(h)], prep[m]["k1"][:, hsl(h)]], axis=0) for m, h in inst]
        vhs = [prep[m]["v"][:, hsl(h)] for m, h in inst]
        n = range(len(inst))
        gmats = [_dot_nt(ars[j], bks[j]) for j in n]
        tinvs = _neumann_inv([jnp.where(strict, gm_[:c, :c], 0.0) for gm_ in gmats], c)
        makvs = [_dot(jnp.where(strict, gmats[j][:c, c:], 0.0), vhs[j]) for j in n]
        rbks = [jnp.concatenate([jnp.where(incl, gm_[c:, :c], 0.0), jnp.where(incl, gm_[c:, c:], 0.0)], axis=1)
                for gm_ in gmats]
        a_ss = [_dot_nt(ars[j], s_ref[m, h]) for j, (m, h) in enumerate(inst)]
        uvs = [jnp.concatenate([_dot(tinvs[j], -a_ss[j][:c] - makvs[j]), vhs[j]], axis=0) for j in n]
        ys = [a_ss[j][c:] + _dot(rbks[j], uvs[j]) for j in n]
        svs = [_dot_tn(uvs[j], jnp.concatenate([prep[m]["b1o"][:, hsl(h)], prep[m]["k1o"][:, hsl(h)]], axis=0))
               for j, (m, h) in enumerate(inst)]
        for j, (m, h) in enumerate(inst):
            s_ref[m, h] = s_ref[m, h] * prep[m]["g_end"][:, hsl(h)] + svs[j]
        rkr = [prep[m]["r"] * prep[m]["kh"] * rk_w for m in range(nb)]
        sum1 = [jnp.sum(ys[j], -1, keepdims=True) for j in n]
        bons = [jnp.sum(rkr[m][:, hsl(h)], -1, keepdims=True) for m, h in inst]
        ycs = [ys[j] - sum1[j] * (1.0 / hd) for j in n]
        sum2 = [jnp.sum(ycs[j] * ycs[j], -1, keepdims=True) for j in n]
        for j, (m, h) in enumerate(inst):
            yn = ycs[j] * lax.rsqrt(sum2[j] * (1.0 / hd) + RW_GN_EPS) * lng[:, hsl(h)] + lnb[:, hsl(h)]
            o_ref[m, :, hsl(h)] = (yn + bons[j] * vhs[j]) * prep[m]["g"][:, hsl(h)]

        @pl.when(i == nc - 1)
        def _():
            s_out_ref[...] = s_ref[...]

    row = lambda w: w.reshape(1, d)
    v3 = lambda z: z.reshape(bsz, t, d)
    blk = pl.BlockSpec((nb, c, d), lambda b, i: (b, i, 0))
    par = pl.BlockSpec((1, d), lambda b, i: (0, 0))
    y, s = pl.pallas_call(
        body, grid=(bsz // nb, nc),
        in_specs=[blk] * 6 + [par] * 5,
        out_specs=[blk, pl.BlockSpec((nb, RW_H, hd, hd), lambda b, i: (b, 0, 0, 0))],
        out_shape=[jax.ShapeDtypeStruct((bsz, t, d), F32), jax.ShapeDtypeStruct((bsz, RW_H, hd, hd), F32)],
        scratch_shapes=[pltpu.VMEM((nb, RW_H, hd, hd), F32)],
        compiler_params=_cp(("parallel", "arbitrary")), name=name,
    )(v3(r), v3(k), v3(v), v3(lw), v3(a), v3(g), row(k_k), row(k_a), row(r_k), row(ln_g), row(ln_b))
    return y.reshape(bsz * t, d), s


def _rwkv_prompt(x, bsz, t, mu, w_rkv, w0, w_w1, w_w2, a0, w_a1, w_a2, w_g1, w_g2, k_k, k_a, r_k, gn_g, gn_b, w_o,
                 ln_g, ln_b):
    m = bsz * t
    x3 = x.reshape(bsz, t, D_MODEL)
    xp = jnp.concatenate([jnp.zeros((bsz, 1, D_MODEL), F32), x3[:, :-1]], axis=1).reshape(m, D_MODEL)
    tm = min(m, 256)
    r, k, v = _rwkv_rkv(x, xp, mu, w_rkv, tm, "rwkv_rkv")
    lw, a, g = _rwkv_lora(x, xp, mu, w0, w_w1, w_w2, a0, w_a1, w_a2, w_g1, w_g2, tm, "rwkv_lora")
    y, s = _rwkv_chunks(r, k, v, lw, a, g, k_k, k_a, r_k, gn_g, gn_b, bsz, t, "rwkv_chunks")
    x1 = _out_ln(x, y, w_o, ln_g, ln_b, min(m, 512), "rwkv_out")
    return x1, x3[:, -1], s


def _eye(n):
    return _iota2((n, n), 0) == _iota2((n, n), 1)


def _to_col(row, eye):
    return jnp.sum(jnp.where(eye, row, 0.0), axis=1, keepdims=True)


def _to_row(col, eye):
    return jnp.sum(jnp.where(eye, col, 0.0), axis=0, keepdims=True)


def _row3(a):
    return a.reshape(a.shape[0], 1, a.shape[1])


def _rows_spec(width, col_block=0):
    return pl.BlockSpec((1, 1, width), lambda b: (b, 0, col_block))


def _gdn_conv_step(proj, buf, conv_w, name):
    def body(p_ref, b0, b1, b2, w_ref, o_ref):
        w = w_ref[...]
        y = p_ref[:, :GDN_CONV_DIM] * w[3:4] + b2[...] * w[2:3] + b1[...] * w[1:2] + b0[...] * w[0:1]
        _gdn_act(y, o_ref)

    rows = [proj, buf[:, 0], buf[:, 1], buf[:, 2]]
    return _rowwise(body, rows, [conv_w], [GDN_CONV_DIM], proj.shape[0], name)[0]


def _gdn_step(qkv, bg, proj, s0, norm_g, name):
    bsz = qkv.shape[0]
    rep = GDN_VH // GDN_KH

    def body(q_ref, k_ref, v_ref, bg_ref, z_ref, ng_ref, s_ref, o_ref, so_ref):
        eye = _eye(GDN_HD)
        bgv = bg_ref[0]
        ng = ng_ref[...]
        for kh in range(GDN_KH):
            sl = slice(kh * GDN_HD, (kh + 1) * GDN_HD)
            qrow, krow = q_ref[0, :, sl], k_ref[0, :, sl]
            qcol, kcol = _to_col(qrow, eye), _to_col(krow, eye)
            qk = jnp.sum(qrow * krow, axis=1, keepdims=True)
            for h in range(kh * rep, (kh + 1) * rep):
                vs = slice(h * GDN_HD, (h + 1) * GDN_HD)
                beta = bgv[:, h:h + 1]
                eg = jnp.exp(bgv[:, GDN_VH + h:GDN_VH + h + 1])
                s = s_ref[0, h]
                ks = jnp.sum(kcol * s, axis=0, keepdims=True)
                qs = jnp.sum(qcol * s, axis=0, keepdims=True)
                v_new = beta * (v_ref[0, :, vs] - eg * ks)
                o = eg * qs + qk * v_new
                so_ref[0, h] = s * eg + kcol * v_new
                o = o * lax.rsqrt(jnp.mean(o * o, -1, keepdims=True) + 1e-6) * ng
                o_ref[0, :, vs] = o * _silu(z_ref[0, :, vs])

    st_spec = pl.BlockSpec((1, GDN_VH, GDN_HD, GDN_HD), lambda b: (b, 0, 0, 0))
    o, s = pl.pallas_call(
        body, grid=(bsz,),
        in_specs=[_rows_spec(GDN_QK, 0), _rows_spec(GDN_QK, 1), _rows_spec(GDN_V, 1), _rows_spec(LANES),
                  _rows_spec(GDN_V, GDN_CONV_DIM // GDN_V), pl.BlockSpec((1, GDN_HD), lambda b: (0, 0)), st_spec],
        out_specs=[_rows_spec(GDN_V), st_spec],
        out_shape=[jax.ShapeDtypeStruct((bsz, 1, GDN_V), F32), jax.ShapeDtypeStruct(s0.shape, F32)],
        compiler_params=_cp(("parallel",)), name=name,
    )(_row3(qkv), _row3(qkv), _row3(qkv), _row3(bg), _row3(proj), norm_g.reshape(1, -1), s0)
    return o.reshape(bsz, GDN_V), s


def _gdn_sample(x, buf, s0, w_in, conv_w, a_log, dt_bias, norm_g, w_out, ln_g, ln_b):
    m = x.shape[0]
    proj = _mm(x, w_in, GDN_CONV_DIM + GDN_V, m, 512, "gdn_proj_s")
    bg = _gdn_gates(x, w_in, a_log, dt_bias, m, "gdn_gates_s")
    qkv = _gdn_conv_step(proj, buf, conv_w, "gdn_conv_s")
    o, s = _gdn_step(qkv, bg, proj, s0, norm_g, "gdn_step_s")
    x1 = _out_ln(x, o, w_out, ln_g, ln_b, m, "gdn_out_s")
    new_buf = jnp.concatenate([buf[:, 1:], proj[:, None, :GDN_CONV_DIM]], axis=1)
    return x1, new_buf, s


def _s5_step(x, s_re, s_im, lr, li, bcat, ccat, d_skip, name):
    def body(x_ref, sr_ref, si_ref, lr_ref, li_ref, b_ref, c_ref, d_ref, z_ref, hr_ref, hi_ref):
        x = x_ref[...]
        dsk = d_ref[...]
        for j in range(S5_NT):
            hs = slice(j * S5_HT, (j + 1) * S5_HT)
            xs = x[:, j * LANES:(j + 1) * LANES]
            bu = _dot(xs, b_ref[j])
            lam_r, lam_i = lr_ref[:, hs], li_ref[:, hs]
            sr, si = sr_ref[:, hs], si_ref[:, hs]
            h_re = lam_r * sr - lam_i * si + bu[:, :S5_HT]
            h_im = lam_r * si + lam_i * sr + bu[:, S5_HT:]
            hr_ref[:, hs] = h_re
            hi_ref[:, hs] = h_im
            cj = c_ref[j]
            y = _dot(h_re, cj[:S5_HT]) + _dot(h_im, cj[S5_HT:])
            z_ref[:, j * LANES:(j + 1) * LANES] = _gelu(y + dsk[:, j * LANES:(j + 1) * LANES] * xs)

    fulls = [lr, li, bcat, ccat, d_skip.reshape(1, -1)]
    return _rowwise(body, [x, s_re, s_im], fulls, [D_MODEL, S5_H, S5_H], x.shape[0], name)


def _s5_sample(x, s_re0, s_im0, a_re, a_im, log_dt, b_re, b_im, c_re, c_im, d_skip, w_o, w_gate, ln_g, ln_b):
    m = x.shape[0]
    lr, li, bcat, ccat = _s5_weights(a_re, a_im, log_dt, b_re, b_im, c_re, c_im)
    z, h_re, h_im = _s5_step(x, s_re0.reshape(m, S5_H), s_im0.reshape(m, S5_H), lr, li, bcat, ccat, d_skip, "s5_step_s")
    x1 = _s5_out(x, z, w_o, w_gate, ln_g, ln_b, m, "s5_out_s")
    return x1, h_re.reshape(m, S5_GROUPS, S5_STATE), h_im.reshape(m, S5_GROUPS, S5_STATE)


def _gla_step(proj, logd, s0, norm_g, name):
    bsz = proj.shape[0]
    scale = GLA_DK ** -0.5

    def body(q_ref, k_ref, v_ref, gate_ref, ld_ref, ng_ref, s_ref, o_ref, so_ref):
        eye = _eye(GLA_DK)
        ng = ng_ref[...]
        for h in range(GLA_HEADS):
            ks = slice(h * GLA_DK, (h + 1) * GLA_DK)
            vs = slice(h * GLA_DV, (h + 1) * GLA_DV)
            bc = ld_ref[0, :, ks]
            k = k_ref[0, :, ks]
            v = v_ref[0, :, vs]
            q_in = q_ref[0, :, ks] * scale * jnp.exp(bc)
            a = jnp.sum(q_in * (k * jnp.exp(-bc)), axis=1, keepdims=True)
            s = s_ref[0, h]
            o = jnp.sum(_to_col(q_in, eye) * s, axis=0, keepdims=True) + a * v
            so_ref[0, h] = s * _to_col(jnp.exp(bc), eye) + _to_col(k, eye) * v
            o = o * lax.rsqrt(jnp.mean(o * o, -1, keepdims=True) + 1e-6) * ng
            o_ref[0, :, vs] = o * _silu(gate_ref[0, :, vs])

    st_spec = pl.BlockSpec((1, GLA_HEADS, GLA_DK, GLA_DV), lambda b: (b, 0, 0, 0))
    o, s = pl.pallas_call(
        body, grid=(bsz,),
        in_specs=[_rows_spec(GLA_QK, 0), _rows_spec(GLA_QK, 1), _rows_spec(GLA_V, 1), _rows_spec(GLA_V, 2),
                  _rows_spec(GLA_QK), pl.BlockSpec((1, GLA_DV), lambda b: (0, 0)), st_spec],
        out_specs=[_rows_spec(GLA_V), st_spec],
        out_shape=[jax.ShapeDtypeStruct((bsz, 1, GLA_V), F32), jax.ShapeDtypeStruct(s0.shape, F32)],
        compiler_params=_cp(("parallel",)), name=name,
    )(_row3(proj), _row3(proj), _row3(proj), _row3(proj), _row3(logd), norm_g.reshape(1, -1), s0)
    return o.reshape(bsz, GLA_V), s


def _gla_sample(x, s0, w_in, w_gk2, b_gk, norm_g, w_out, ln_g, ln_b):
    m = x.shape[0]
    proj = _mm(x, w_in, GLA_PROJ, m, 512, "gla_proj_s")
    logd = _gla_logd(x, w_in, w_gk2, b_gk, m, "gla_logd_s")
    o, s = _gla_step(proj, logd, s0, norm_g, "gla_step_s")
    x1 = _out_ln(x, o, w_out, ln_g, ln_b, m, "gla_out_s")
    return x1, s


def _dot3_r(x, l):
    hi = x.astype(BF16).astype(F32)
    r1 = x - hi
    mid = r1.astype(BF16).astype(F32)
    lo = r1 - mid
    return _dot(hi, l) + _dot(mid, l) + _dot(lo, l)


def _head_ones():
    return jnp.kron(jnp.eye(RW_H, dtype=F32), jnp.ones((RW_HD, RW_HD), F32))


def _rwkv_step_prep(r, k, v, lw, a, k_k, k_a, name):
    bsz = r.shape[0]

    def body(r_ref, k_ref, v_ref, lw_ref, a_ref, kk_ref, ka_ref, ones_ref, rt, kt, vt, kkt, bt, dt, kh_ref):
        k, a = k_ref[...], a_ref[...]
        kk = k * kk_ref[...]
        kkn = kk * lax.rsqrt(_dot3_r(kk * kk, ones_ref[...]) + 1e-6)
        kh = k * (1.0 + (a - 1.0) * ka_ref[...])
        kh_ref[...] = kh
        rt[...] = r_ref[...].T
        kt[...] = kh.T
        vt[...] = v_ref[...].T
        kkt[...] = kkn.T
        bt[...] = (kkn * a).T
        dt[...] = jnp.exp(lw_ref[...]).T

    args = (r, k, v, lw, a, k_k.reshape(1, -1), k_a.reshape(1, -1), _head_ones())
    tshape = jax.ShapeDtypeStruct((D_MODEL, bsz), F32)
    return pl.pallas_call(
        body, grid=(1,), in_specs=[_full_spec(t) for t in args],
        out_specs=[pl.BlockSpec((D_MODEL, bsz), lambda i: (0, 0))] * 6 + [pl.BlockSpec((bsz, D_MODEL), lambda i: (0, 0))],
        out_shape=[tshape] * 6 + [jax.ShapeDtypeStruct((bsz, D_MODEL), F32)],
        compiler_params=_cp(("arbitrary",)), name=name)(*args)


def _rwkv_step_lanes(rt, kt, vt, kkt, bt, dt, s_t, name):
    hd = RW_HD
    bsz = s_t.shape[-1]

    def body(rt_ref, kt_ref, vt_ref, kkt_ref, bt_ref, dt_ref, s_ref, y_ref, so_ref):
        hs = pl.ds(pl.multiple_of(pl.program_id(0) * hd, hd), hd)
        r_h, k_h, v_h, kk_h, b_h, d_h = (ref[hs, :] for ref in (rt_ref, kt_ref, vt_ref, kkt_ref, bt_ref, dt_ref))
        ys = []
        for vi in range(hd):
            s = s_ref[0, vi]
            sa = -jnp.sum(s * kk_h, axis=0, keepdims=True)
            s_new = s * d_h + sa * b_h + v_h[vi:vi + 1] * k_h
            so_ref[0, vi] = s_new
            ys.append(jnp.sum(s_new * r_h, axis=0, keepdims=True))
        y_ref[...] = jnp.concatenate(ys, axis=0)

    vec = pl.BlockSpec((D_MODEL, bsz), lambda h: (0, 0))
    st = pl.BlockSpec((1, hd, hd, bsz), lambda h: (h, 0, 0, 0))
    return pl.pallas_call(
        body, grid=(RW_H,), in_specs=[vec] * 6 + [st],
        out_specs=[pl.BlockSpec((hd, bsz), lambda h: (h, 0)), st],
        out_shape=[jax.ShapeDtypeStruct((D_MODEL, bsz), F32), jax.ShapeDtypeStruct(s_t.shape, F32)],
        compiler_params=_cp(("parallel",)), name=name)(rt, kt, vt, kkt, bt, dt, s_t)


def _rwkv_step_out(x, yt, r, kh, v, g, r_k, gn_g, gn_b, w_o, ln_g, ln_b, name):
    def body(x_ref, yt_ref, r_ref, k_ref, v_ref, g_ref, rk_ref, gg_ref, gb_ref, ones_ref, wo_ref, lg_ref, lb_ref,
             o_ref):
        ones = ones_ref[...]
        y = yt_ref[...].T
        yc = y - _dot3_r(y, ones) * (1.0 / RW_HD)
        yn = yc * lax.rsqrt(_dot3_r(yc * yc, ones) * (1.0 / RW_HD) + RW_GN_EPS) * gg_ref[...] + gb_ref[...]
        bonus = _dot3_r(r_ref[...] * k_ref[...] * rk_ref[...], ones) * v_ref[...]
        o = (yn + bonus) * g_ref[...]
        o_ref[...] = _ln(ALPHA * x_ref[...] + _dot(o, wo_ref[...]), lg_ref[...], lb_ref[...])

    row = lambda w: w.reshape(1, D_MODEL)
    args = (x, yt, r, kh, v, g, row(r_k), row(gn_g), row(gn_b), _head_ones(), w_o, row(ln_g), row(ln_b))
    return pl.pallas_call(
        body, grid=(1,), in_specs=[_full_spec(t) for t in args],
        out_specs=pl.BlockSpec(x.shape, lambda i: (0, 0)),
        out_shape=jax.ShapeDtypeStruct(x.shape, F32),
        compiler_params=_cp(("arbitrary",)), name=name)(*args)


def _rwkv_sample(x, shift0, s0, mu, w_rkv, w0, w_w1, w_w2, a0, w_a1, w_a2, w_g1, w_g2, k_k, k_a, r_k, gn_g, gn_b, w_o,
                 ln_g, ln_b):
    m = x.shape[0]
    r, k, v = _rwkv_rkv(x, shift0, mu, w_rkv, m, "rwkv_rkv_s")
    lw, a, g = _rwkv_lora(x, shift0, mu, w0, w_w1, w_w2, a0, w_a1, w_a2, w_g1, w_g2, m, "rwkv_lora_s")
    rt, kt, vt, kkt, bt, dt, kh = _rwkv_step_prep(r, k, v, lw, a, k_k, k_a, "rwkv_prep_s")
    yt, s_t = _rwkv_step_lanes(rt, kt, vt, kkt, bt, dt, jnp.transpose(s0, (1, 2, 3, 0)), "rwkv_step_s")
    x1 = _rwkv_step_out(x, yt, r, kh, v, g, r_k, gn_g, gn_b, w_o, ln_g, ln_b, "rwkv_out_s")
    return x1, x, jnp.transpose(s_t, (3, 0, 1, 2))


def kernel(x_prompt, x_sample, state_gdn_conv, state_gdn, state_s5_re, state_s5_im, state_gla, state_rwkv_shift,
           state_rwkv, p_prompt, p_sample, gdn_w_in, gdn_conv_w, gdn_a_log, gdn_dt_bias, gdn_norm_g, gdn_w_out,
           s5_a_re, s5_a_im, s5_log_dt, s5_b_re, s5_b_im, s5_c_re, s5_c_im, s5_d, s5_w_o, s5_w_gate,
           gla_w_in, gla_w_gk2, gla_b_gk, gla_norm_g, gla_w_out,
           rwkv_mu, rwkv_w_rkv, rwkv_w0, rwkv_w_w1, rwkv_w_w2, rwkv_a0, rwkv_w_a1, rwkv_w_a2, rwkv_w_g1, rwkv_w_g2,
           rwkv_k_k, rwkv_k_a, rwkv_r_k, rwkv_ln_g, rwkv_ln_b, rwkv_w_o,
           ln_mix_g, ln_mix_b, ln_ffn_g, ln_ffn_b, mlp_w1, mlp_w2, ple_w, ple_gate_w):
    bsz, t, d = x_prompt.shape
    bs = x_sample.shape[0]
    gdn_w = (gdn_w_in[0], gdn_conv_w[0], gdn_a_log[0], gdn_dt_bias[0], gdn_norm_g[0], gdn_w_out[0])
    s5_w = (s5_a_re[0], s5_a_im[0], s5_log_dt[0], s5_b_re[0], s5_b_im[0], s5_c_re[0], s5_c_im[0], s5_d[0],
            s5_w_o[0], s5_w_gate[0])
    gla_w = (gla_w_in[0], gla_w_gk2[0], gla_b_gk[0], gla_norm_g[0], gla_w_out[0])
    rwkv_w = (rwkv_mu[0], rwkv_w_rkv[0], rwkv_w0[0], rwkv_w_w1[0], rwkv_w_w2[0], rwkv_a0[0], rwkv_w_a1[0],
              rwkv_w_a2[0], rwkv_w_g1[0], rwkv_w_g2[0], rwkv_k_k[0], rwkv_k_a[0], rwkv_r_k[0], rwkv_ln_g[0],
              rwkv_ln_b[0], rwkv_w_o[0])

    def ffn(x, p, i, tm, tag):
        return _mlp_ple(x, p, mlp_w1[i], mlp_w2[i], ln_ffn_g[i], ln_ffn_b[i], ple_w[i], ple_gate_w[i], tm, 1024,
                        f"mlp{i}_{tag}")

    xp = x_prompt.reshape(bsz * t, d)
    pp = p_prompt.reshape(DEPTH, bsz * t, D_PLE)
    tm_p = min(bsz * t, 512)
    xp, gc_p, gs_p = _gdn_prompt(xp, bsz, t, *gdn_w, ln_mix_g[0], ln_mix_b[0])
    xp = ffn(xp, pp[0], 0, tm_p, "p")
    xp, sr_p, si_p = _s5_prompt(xp, bsz, t, *s5_w, ln_mix_g[1], ln_mix_b[1])
    xp = ffn(xp, pp[1], 1, tm_p, "p")
    xp, la_p = _gla_prompt(xp, bsz, t, *gla_w, ln_mix_g[2], ln_mix_b[2])
    xp = ffn(xp, pp[2], 2, tm_p, "p")
    xp, sh_p, rs_p = _rwkv_prompt(xp, bsz, t, *rwkv_w, ln_mix_g[3], ln_mix_b[3])
    xp = ffn(xp, pp[3], 3, tm_p, "p")

    xs = x_sample.reshape(bs, d)
    ps = p_sample.reshape(DEPTH, bs, D_PLE)
    xs, gc_s, gs_s = _gdn_sample(xs, state_gdn_conv[0], state_gdn[0], *gdn_w, ln_mix_g[0], ln_mix_b[0])
    xs = ffn(xs, ps[0], 0, bs, "s")
    xs, sr_s, si_s = _s5_sample(xs, state_s5_re[0], state_s5_im[0], *s5_w, ln_mix_g[1], ln_mix_b[1])
    xs = ffn(xs, ps[1], 1, bs, "s")
    xs, la_s = _gla_sample(xs, state_gla[0], *gla_w, ln_mix_g[2], ln_mix_b[2])
    xs = ffn(xs, ps[2], 2, bs, "s")
    xs, sh_s, rs_s = _rwkv_sample(xs, state_rwkv_shift[0], state_rwkv[0], *rwkv_w, ln_mix_g[3], ln_mix_b[3])
    xs = ffn(xs, ps[3], 3, bs, "s")

    e = lambda a: a[None]
    return (xp.reshape(bsz, t, d), xs.reshape(bs, 1, d), e(gc_p), e(gc_s), e(gs_p), e(gs_s), e(sr_p), e(sr_s),
            e(si_p), e(si_s), e(la_p), e(la_s), e(sh_p), e(sh_s), e(rs_p), e(rs_s))
```

```python
import math
from typing import Callable, NamedTuple

import jax
import jax.numpy as jnp
from jax import lax
from jax.experimental import pallas as pl
from jax.experimental.pallas import tpu as pltpu

F32 = jnp.float32
BF16 = jnp.bfloat16

D_MODEL = 1024
DEPTH = 4
D_PLE = 256
D_FF = 4 * D_MODEL
LN_EPS = 1e-5
ALPHA = (2.0 * DEPTH) ** 0.25

GDN_HD = 128
GDN_KH = 8
GDN_VH = 16
GDN_QK = GDN_KH * GDN_HD
GDN_V = GDN_VH * GDN_HD
GDN_CONV_DIM = 2 * GDN_QK + GDN_V
GDN_CONV_W = 4
GDN_CHUNK = 64

S5_GROUP = 16
S5_GROUPS = D_MODEL // S5_GROUP
S5_STATE = 64
S5_H = S5_GROUPS * S5_STATE

GLA_HEADS = 4
GLA_DK = 128
GLA_DV = 256
GLA_QK = GLA_HEADS * GLA_DK
GLA_V = GLA_HEADS * GLA_DV
GLA_RANK = 16
GLA_GATE_NORM = 16.0
GLA_CHUNK = 16

RW_HD = 64
RW_H = D_MODEL // RW_HD
RW_GN_EPS = 64e-5
RW_CHUNK = 64

LANES = 128
SUBLANES = 8
VMEM_LIMIT = 56 * 1024 * 1024


def _cp(sem, vmem=VMEM_LIMIT):
    return pltpu.CompilerParams(dimension_semantics=sem, vmem_limit_bytes=vmem)


def _dot(a, b):
    return jnp.dot(a, b, preferred_element_type=F32)


def _dot_nt(a, b):
    return lax.dot_general(a, b, (((1,), (1,)), ((), ())), preferred_element_type=F32)


def _dot_tn(a, b):
    return lax.dot_general(a, b, (((0,), (0,)), ((), ())), preferred_element_type=F32)


def _dot3(l, x):
    hi = x.astype(BF16).astype(F32)
    r1 = x - hi
    mid = r1.astype(BF16).astype(F32)
    lo = r1 - mid
    return _dot(l, hi) + _dot(l, mid) + _dot(l, lo)


def _softplus(x):
    return jnp.maximum(x, 0.0) + jnp.log(1.0 + jnp.exp(-jnp.abs(x)))


def _sigmoid(x):
    return 1.0 / (1.0 + jnp.exp(-x))


def _silu(x):
    return x * _sigmoid(x)


def _ln(x, g, b):
    xc = x - jnp.mean(x, -1, keepdims=True)
    var = jnp.mean(xc * xc, -1, keepdims=True)
    return xc * lax.rsqrt(var + LN_EPS) * g + b


def _iota2(shape, axis):
    return lax.broadcasted_iota(jnp.int32, shape, axis)


def _tri(n, strict=False, block=None):
    r = _iota2((n, n), 0)
    c = _iota2((n, n), 1)
    m = (r > c) if strict else (r >= c)
    if block is not None:
        m = m & ((r // block) == (c // block))
    return m


def _full_spec(a):
    n = a.ndim
    return pl.BlockSpec(a.shape, lambda *_: (0,) * n)


def _mm(x, w, n_cols, tm, tn, name, side_fn, side_fulls, side_cols):
    m, k = x.shape

    def body(x_ref, w_ref, *rest):
        side_refs, o_ref, s_ref, xb_ref = rest[:-3], rest[-3], rest[-2], rest[-1]

        @pl.when(pl.program_id(1) == 0)
        def _():
            xb_ref[...] = x_ref[...].astype(BF16)
            s_ref[...] = side_fn(xb_ref[...], *side_refs)

        o_ref[...] = _dot(xb_ref[...], w_ref[...])

    return pl.pallas_call(
        body, grid=(m // tm, n_cols // tn),
        in_specs=[pl.BlockSpec((tm, k), lambda i, j: (i, 0)), pl.BlockSpec((k, tn), lambda i, j: (0, j))]
        + [_full_spec(a) for a in side_fulls],
        out_specs=[pl.BlockSpec((tm, tn), lambda i, j: (i, j)), pl.BlockSpec((tm, side_cols), lambda i, j: (i, 0))],
        out_shape=[jax.ShapeDtypeStruct((m, n_cols), F32), jax.ShapeDtypeStruct((m, side_cols), F32)],
        scratch_shapes=[pltpu.VMEM((tm, k), BF16)],
        compiler_params=_cp(("parallel", "arbitrary")), name=name)(x, w.astype(BF16), *side_fulls)


def _rowwise(body, rows, fulls, out_cols, tm, name):
    m = rows[0].shape[0]
    in_specs = [pl.BlockSpec((tm, r.shape[1]), lambda i: (i, 0)) for r in rows] + [_full_spec(f) for f in fulls]
    out_specs = [pl.BlockSpec((tm, c), lambda i: (i, 0)) for c in out_cols]
    out_shape = [jax.ShapeDtypeStruct((m, c), F32) for c in out_cols]
    res = pl.pallas_call(body, grid=(m // tm,), in_specs=in_specs, out_specs=out_specs, out_shape=out_shape,
                         compiler_params=_cp(("parallel",)), name=name)(*rows, *fulls)
    return res


def _resident_spec(a):
    n = a.ndim
    return pl.BlockSpec(a.shape, lambda *_: (0,) * n, pipeline_mode=pl.Buffered(1))


class Mix(NamedTuple):
    rows: tuple
    fulls: tuple
    fn: Callable


def _proj_mix(o, w_out):
    return Mix((o,), (w_out.astype(BF16),), lambda o_t, w_ref: _dot(o_t.astype(BF16), w_ref[...]))


def _block_tail(x, mix, lm_g, lm_b, p_all, layer, w1, w2, g, b, ple_w, gate_w, tm, tf, name):
    m = x.shape[0]
    nf = D_FF // tf
    nrow, nfull = len(mix.rows), len(mix.fulls)

    def body(*refs):
        x_ref, p_ref = refs[0], refs[1 + nrow]
        row_refs = refs[1:1 + nrow]
        full_refs = refs[2 + nrow:2 + nrow + nfull]
        lmg_ref, lmb_ref, w1_ref, w2_ref, g_ref, b_ref, pw_ref, gw_ref, o_ref, h_ref = refs[2 + nrow + nfull:]
        x1 = _ln(ALPHA * x_ref[...] + mix.fn(*[r[...] for r in row_refs], *full_refs), lmg_ref[...], lmb_ref[...])
        xb = x1.astype(BF16)
        for f in range(nf):
            h = jnp.maximum(_dot(xb, w1_ref[:, f * tf:(f + 1) * tf]), 0.0)
            h_ref[:, f * tf:(f + 1) * tf] = (h * h).astype(BF16)
        x2 = _ln(ALPHA * x1 + _dot(h_ref[...], w2_ref[...]), g_ref[...], b_ref[...])
        gate = _sigmoid(_dot(x2.astype(BF16), gw_ref[...]))
        o_ref[...] = x2 + _dot(p_ref[...].astype(BF16), pw_ref[...]) * gate

    row = lambda v: v.reshape(1, -1)
    fulls = mix.fulls + (row(lm_g), row(lm_b), w1.astype(BF16), w2.astype(BF16), row(g), row(b), ple_w.astype(BF16),
                         gate_w.astype(BF16))
    return pl.pallas_call(
        body, grid=(m // tm,),
        in_specs=[pl.BlockSpec((tm, D_MODEL), lambda i: (i, 0))]
        + [pl.BlockSpec((tm, r.shape[1]), lambda i: (i, 0)) for r in mix.rows]
        + [pl.BlockSpec((None, tm, D_PLE), lambda i: (layer, i, 0))]
        + [_resident_spec(a) for a in fulls],
        out_specs=pl.BlockSpec((tm, D_MODEL), lambda i: (i, 0)),
        out_shape=jax.ShapeDtypeStruct((m, D_MODEL), F32),
        scratch_shapes=[pltpu.VMEM((tm, D_FF), BF16)],
        compiler_params=_cp(("parallel",)), name=name,
    )(x, *mix.rows, p_all, *fulls)


def _gdn_proj(x, w_in, a_log, dt_bias, tm, name):
    w_ba = jnp.pad(w_in[:, GDN_CONV_DIM + GDN_V:], ((0, 0), (0, LANES - 2 * GDN_VH))).astype(BF16)
    pad = lambda v: jnp.pad(v.reshape(1, -1), ((0, 0), (GDN_VH, LANES - 2 * GDN_VH)))

    def gates(xb, w_ref, al_ref, dt_ref):
        y = _dot(xb, w_ref[...])
        lane = _iota2(y.shape, 1)
        g = -jnp.exp(al_ref[...]) * _softplus(y + dt_ref[...])
        return jnp.where(lane < GDN_VH, _sigmoid(y), g)

    return _mm(x, w_in, GDN_CONV_DIM + GDN_V, tm, 512, name, gates, (w_ba, pad(a_log), pad(dt_bias)), LANES)


def _l2n(x, scale):
    return x * (lax.rsqrt(jnp.sum(x * x, -1, keepdims=True) + 1e-6) * scale)


def _gdn_act(y, o_ref):
    y = _silu(y)
    for h in range(GDN_KH):
        sl = slice(h * GDN_HD, (h + 1) * GDN_HD)
        o_ref[:, sl] = _l2n(y[:, sl], GDN_HD ** -0.5)
        sl = slice(GDN_QK + h * GDN_HD, GDN_QK + (h + 1) * GDN_HD)
        o_ref[:, sl] = _l2n(y[:, sl], 1.0)
    o_ref[:, 2 * GDN_QK:] = y[:, 2 * GDN_QK:]


def _gdn_conv_prompt(proj, conv_w, bsz, t, tt, name):
    nt = t // tt
    c = GDN_CONV_DIM

    def body(x_ref, w_ref, o_ref, tail_ref, buf_ref):
        i = pl.program_id(1)

        @pl.when(i == 0)
        def _():
            buf_ref[pl.ds(0, SUBLANES)] = jnp.zeros((SUBLANES, c), F32)

        x = x_ref[...]
        w = w_ref[...]
        buf_ref[pl.ds(SUBLANES, tt)] = x
        y = x * w[3:4]
        for j in range(1, GDN_CONV_W):
            y = y + buf_ref[pl.ds(SUBLANES - j, tt)] * w[3 - j:4 - j]
        _gdn_act(y, o_ref)
        buf_ref[pl.ds(0, SUBLANES)] = x[tt - SUBLANES:]
        tail_ref[0] = x[tt - SUBLANES:]

    return pl.pallas_call(
        body, grid=(bsz, nt),
        in_specs=[pl.BlockSpec((tt, c), lambda b, i: (b * nt + i, 0)), _full_spec(conv_w)],
        out_specs=[pl.BlockSpec((tt, c), lambda b, i: (b * nt + i, 0)),
                   pl.BlockSpec((1, SUBLANES, c), lambda b, i: (b, 0, 0))],
        out_shape=[jax.ShapeDtypeStruct((bsz * t, c), F32), jax.ShapeDtypeStruct((bsz, SUBLANES, c), F32)],
        scratch_shapes=[pltpu.VMEM((SUBLANES + tt, c), F32)],
        compiler_params=_cp(("parallel", "arbitrary")), name=name)(proj, conv_w)


def _neumann_inv(ms, n):
    eye = (_iota2((n, n), 0) == _iota2((n, n), 1)).astype(F32)
    ts = [eye - m for m in ms]
    ps = [_dot(m, m) for m in ms]
    k = 2
    while True:
        ts = [t + _dot(t, p) for t, p in zip(ts, ps)]
        k *= 2
        if k >= n:
            return ts
        ps = [_dot(p, p) for p in ps]


def _gdn_chunks(qkv, bg, proj, norm_g, bsz, t, name):
    c = GDN_CHUNK
    nc = t // c
    z_blk = GDN_CONV_DIM // GDN_V

    def body(q_ref, k_ref, v_ref, bg_ref, z_ref, ng_ref, o_ref, s_out_ref, s_ref):
        i = pl.program_id(1)

        @pl.when(i == 0)
        def _():
            s_ref[...] = jnp.zeros_like(s_ref)

        bgv = bg_ref[...]
        ltri = _tri(c).astype(F32)
        gc = _dot3(ltri, bgv)
        gct = jnp.concatenate([gc, jnp.zeros_like(gc)], axis=0).T
        incl = _tri(c)
        strict = _tri(c, strict=True)
        ng = ng_ref[...]
        rep = GDN_VH // GDN_KH
        heads = range(GDN_VH)
        hsl = lambda n: slice(n * GDN_HD, (n + 1) * GDN_HD)
        gram = [_dot_nt(jnp.concatenate([k_ref[:, hsl(n)], q_ref[:, hsl(n)]], axis=0), k_ref[:, hsl(n)])
                for n in range(GDN_KH)]
        ms, aqks, rhss, q_ins, k_outs, g_ends = [], [], [], [], [], []
        for h in heads:
            kh = h // rep
            kk = k_ref[:, hsl(kh)]
            beta = bgv[:, h:h + 1]
            gcol = gc[:, GDN_VH + h:GDN_VH + h + 1]
            grow = gct[GDN_VH + h:GDN_VH + h + 1, :c]
            decay = jnp.where(incl, jnp.exp(jnp.where(incl, gcol - grow, 0.0)), 0.0)
            ms.append(jnp.where(strict, gram[kh][:c] * beta * decay, 0.0))
            aqks.append(gram[kh][c:] * decay)
            egc = jnp.exp(gcol)
            rhss.append(jnp.concatenate([v_ref[:, hsl(h)] * beta, kk * (beta * egc)], axis=1))
            q_ins.append(q_ref[:, hsl(kh)] * egc)
            glast = gc[c - 1:c, GDN_VH + h:GDN_VH + h + 1]
            k_outs.append(kk * jnp.exp(glast - gcol))
            g_ends.append(jnp.exp(glast))
        tinvs = _neumann_inv(ms, c)
        sols = [_dot(tinvs[h], rhss[h]) for h in heads]
        wqs = [_dot(jnp.concatenate([sols[h][:, GDN_HD:], q_ins[h]], axis=0), s_ref[h]) for h in heads]
        v_news = [sols[h][:, :GDN_HD] - wqs[h][:c] for h in heads]
        avs = [_dot(aqks[h], v_news[h]) for h in heads]
        kvs = [_dot_tn(k_outs[h], v_news[h]) for h in heads]
        for h in heads:
            s_ref[h] = s_ref[h] * g_ends[h] + kvs[h]
            o = wqs[h][c:] + avs[h]
            o = o * lax.rsqrt(jnp.mean(o * o, -1, keepdims=True) + 1e-6) * ng
            o_ref[:, hsl(h)] = o * _silu(z_ref[:, hsl(h)])

        @pl.when(i == nc - 1)
        def _():
            s_out_ref[0] = s_ref[...]

    return pl.pallas_call(
        body, grid=(bsz, nc),
        in_specs=[pl.BlockSpec((c, GDN_QK), lambda b, i: (b * nc + i, 0)),
                  pl.BlockSpec((c, GDN_QK), lambda b, i: (b * nc + i, 1)),
                  pl.BlockSpec((c, GDN_V), lambda b, i: (b * nc + i, 1)),
                  pl.BlockSpec((c, LANES), lambda b, i: (b * nc + i, 0)),
                  pl.BlockSpec((c, GDN_V), lambda b, i: (b * nc + i, z_blk)),
                  pl.BlockSpec((1, GDN_HD), lambda b, i: (0, 0))],
        out_specs=[pl.BlockSpec((c, GDN_V), lambda b, i: (b * nc + i, 0)),
                   pl.BlockSpec((1, GDN_VH, GDN_HD, GDN_HD), lambda b, i: (b, 0, 0, 0))],
        out_shape=[jax.ShapeDtypeStruct((bsz * t, GDN_V), F32),
                   jax.ShapeDtypeStruct((bsz, GDN_VH, GDN_HD, GDN_HD), F32)],
        scratch_shapes=[pltpu.VMEM((GDN_VH, GDN_HD, GDN_HD), F32)],
        compiler_params=_cp(("parallel", "arbitrary")), name=name,
    )(qkv, qkv, qkv, bg, proj, norm_g.reshape(1, -1))


def _gdn_prompt(x, bsz, t, w_in, conv_w, a_log, dt_bias, norm_g, w_out):
    proj, bg = _gdn_proj(x, w_in, a_log, dt_bias, min(bsz * t, 2048), "gdn_proj")
    qkv, tail = _gdn_conv_prompt(proj, conv_w, bsz, t, min(t, 256), "gdn_conv")
    o, s = _gdn_chunks(qkv, bg, proj, norm_g, bsz, t, "gdn_chunks")
    return _proj_mix(o, w_out), tail[:, SUBLANES - (GDN_CONV_W - 1):], s


def _s5_discretize(a_re, a_im, log_dt, b_re, b_im):
    g, p = a_re.shape
    bt_re = jnp.swapaxes(b_re, 1, 2)
    bt_im = jnp.swapaxes(b_im, 1, 2)

    def body(ar_ref, ai_ref, ldt_ref, br_ref, bi_ref, lr_ref, li_ref, bbr_ref, bbi_ref):
        ar, ai = ar_ref[...], ai_ref[...]
        dt = jnp.exp(ldt_ref[...])
        mag = jnp.exp(ar * dt)
        lr, li = mag * jnp.cos(ai * dt), mag * jnp.sin(ai * dt)
        den = ar * ar + ai * ai
        f_re = ((lr - 1.0) * ar + li * ai) / den
        f_im = (li * ar - (lr - 1.0) * ai) / den
        lr_ref[...] = lr
        li_ref[...] = li
        br, bi = br_ref[...], bi_ref[...]
        fr, fi = f_re[:, None, :], f_im[:, None, :]
        bbr_ref[...] = fr * br - fi * bi
        bbi_ref[...] = fr * bi + fi * br

    args = (a_re, a_im, log_dt.reshape(g, 1), bt_re, bt_im)
    return pl.pallas_call(
        body, grid=(1,), in_specs=[_full_spec(a) for a in args],
        out_specs=[pl.BlockSpec((g, p), lambda i: (0, 0))] * 2 + [pl.BlockSpec(bt_re.shape, lambda i: (0, 0, 0))] * 2,
        out_shape=[jax.ShapeDtypeStruct((g, p), F32)] * 2 + [jax.ShapeDtypeStruct(bt_re.shape, F32)] * 2,
        name="s5_discretize")(*args)


def _blockdiag(a, per):
    g, r, c = a.shape
    a4 = a.reshape(g // per, per, r, c)
    eye = jnp.eye(per, dtype=a.dtype)
    return jnp.einsum("jgrc,gh->jgrhc", a4, eye).reshape(g // per, per * r, per * c)


S5_GPT = LANES // S5_GROUP
S5_NT = S5_GROUPS // S5_GPT
S5_HT = S5_GPT * S5_STATE


def _s5_weights(a_re, a_im, log_dt, b_re, b_im, c_re, c_im):
    lr, li, bbr, bbi = _s5_discretize(a_re, a_im, log_dt, b_re, b_im)
    bcat = jnp.concatenate([_blockdiag(bbr, S5_GPT), _blockdiag(bbi, S5_GPT)], axis=2)
    ccat = jnp.concatenate([_blockdiag(jnp.swapaxes(c_re, 1, 2), S5_GPT),
                            -_blockdiag(jnp.swapaxes(c_im, 1, 2), S5_GPT)], axis=1)
    return lr.reshape(1, S5_H), li.reshape(1, S5_H), bcat, ccat


def _gelu(y):
    return 0.5 * y * (1.0 + jnp.tanh(math.sqrt(2.0 / math.pi) * (y + 0.044715 * (y * y * y))))


def _s5_scan(x, bsz, t, s_re0, s_im0, lr, li, bcat, ccat, d_skip, tc, name):
    d = x.shape[1]
    nt = t // tc
    rows = bsz * tc
    lq = 1024

    def body(x_ref, sr0_ref, si0_ref, lr_ref, li_ref, b_ref, c_ref, d_ref, z_ref, sr_ref, si_ref, hre, him, xs, zs):
        i = pl.program_id(0)

        @pl.when(i == 0)
        def _():
            sr_ref[...] = sr0_ref[...]
            si_ref[...] = si0_ref[...]

        xs[...] = jnp.swapaxes(x_ref[...], 0, 1).reshape(rows, d)

        def x_tile(j):
            return xs[:, j * LANES:(j + 1) * LANES]

        for j in range(S5_NT):
            bu = _dot(x_tile(j), b_ref[j])
            hre[:, j * S5_HT:(j + 1) * S5_HT] = bu[:, :S5_HT]
            him[:, j * S5_HT:(j + 1) * S5_HT] = bu[:, S5_HT:]

        for q in range(S5_H // lq):
            ls = slice(q * lq, (q + 1) * lq)
            lam_r = jnp.broadcast_to(lr_ref[:, ls], (bsz, lq))
            lam_i = jnp.broadcast_to(li_ref[:, ls], (bsz, lq))

            def step(tt, carry):
                sr, si = carry
                idx = pl.ds(pl.multiple_of(tt * bsz, bsz), bsz)
                nr = lam_r * sr - lam_i * si + hre[idx, ls]
                ni = lam_r * si + lam_i * sr + him[idx, ls]
                hre[idx, ls] = nr
                him[idx, ls] = ni
                return nr, ni

            sr, si = lax.fori_loop(0, tc, step, (sr_ref[:, ls], si_ref[:, ls]), unroll=8)
            sr_ref[:, ls] = sr
            si_ref[:, ls] = si

        dsk = d_ref[...]
        for j in range(S5_NT):
            hs = slice(j * S5_HT, (j + 1) * S5_HT)
            cj = c_ref[j]
            y = _dot(hre[:, hs], cj[:S5_HT]) + _dot(him[:, hs], cj[S5_HT:])
            zs[:, j * LANES:(j + 1) * LANES] = _gelu(y + dsk[:, j * LANES:(j + 1) * LANES] * x_tile(j))
        z_ref[...] = jnp.swapaxes(zs[...].reshape(tc, bsz, d), 0, 1)

    fulls = (s_re0, s_im0, lr, li, bcat, ccat, d_skip.reshape(1, d))
    z, s_re, s_im = pl.pallas_call(
        body, grid=(nt,),
        in_specs=[pl.BlockSpec((bsz, tc, d), lambda i: (0, i, 0))] + [_full_spec(a) for a in fulls],
        out_specs=[pl.BlockSpec((bsz, tc, d), lambda i: (0, i, 0)),
                   pl.BlockSpec((bsz, S5_H), lambda i: (0, 0)), pl.BlockSpec((bsz, S5_H), lambda i: (0, 0))],
        out_shape=[jax.ShapeDtypeStruct((bsz, t, d), F32),
                   jax.ShapeDtypeStruct((bsz, S5_H), F32), jax.ShapeDtypeStruct((bsz, S5_H), F32)],
        scratch_shapes=[pltpu.VMEM((rows, S5_H), F32), pltpu.VMEM((rows, S5_H), F32),
                        pltpu.VMEM((rows, d), F32), pltpu.VMEM((rows, d), F32)],
        compiler_params=_cp(("arbitrary",)), name=name)(x.reshape(bsz, t, d), *fulls)
    return z.reshape(bsz * t, d), s_re, s_im


def _s5_mix(z, w_o, w_gate):
    def fn(z_t, wo_ref, wg_ref):
        zb = z_t.astype(BF16)
        return _dot(zb, wo_ref[...]) * _sigmoid(_dot(zb, wg_ref[...]))

    return Mix((z,), (w_o.astype(BF16), w_gate.astype(BF16)), fn)


def _s5_prompt(x, bsz, t, a_re, a_im, log_dt, b_re, b_im, c_re, c_im, d_skip, w_o, w_gate):
    lr, li, bcat, ccat = _s5_weights(a_re, a_im, log_dt, b_re, b_im, c_re, c_im)
    zero = jnp.zeros((bsz, S5_H), F32)
    z, s_re, s_im = _s5_scan(x, bsz, t, zero, zero, lr, li, bcat, ccat, d_skip, min(t, 64), "s5_scan")
    return _s5_mix(z, w_o, w_gate), s_re.reshape(bsz, S5_GROUPS, S5_STATE), s_im.reshape(bsz, S5_GROUPS, S5_STATE)


GLA_PROJ = 2 * GLA_QK + 2 * GLA_V


def _gla_proj(x, w_in, w_gk2, b_gk, tm, name):
    w1 = jnp.pad(w_in[:, GLA_PROJ:], ((0, 0), (0, LANES - GLA_RANK))).astype(BF16)
    w2 = jnp.pad(w_gk2, ((0, LANES - GLA_RANK), (0, 0)))

    def logd(xb, w1_ref, w2_ref, b_ref):
        y = _dot(_dot(xb, w1_ref[...]), w2_ref[...]) + b_ref[...]
        return -_softplus(-y) * (1.0 / GLA_GATE_NORM)

    return _mm(x, w_in, GLA_PROJ, tm, 512, name, logd, (w1, w2, b_gk.reshape(1, -1)), GLA_QK)


def _gla_chunks(proj, logd, norm_g, bsz, t, name):
    c = GLA_CHUNK
    rows = 64
    nr = t // rows
    scale = GLA_DK ** -0.5

    def body(q_ref, k_ref, v_ref, gate_ref, ld_ref, ng_ref, o_ref, s_out_ref, st_ref):
        i = pl.program_id(1)

        @pl.when(i == 0)
        def _():
            st_ref[...] = jnp.zeros_like(st_ref)

        lblk = _tri(rows, block=c).astype(F32)
        bc_all = _dot3(lblk, ld_ref[...])
        incl = _tri(c)
        ng = ng_ref[...]
        subs = range(rows // c)
        heads = range(GLA_HEADS)
        rsl = lambda s: slice(s * c, (s + 1) * c)
        ksl = lambda h: slice(h * GLA_DK, (h + 1) * GLA_DK)
        vsl = lambda h: slice(h * GLA_DV, (h + 1) * GLA_DV)
        q_ins, k_outs, g_ends, a_s = {}, {}, {}, {}
        for s in subs:
            for h in heads:
                bc = bc_all[rsl(s), ksl(h)]
                k = k_ref[rsl(s), ksl(h)]
                bcl = bc[c - 1:c]
                q_ins[s, h] = q_ref[rsl(s), ksl(h)] * scale * jnp.exp(bc)
                k_outs[s, h] = k * jnp.exp(bcl - bc)
                g_ends[s, h] = jnp.exp(bcl)
                a_s[s, h] = jnp.where(incl, _dot_nt(q_ins[s, h], k * jnp.exp(-bc)), 0.0)
        o_intra = {(s, h): _dot(a_s[s, h], v_ref[rsl(s), vsl(h)]) for s in subs for h in heads}
        for s in subs:
            sts = [st_ref[h] for h in heads]
            o_inter = [_dot_nt(q_ins[s, h], sts[h]) for h in heads]
            kvs = [_dot_tn(v_ref[rsl(s), vsl(h)], k_outs[s, h]) for h in heads]
            for h in heads:
                st_ref[h] = sts[h] * g_ends[s, h] + kvs[h]
                o = o_inter[h] + o_intra[s, h]
                o = o * lax.rsqrt(jnp.mean(o * o, -1, keepdims=True) + 1e-6) * ng
                o_ref[rsl(s), vsl(h)] = o * _silu(gate_ref[rsl(s), vsl(h)])

        @pl.when(i == nr - 1)
        def _():
            for h in range(GLA_HEADS):
                s_out_ref[0, h] = st_ref[h].T

    return pl.pallas_call(
        body, grid=(bsz, nr),
        in_specs=[pl.BlockSpec((rows, GLA_QK), lambda b, i: (b * nr + i, 0)),
                  pl.BlockSpec((rows, GLA_QK), lambda b, i: (b * nr + i, 1)),
                  pl.BlockSpec((rows, GLA_V), lambda b, i: (b * nr + i, 1)),
                  pl.BlockSpec((rows, GLA_V), lambda b, i: (b * nr + i, 2)),
                  pl.BlockSpec((rows, GLA_QK), lambda b, i: (b * nr + i, 0)),
                  pl.BlockSpec((1, GLA_DV), lambda b, i: (0, 0))],
        out_specs=[pl.BlockSpec((rows, GLA_V), lambda b, i: (b * nr + i, 0)),
                   pl.BlockSpec((1, GLA_HEADS, GLA_DK, GLA_DV), lambda b, i: (b, 0, 0, 0))],
        out_shape=[jax.ShapeDtypeStruct((bsz * t, GLA_V), F32),
                   jax.ShapeDtypeStruct((bsz, GLA_HEADS, GLA_DK, GLA_DV), F32)],
        scratch_shapes=[pltpu.VMEM((GLA_HEADS, GLA_DV, GLA_DK), F32)],
        compiler_params=_cp(("parallel", "arbitrary")), name=name,
    )(proj, proj, proj, proj, logd, norm_g.reshape(1, -1))


def _gla_prompt(x, bsz, t, w_in, w_gk2, b_gk, norm_g, w_out):
    proj, logd = _gla_proj(x, w_in, w_gk2, b_gk, min(bsz * t, 2048), "gla_proj")
    o, s = _gla_chunks(proj, logd, norm_g, bsz, t, "gla_chunks")
    return _proj_mix(o, w_out), s


def _rwkv_proj(x, xp, bsz, t, mu, w_rkv, w0, w_w1, w_w2, a0, w_a1, w_a2, w_g1, w_g2, tm, name):
    d = D_MODEL
    nt = t // tm
    carried = xp is None
    bf = lambda w: w.astype(BF16)

    def body(*refs):
        if carried:
            x_ref, refs, buf_ref = refs[0], refs[1:-1], refs[-1]
        else:
            x_ref, xp_ref, refs = refs[0], refs[1], refs[2:]
        mu_ref, wr_ref, w0_ref, ww1, ww2, a0_ref, wa1, wa2, wg1, wg2, r_ref, k_ref, v_ref, lw_ref, a_ref, g_ref = refs
        x = x_ref[...]
        if carried:
            @pl.when(pl.program_id(1) == 0)
            def _():
                buf_ref[pl.ds(0, SUBLANES)] = jnp.zeros((SUBLANES, d), F32)

            buf_ref[pl.ds(SUBLANES, tm)] = x
            x_prev = buf_ref[pl.ds(SUBLANES - 1, tm)]
            buf_ref[pl.ds(0, SUBLANES)] = x[tm - SUBLANES:]
        else:
            x_prev = xp_ref[...]
        dx = x_prev - x
        xs = lambda s: (x + dx * mu_ref[s:s + 1]).astype(BF16)
        for s, o_ref in enumerate((r_ref, k_ref, v_ref)):
            o_ref[...] = _dot(xs(s), wr_ref[s])
        lora = lambda h, w2_ref: _dot(h.astype(BF16), w2_ref[...])
        w_log = -_softplus(-(w0_ref[...] + lora(jnp.tanh(_dot(xs(3), ww1[...])), ww2))) - 0.5
        lw_ref[...] = -jnp.exp(w_log)
        a_ref[...] = _sigmoid(a0_ref[...] + lora(_dot(xs(4), wa1[...]), wa2))
        g_ref[...] = lora(_sigmoid(_dot(xs(5), wg1[...])), wg2)

    fulls = (mu, bf(w_rkv), w0.reshape(1, -1), bf(w_w1), bf(w_w2), a0.reshape(1, -1), bf(w_a1), bf(w_a2), bf(w_g1),
             bf(w_g2))
    tile = pl.BlockSpec((tm, d), lambda b, i: (b * nt + i, 0))
    return pl.pallas_call(
        body, grid=(bsz, nt),
        in_specs=[tile] * (1 if carried else 2) + [_resident_spec(a) for a in fulls],
        out_specs=[tile] * 6,
        out_shape=[jax.ShapeDtypeStruct((bsz * t, d), F32)] * 6,
        scratch_shapes=[pltpu.VMEM((SUBLANES + tm, d), F32)] if carried else [],
        compiler_params=_cp(("parallel", "arbitrary")), name=name,
    )(*((x,) if carried else (x, xp)), *fulls)


def _rwkv_head_inputs(r, k, v, a, kk_w, ka_w, sl):
    kraw = k[:, sl]
    kkn = _l2n(kraw * kk_w[:, sl], 1.0)
    ah = a[:, sl]
    kh = kraw * (1.0 + (ah - 1.0) * ka_w[:, sl])
    return r[:, sl], kh, v[:, sl], kkn, kkn * ah


def _rwkv_head_out(y, rh, kh, vh, g, rk_w, lng, lnb, sl):
    yc = y - jnp.mean(y, -1, keepdims=True)
    yn = yc * lax.rsqrt(jnp.mean(yc * yc, -1, keepdims=True) + RW_GN_EPS) * lng[:, sl] + lnb[:, sl]
    bonus = jnp.sum(rh * kh * rk_w[:, sl], -1, keepdims=True) * vh
    return (yn + bonus) * g[:, sl]


def _rwkv_chunks(r, k, v, lw, a, g, k_k, k_a, r_k, ln_g, ln_b, bsz, t, name):
    c = RW_CHUNK
    nc = t // c
    hd = RW_HD
    nb = 1
    d = D_MODEL

    def body(r_ref, k_ref, v_ref, lw_ref, a_ref, g_ref, kk_ref, ka_ref, rk_ref, lng_ref, lnb_ref,
             o_ref, s_out_ref, s_ref):
        i = pl.program_id(1)

        @pl.when(i == 0)
        def _():
            s_ref[...] = jnp.zeros_like(s_ref)

        kk_w, ka_w, rk_w, lng, lnb = kk_ref[...], ka_ref[...], rk_ref[...], lng_ref[...], lnb_ref[...]
        tri = _tri(c).astype(F32)
        strict = _tri(c, strict=True)
        incl = _tri(c)
        hsl = lambda n: slice(n * hd, (n + 1) * hd)
        inst = [(m, h) for m in range(nb) for h in range(RW_H)]
        prep = []
        for m in range(nb):
            lw = lw_ref[m]
            gam = _dot3(tri, lw)
            glast = gam[c - 1:c]
            r, k, a = r_ref[m], k_ref[m], a_ref[m]
            kk = k * kk_w
            scale = jnp.concatenate(
                [jnp.broadcast_to(lax.rsqrt(jnp.sum(kk[:, hsl(h)] * kk[:, hsl(h)], -1, keepdims=True) + 1e-6), (c, hd))
                 for h in range(RW_H)], axis=1)
            kkn = kk * scale
            kh = k * (1.0 + (a - 1.0) * ka_w)
            bh = kkn * a
            e_neg = jnp.exp(-gam)
            e_out = jnp.exp(glast - gam)
            prep.append(dict(a1=kkn * jnp.exp(gam - lw), r1=r * jnp.exp(gam), b1=bh * e_neg, k1=kh * e_neg,
                             b1o=bh * e_out, k1o=kh * e_out, g_end=jnp.exp(glast), r=r, kh=kh, v=v_ref[m],
                             g=g_ref[m]))
        ars = [jnp.concatenate([prep[m]["a1"][:, hsl(h)], prep[m]["r1"][:, hsl(h)]], axis=0) for m, h in inst]
        bks = [jnp.concatenate([prep[m]["b1"][:, hsl(h)], prep[m]["k1"][:, hsl(h)]], axis=0) for m, h in inst]
        vhs = [prep[m]["v"][:, hsl(h)] for m, h in inst]
        n = range(len(inst))
        gmats = [_dot_nt(ars[j], bks[j]) for j in n]
        tinvs = _neumann_inv([jnp.where(strict, gm_[:c, :c], 0.0) for gm_ in gmats], c)
        makvs = [_dot(jnp.where(strict, gmats[j][:c, c:], 0.0), vhs[j]) for j in n]
        rbks = [jnp.concatenate([jnp.where(incl, gm_[c:, :c], 0.0), jnp.where(incl, gm_[c:, c:], 0.0)], axis=1)
                for gm_ in gmats]
        a_ss = [_dot_nt(ars[j], s_ref[m, h]) for j, (m, h) in enumerate(inst)]
        uvs = [jnp.concatenate([_dot(tinvs[j], -a_ss[j][:c] - makvs[j]), vhs[j]], axis=0) for j in n]
        ys = [a_ss[j][c:] + _dot(rbks[j], uvs[j]) for j in n]
        svs = [_dot_tn(uvs[j], jnp.concatenate([prep[m]["b1o"][:, hsl(h)], prep[m]["k1o"][:, hsl(h)]], axis=0))
               for j, (m, h) in enumerate(inst)]
        for j, (m, h) in enumerate(inst):
            s_ref[m, h] = s_ref[m, h] * prep[m]["g_end"][:, hsl(h)] + svs[j]
        rkr = [prep[m]["r"] * prep[m]["kh"] * rk_w for m in range(nb)]
        sum1 = [jnp.sum(ys[j], -1, keepdims=True) for j in n]
        bons = [jnp.sum(rkr[m][:, hsl(h)], -1, keepdims=True) for m, h in inst]
        ycs = [ys[j] - sum1[j] * (1.0 / hd) for j in n]
        sum2 = [jnp.sum(ycs[j] * ycs[j], -1, keepdims=True) for j in n]
        for j, (m, h) in enumerate(inst):
            yn = ycs[j] * lax.rsqrt(sum2[j] * (1.0 / hd) + RW_GN_EPS) * lng[:, hsl(h)] + lnb[:, hsl(h)]
            o_ref[m, :, hsl(h)] = (yn + bons[j] * vhs[j]) * prep[m]["g"][:, hsl(h)]

        @pl.when(i == nc - 1)
        def _():
            s_out_ref[...] = s_ref[...]

    row = lambda w: w.reshape(1, d)
    v3 = lambda z: z.reshape(bsz, t, d)
    blk = pl.BlockSpec((nb, c, d), lambda b, i: (b, i, 0))
    par = pl.BlockSpec((1, d), lambda b, i: (0, 0))
    y, s = pl.pallas_call(
        body, grid=(bsz // nb, nc),
        in_specs=[blk] * 6 + [par] * 5,
        out_specs=[blk, pl.BlockSpec((nb, RW_H, hd, hd), lambda b, i: (b, 0, 0, 0))],
        out_shape=[jax.ShapeDtypeStruct((bsz, t, d), F32), jax.ShapeDtypeStruct((bsz, RW_H, hd, hd), F32)],
        scratch_shapes=[pltpu.VMEM((nb, RW_H, hd, hd), F32)],
        compiler_params=_cp(("parallel", "arbitrary")), name=name,
    )(v3(r), v3(k), v3(v), v3(lw), v3(a), v3(g), row(k_k), row(k_a), row(r_k), row(ln_g), row(ln_b))
    return y.reshape(bsz * t, d), s


def _rwkv_prompt(x, bsz, t, mu, w_rkv, w0, w_w1, w_w2, a0, w_a1, w_a2, w_g1, w_g2, k_k, k_a, r_k, gn_g, gn_b, w_o):
    r, k, v, lw, a, g = _rwkv_proj(x, None, bsz, t, mu, w_rkv, w0, w_w1, w_w2, a0, w_a1, w_a2, w_g1, w_g2,
                                   min(t, 256), "rwkv_proj")
    y, s = _rwkv_chunks(r, k, v, lw, a, g, k_k, k_a, r_k, gn_g, gn_b, bsz, t, "rwkv_chunks")
    return _proj_mix(y, w_o), x.reshape(bsz, t, D_MODEL)[:, -1], s


def _eye(n):
    return _iota2((n, n), 0) == _iota2((n, n), 1)


def _to_col(row, eye):
    return jnp.sum(jnp.where(eye, row, 0.0), axis=1, keepdims=True)


def _to_row(col, eye):
    return jnp.sum(jnp.where(eye, col, 0.0), axis=0, keepdims=True)


def _row3(a):
    return a.reshape(a.shape[0], 1, a.shape[1])


def _rows_spec(width, col_block=0):
    return pl.BlockSpec((1, 1, width), lambda b: (b, 0, col_block))


def _gdn_conv_step(proj, buf, conv_w, name):
    def body(p_ref, b0, b1, b2, w_ref, o_ref):
        w = w_ref[...]
        y = p_ref[:, :GDN_CONV_DIM] * w[3:4] + b2[...] * w[2:3] + b1[...] * w[1:2] + b0[...] * w[0:1]
        _gdn_act(y, o_ref)

    rows = [proj, buf[:, 0], buf[:, 1], buf[:, 2]]
    return _rowwise(body, rows, [conv_w], [GDN_CONV_DIM], proj.shape[0], name)[0]


def _gdn_step(qkv, bg, proj, s0, norm_g, name):
    bsz = qkv.shape[0]
    rep = GDN_VH // GDN_KH

    def body(q_ref, k_ref, v_ref, bg_ref, z_ref, ng_ref, s_ref, o_ref, so_ref):
        eye = _eye(GDN_HD)
        bgv = bg_ref[0]
        ng = ng_ref[...]
        for kh in range(GDN_KH):
            sl = slice(kh * GDN_HD, (kh + 1) * GDN_HD)
            qrow, krow = q_ref[0, :, sl], k_ref[0, :, sl]
            qcol, kcol = _to_col(qrow, eye), _to_col(krow, eye)
            qk = jnp.sum(qrow * krow, axis=1, keepdims=True)
            for h in range(kh * rep, (kh + 1) * rep):
                vs = slice(h * GDN_HD, (h + 1) * GDN_HD)
                beta = bgv[:, h:h + 1]
                eg = jnp.exp(bgv[:, GDN_VH + h:GDN_VH + h + 1])
                s = s_ref[0, h]
                ks = jnp.sum(kcol * s, axis=0, keepdims=True)
                qs = jnp.sum(qcol * s, axis=0, keepdims=True)
                v_new = beta * (v_ref[0, :, vs] - eg * ks)
                o = eg * qs + qk * v_new
                so_ref[0, h] = s * eg + kcol * v_new
                o = o * lax.rsqrt(jnp.mean(o * o, -1, keepdims=True) + 1e-6) * ng
                o_ref[0, :, vs] = o * _silu(z_ref[0, :, vs])

    st_spec = pl.BlockSpec((1, GDN_VH, GDN_HD, GDN_HD), lambda b: (b, 0, 0, 0))
    o, s = pl.pallas_call(
        body, grid=(bsz,),
        in_specs=[_rows_spec(GDN_QK, 0), _rows_spec(GDN_QK, 1), _rows_spec(GDN_V, 1), _rows_spec(LANES),
                  _rows_spec(GDN_V, GDN_CONV_DIM // GDN_V), pl.BlockSpec((1, GDN_HD), lambda b: (0, 0)), st_spec],
        out_specs=[_rows_spec(GDN_V), st_spec],
        out_shape=[jax.ShapeDtypeStruct((bsz, 1, GDN_V), F32), jax.ShapeDtypeStruct(s0.shape, F32)],
        compiler_params=_cp(("parallel",)), name=name,
    )(_row3(qkv), _row3(qkv), _row3(qkv), _row3(bg), _row3(proj), norm_g.reshape(1, -1), s0)
    return o.reshape(bsz, GDN_V), s


def _gdn_sample(x, buf, s0, w_in, conv_w, a_log, dt_bias, norm_g, w_out):
    proj, bg = _gdn_proj(x, w_in, a_log, dt_bias, x.shape[0], "gdn_proj_s")
    qkv = _gdn_conv_step(proj, buf, conv_w, "gdn_conv_s")
    o, s = _gdn_step(qkv, bg, proj, s0, norm_g, "gdn_step_s")
    new_buf = jnp.concatenate([buf[:, 1:], proj[:, None, :GDN_CONV_DIM]], axis=1)
    return _proj_mix(o, w_out), new_buf, s


def _s5_step(x, s_re, s_im, lr, li, bcat, ccat, d_skip, name):
    def body(x_ref, sr_ref, si_ref, lr_ref, li_ref, b_ref, c_ref, d_ref, z_ref, hr_ref, hi_ref):
        x = x_ref[...]
        dsk = d_ref[...]
        for j in range(S5_NT):
            hs = slice(j * S5_HT, (j + 1) * S5_HT)
            xs = x[:, j * LANES:(j + 1) * LANES]
            bu = _dot(xs, b_ref[j])
            lam_r, lam_i = lr_ref[:, hs], li_ref[:, hs]
            sr, si = sr_ref[:, hs], si_ref[:, hs]
            h_re = lam_r * sr - lam_i * si + bu[:, :S5_HT]
            h_im = lam_r * si + lam_i * sr + bu[:, S5_HT:]
            hr_ref[:, hs] = h_re
            hi_ref[:, hs] = h_im
            cj = c_ref[j]
            y = _dot(h_re, cj[:S5_HT]) + _dot(h_im, cj[S5_HT:])
            z_ref[:, j * LANES:(j + 1) * LANES] = _gelu(y + dsk[:, j * LANES:(j + 1) * LANES] * xs)

    fulls = [lr, li, bcat, ccat, d_skip.reshape(1, -1)]
    return _rowwise(body, [x, s_re, s_im], fulls, [D_MODEL, S5_H, S5_H], x.shape[0], name)


def _s5_sample(x, s_re0, s_im0, a_re, a_im, log_dt, b_re, b_im, c_re, c_im, d_skip, w_o, w_gate):
    m = x.shape[0]
    lr, li, bcat, ccat = _s5_weights(a_re, a_im, log_dt, b_re, b_im, c_re, c_im)
    z, h_re, h_im = _s5_step(x, s_re0.reshape(m, S5_H), s_im0.reshape(m, S5_H), lr, li, bcat, ccat, d_skip, "s5_step_s")
    return _s5_mix(z, w_o, w_gate), h_re.reshape(m, S5_GROUPS, S5_STATE), h_im.reshape(m, S5_GROUPS, S5_STATE)


def _gla_step(proj, logd, s0, norm_g, name):
    bsz = proj.shape[0]
    scale = GLA_DK ** -0.5

    def body(q_ref, k_ref, v_ref, gate_ref, ld_ref, ng_ref, s_ref, o_ref, so_ref):
        eye = _eye(GLA_DK)
        ng = ng_ref[...]
        for h in range(GLA_HEADS):
            ks = slice(h * GLA_DK, (h + 1) * GLA_DK)
            vs = slice(h * GLA_DV, (h + 1) * GLA_DV)
            bc = ld_ref[0, :, ks]
            k = k_ref[0, :, ks]
            v = v_ref[0, :, vs]
            q_in = q_ref[0, :, ks] * scale * jnp.exp(bc)
            a = jnp.sum(q_in * (k * jnp.exp(-bc)), axis=1, keepdims=True)
            s = s_ref[0, h]
            o = jnp.sum(_to_col(q_in, eye) * s, axis=0, keepdims=True) + a * v
            so_ref[0, h] = s * _to_col(jnp.exp(bc), eye) + _to_col(k, eye) * v
            o = o * lax.rsqrt(jnp.mean(o * o, -1, keepdims=True) + 1e-6) * ng
            o_ref[0, :, vs] = o * _silu(gate_ref[0, :, vs])

    st_spec = pl.BlockSpec((1, GLA_HEADS, GLA_DK, GLA_DV), lambda b: (b, 0, 0, 0))
    o, s = pl.pallas_call(
        body, grid=(bsz,),
        in_specs=[_rows_spec(GLA_QK, 0), _rows_spec(GLA_QK, 1), _rows_spec(GLA_V, 1), _rows_spec(GLA_V, 2),
                  _rows_spec(GLA_QK), pl.BlockSpec((1, GLA_DV), lambda b: (0, 0)), st_spec],
        out_specs=[_rows_spec(GLA_V), st_spec],
        out_shape=[jax.ShapeDtypeStruct((bsz, 1, GLA_V), F32), jax.ShapeDtypeStruct(s0.shape, F32)],
        compiler_params=_cp(("parallel",)), name=name,
    )(_row3(proj), _row3(proj), _row3(proj), _row3(proj), _row3(logd), norm_g.reshape(1, -1), s0)
    return o.reshape(bsz, GLA_V), s


def _gla_sample(x, s0, w_in, w_gk2, b_gk, norm_g, w_out):
    proj, logd = _gla_proj(x, w_in, w_gk2, b_gk, x.shape[0], "gla_proj_s")
    o, s = _gla_step(proj, logd, s0, norm_g, "gla_step_s")
    return _proj_mix(o, w_out), s


def _dot3_r(x, l):
    hi = x.astype(BF16).astype(F32)
    r1 = x - hi
    mid = r1.astype(BF16).astype(F32)
    lo = r1 - mid
    return _dot(hi, l) + _dot(mid, l) + _dot(lo, l)


def _head_ones():
    return jnp.kron(jnp.eye(RW_H, dtype=F32), jnp.ones((RW_HD, RW_HD), F32))


def _rwkv_step_prep(r, k, v, lw, a, k_k, k_a, name):
    bsz = r.shape[0]

    def body(r_ref, k_ref, v_ref, lw_ref, a_ref, kk_ref, ka_ref, ones_ref, rt, kt, vt, kkt, bt, dt, kh_ref):
        k, a = k_ref[...], a_ref[...]
        kk = k * kk_ref[...]
        kkn = kk * lax.rsqrt(_dot3_r(kk * kk, ones_ref[...]) + 1e-6)
        kh = k * (1.0 + (a - 1.0) * ka_ref[...])
        kh_ref[...] = kh
        rt[...] = r_ref[...].T
        kt[...] = kh.T
        vt[...] = v_ref[...].T
        kkt[...] = kkn.T
        bt[...] = (kkn * a).T
        dt[...] = jnp.exp(lw_ref[...]).T

    args = (r, k, v, lw, a, k_k.reshape(1, -1), k_a.reshape(1, -1), _head_ones())
    tshape = jax.ShapeDtypeStruct((D_MODEL, bsz), F32)
    return pl.pallas_call(
        body, grid=(1,), in_specs=[_full_spec(t) for t in args],
        out_specs=[pl.BlockSpec((D_MODEL, bsz), lambda i: (0, 0))] * 6 + [pl.BlockSpec((bsz, D_MODEL), lambda i: (0, 0))],
        out_shape=[tshape] * 6 + [jax.ShapeDtypeStruct((bsz, D_MODEL), F32)],
        compiler_params=_cp(("arbitrary",)), name=name)(*args)


def _rwkv_step_lanes(rt, kt, vt, kkt, bt, dt, s_t, name):
    hd = RW_HD
    bsz = s_t.shape[-1]

    def body(rt_ref, kt_ref, vt_ref, kkt_ref, bt_ref, dt_ref, s_ref, y_ref, so_ref):
        hs = pl.ds(pl.multiple_of(pl.program_id(0) * hd, hd), hd)
        r_h, k_h, v_h, kk_h, b_h, d_h = (ref[hs, :] for ref in (rt_ref, kt_ref, vt_ref, kkt_ref, bt_ref, dt_ref))
        ys = []
        for vi in range(hd):
            s = s_ref[0, vi]
            sa = -jnp.sum(s * kk_h, axis=0, keepdims=True)
            s_new = s * d_h + sa * b_h + v_h[vi:vi + 1] * k_h
            so_ref[0, vi] = s_new
            ys.append(jnp.sum(s_new * r_h, axis=0, keepdims=True))
        y_ref[...] = jnp.concatenate(ys, axis=0)

    vec = pl.BlockSpec((D_MODEL, bsz), lambda h: (0, 0))
    st = pl.BlockSpec((1, hd, hd, bsz), lambda h: (h, 0, 0, 0))
    return pl.pallas_call(
        body, grid=(RW_H,), in_specs=[vec] * 6 + [st],
        out_specs=[pl.BlockSpec((hd, bsz), lambda h: (h, 0)), st],
        out_shape=[jax.ShapeDtypeStruct((D_MODEL, bsz), F32), jax.ShapeDtypeStruct(s_t.shape, F32)],
        compiler_params=_cp(("parallel",)), name=name)(rt, kt, vt, kkt, bt, dt, s_t)


def _rwkv_step_out(yt, r, kh, v, g, r_k, gn_g, gn_b, name):
    def body(yt_ref, r_ref, k_ref, v_ref, g_ref, rk_ref, gg_ref, gb_ref, ones_ref, o_ref):
        ones = ones_ref[...]
        y = yt_ref[...].T
        yc = y - _dot3_r(y, ones) * (1.0 / RW_HD)
        yn = yc * lax.rsqrt(_dot3_r(yc * yc, ones) * (1.0 / RW_HD) + RW_GN_EPS) * gg_ref[...] + gb_ref[...]
        bonus = _dot3_r(r_ref[...] * k_ref[...] * rk_ref[...], ones) * v_ref[...]
        o_ref[...] = (yn + bonus) * g_ref[...]

    row = lambda w: w.reshape(1, D_MODEL)
    args = (yt, r, kh, v, g, row(r_k), row(gn_g), row(gn_b), _head_ones())
    return pl.pallas_call(
        body, grid=(1,), in_specs=[_full_spec(t) for t in args],
        out_specs=pl.BlockSpec(r.shape, lambda i: (0, 0)),
        out_shape=jax.ShapeDtypeStruct(r.shape, F32),
        compiler_params=_cp(("arbitrary",)), name=name)(*args)


def _rwkv_sample(x, shift0, s0, mu, w_rkv, w0, w_w1, w_w2, a0, w_a1, w_a2, w_g1, w_g2, k_k, k_a, r_k, gn_g, gn_b, w_o):
    m = x.shape[0]
    r, k, v, lw, a, g = _rwkv_proj(x, shift0, 1, m, mu, w_rkv, w0, w_w1, w_w2, a0, w_a1, w_a2, w_g1, w_g2, m,
                                   "rwkv_proj_s")
    rt, kt, vt, kkt, bt, dt, kh = _rwkv_step_prep(r, k, v, lw, a, k_k, k_a, "rwkv_prep_s")
    yt, s_t = _rwkv_step_lanes(rt, kt, vt, kkt, bt, dt, jnp.transpose(s0, (1, 2, 3, 0)), "rwkv_step_s")
    o = _rwkv_step_out(yt, r, kh, v, g, r_k, gn_g, gn_b, "rwkv_out_s")
    return _proj_mix(o, w_o), x, jnp.transpose(s_t, (3, 0, 1, 2))


def kernel(x_prompt, x_sample, state_gdn_conv, state_gdn, state_s5_re, state_s5_im, state_gla, state_rwkv_shift,
           state_rwkv, p_prompt, p_sample, gdn_w_in, gdn_conv_w, gdn_a_log, gdn_dt_bias, gdn_norm_g, gdn_w_out,
           s5_a_re, s5_a_im, s5_log_dt, s5_b_re, s5_b_im, s5_c_re, s5_c_im, s5_d, s5_w_o, s5_w_gate,
           gla_w_in, gla_w_gk2, gla_b_gk, gla_norm_g, gla_w_out,
           rwkv_mu, rwkv_w_rkv, rwkv_w0, rwkv_w_w1, rwkv_w_w2, rwkv_a0, rwkv_w_a1, rwkv_w_a2, rwkv_w_g1, rwkv_w_g2,
           rwkv_k_k, rwkv_k_a, rwkv_r_k, rwkv_ln_g, rwkv_ln_b, rwkv_w_o,
           ln_mix_g, ln_mix_b, ln_ffn_g, ln_ffn_b, mlp_w1, mlp_w2, ple_w, ple_gate_w):
    bsz, t, d = x_prompt.shape
    bs = x_sample.shape[0]
    gdn_w = (gdn_w_in[0], gdn_conv_w[0], gdn_a_log[0], gdn_dt_bias[0], gdn_norm_g[0], gdn_w_out[0])
    s5_w = (s5_a_re[0], s5_a_im[0], s5_log_dt[0], s5_b_re[0], s5_b_im[0], s5_c_re[0], s5_c_im[0], s5_d[0],
            s5_w_o[0], s5_w_gate[0])
    gla_w = (gla_w_in[0], gla_w_gk2[0], gla_b_gk[0], gla_norm_g[0], gla_w_out[0])
    rwkv_w = (rwkv_mu[0], rwkv_w_rkv[0], rwkv_w0[0], rwkv_w_w1[0], rwkv_w_w2[0], rwkv_a0[0], rwkv_w_a1[0],
              rwkv_w_a2[0], rwkv_w_g1[0], rwkv_w_g2[0], rwkv_k_k[0], rwkv_k_a[0], rwkv_r_k[0], rwkv_ln_g[0],
              rwkv_ln_b[0], rwkv_w_o[0])

    def tail(x, mix, p_all, i, tm, tag):
        return _block_tail(x, mix, ln_mix_g[i], ln_mix_b[i], p_all, i, mlp_w1[i], mlp_w2[i], ln_ffn_g[i], ln_ffn_b[i],
                           ple_w[i], ple_gate_w[i], tm, 1024, f"tail{i}_{tag}")

    xp = x_prompt.reshape(bsz * t, d)
    pp = p_prompt.reshape(DEPTH, bsz * t, D_PLE)
    tm_p = min(bsz * t, 512)
    mix, gc_p, gs_p = _gdn_prompt(xp, bsz, t, *gdn_w)
    xp = tail(xp, mix, pp, 0, tm_p, "p")
    mix, sr_p, si_p = _s5_prompt(xp, bsz, t, *s5_w)
    xp = tail(xp, mix, pp, 1, tm_p, "p")
    mix, la_p = _gla_prompt(xp, bsz, t, *gla_w)
    xp = tail(xp, mix, pp, 2, tm_p, "p")
    mix, sh_p, rs_p = _rwkv_prompt(xp, bsz, t, *rwkv_w)
    xp = tail(xp, mix, pp, 3, tm_p, "p")

    xs = x_sample.reshape(bs, d)
    ps = p_sample.reshape(DEPTH, bs, D_PLE)
    mix, gc_s, gs_s = _gdn_sample(xs, state_gdn_conv[0], state_gdn[0], *gdn_w)
    xs = tail(xs, mix, ps, 0, bs, "s")
    mix, sr_s, si_s = _s5_sample(xs, state_s5_re[0], state_s5_im[0], *s5_w)
    xs = tail(xs, mix, ps, 1, bs, "s")
    mix, la_s = _gla_sample(xs, state_gla[0], *gla_w)
    xs = tail(xs, mix, ps, 2, bs, "s")
    mix, sh_s, rs_s = _rwkv_sample(xs, state_rwkv_shift[0], state_rwkv[0], *rwkv_w)
    xs = tail(xs, mix, ps, 3, bs, "s")

    e = lambda a: a[None]
    return (xp.reshape(bsz, t, d), xs.reshape(bs, 1, d), e(gc_p), e(gc_s), e(gs_p), e(gs_s), e(sr_p), e(sr_s),
            e(si_p), e(si_s), e(la_p), e(la_s), e(sh_p), e(sh_s), e(rs_p), e(rs_s))
```

```python
import math
from typing import Callable, NamedTuple

import jax
import jax.numpy as jnp
from jax import lax
from jax.experimental import pallas as pl
from jax.experimental.pallas import tpu as pltpu

F32 = jnp.float32
BF16 = jnp.bfloat16

D_MODEL = 1024
DEPTH = 4
D_PLE = 256
D_FF = 4 * D_MODEL
LN_EPS = 1e-5
ALPHA = (2.0 * DEPTH) ** 0.25

GDN_HD = 128
GDN_KH = 8
GDN_VH = 16
GDN_QK = GDN_KH * GDN_HD
GDN_V = GDN_VH * GDN_HD
GDN_CONV_DIM = 2 * GDN_QK + GDN_V
GDN_CONV_W = 4
GDN_CHUNK = 64

S5_GROUP = 16
S5_GROUPS = D_MODEL // S5_GROUP
S5_STATE = 64
S5_H = S5_GROUPS * S5_STATE

GLA_HEADS = 4
GLA_DK = 128
GLA_DV = 256
GLA_QK = GLA_HEADS * GLA_DK
GLA_V = GLA_HEADS * GLA_DV
GLA_RANK = 16
GLA_GATE_NORM = 16.0
GLA_CHUNK = 16

RW_HD = 64
RW_H = D_MODEL // RW_HD
RW_GN_EPS = 64e-5
RW_CHUNK = 64

LANES = 128
SUBLANES = 8
VMEM_LIMIT = 56 * 1024 * 1024


def _cp(sem, vmem=VMEM_LIMIT):
    return pltpu.CompilerParams(dimension_semantics=sem, vmem_limit_bytes=vmem)


def _dot(a, b):
    return jnp.dot(a, b, preferred_element_type=F32)


def _dot_nt(a, b):
    return lax.dot_general(a, b, (((1,), (1,)), ((), ())), preferred_element_type=F32)


def _dot_tn(a, b):
    return lax.dot_general(a, b, (((0,), (0,)), ((), ())), preferred_element_type=F32)


def _dot3(l, x):
    hi = x.astype(BF16).astype(F32)
    r1 = x - hi
    mid = r1.astype(BF16).astype(F32)
    lo = r1 - mid
    return _dot(l, hi) + _dot(l, mid) + _dot(l, lo)


def _softplus(x):
    return jnp.maximum(x, 0.0) + jnp.log(1.0 + jnp.exp(-jnp.abs(x)))


def _sigmoid(x):
    return 1.0 / (1.0 + jnp.exp(-x))


def _silu(x):
    return x * _sigmoid(x)


def _ln(x, g, b):
    xc = x - jnp.mean(x, -1, keepdims=True)
    var = jnp.mean(xc * xc, -1, keepdims=True)
    return xc * lax.rsqrt(var + LN_EPS) * g + b


def _iota2(shape, axis):
    return lax.broadcasted_iota(jnp.int32, shape, axis)


def _tri(n, strict=False, block=None):
    r = _iota2((n, n), 0)
    c = _iota2((n, n), 1)
    m = (r > c) if strict else (r >= c)
    if block is not None:
        m = m & ((r // block) == (c // block))
    return m


def _full_spec(a):
    n = a.ndim
    return pl.BlockSpec(a.shape, lambda *_: (0,) * n)


def _mm(x, w, n_cols, tm, tn, name, side_fn, side_fulls, side_cols):
    m, k = x.shape

    def body(x_ref, w_ref, *rest):
        side_refs, o_ref, s_ref, xb_ref = rest[:-3], rest[-3], rest[-2], rest[-1]

        @pl.when(pl.program_id(1) == 0)
        def _():
            xb_ref[...] = x_ref[...].astype(BF16)
            s_ref[...] = side_fn(xb_ref[...], *side_refs)

        o_ref[...] = _dot(xb_ref[...], w_ref[...])

    return pl.pallas_call(
        body, grid=(m // tm, n_cols // tn),
        in_specs=[pl.BlockSpec((tm, k), lambda i, j: (i, 0)), pl.BlockSpec((k, tn), lambda i, j: (0, j))]
        + [_full_spec(a) for a in side_fulls],
        out_specs=[pl.BlockSpec((tm, tn), lambda i, j: (i, j)), pl.BlockSpec((tm, side_cols), lambda i, j: (i, 0))],
        out_shape=[jax.ShapeDtypeStruct((m, n_cols), F32), jax.ShapeDtypeStruct((m, side_cols), F32)],
        scratch_shapes=[pltpu.VMEM((tm, k), BF16)],
        compiler_params=_cp(("parallel", "arbitrary")), name=name)(x, w.astype(BF16), *side_fulls)


def _rowwise(body, rows, fulls, out_cols, tm, name):
    m = rows[0].shape[0]
    in_specs = [pl.BlockSpec((tm, r.shape[1]), lambda i: (i, 0)) for r in rows] + [_full_spec(f) for f in fulls]
    out_specs = [pl.BlockSpec((tm, c), lambda i: (i, 0)) for c in out_cols]
    out_shape = [jax.ShapeDtypeStruct((m, c), F32) for c in out_cols]
    res = pl.pallas_call(body, grid=(m // tm,), in_specs=in_specs, out_specs=out_specs, out_shape=out_shape,
                         compiler_params=_cp(("parallel",)), name=name)(*rows, *fulls)
    return res


def _resident_spec(a):
    n = a.ndim
    return pl.BlockSpec(a.shape, lambda *_: (0,) * n, pipeline_mode=pl.Buffered(1))


class Mix(NamedTuple):
    rows: tuple
    fulls: tuple
    fn: Callable


def _proj_mix(o, w_out):
    return Mix((o,), (w_out.astype(BF16),), lambda o_t, w_ref: _dot(o_t.astype(BF16), w_ref[...]))


def _block_tail(x, mix, lm_g, lm_b, p_all, layer, w1, w2, g, b, ple_w, gate_w, tm, tf, name):
    m = x.shape[0]
    nf = D_FF // tf
    nrow, nfull = len(mix.rows), len(mix.fulls)

    def body(*refs):
        x_ref, p_ref = refs[0], refs[1 + nrow]
        row_refs = refs[1:1 + nrow]
        full_refs = refs[2 + nrow:2 + nrow + nfull]
        lmg_ref, lmb_ref, w1_ref, w2_ref, g_ref, b_ref, pw_ref, gw_ref, o_ref, h_ref = refs[2 + nrow + nfull:]
        x1 = _ln(ALPHA * x_ref[...] + mix.fn(*[r[...] for r in row_refs], *full_refs), lmg_ref[...], lmb_ref[...])
        xb = x1.astype(BF16)
        for f in range(nf):
            h = jnp.maximum(_dot(xb, w1_ref[:, f * tf:(f + 1) * tf]), 0.0)
            h_ref[:, f * tf:(f + 1) * tf] = (h * h).astype(BF16)
        x2 = _ln(ALPHA * x1 + _dot(h_ref[...], w2_ref[...]), g_ref[...], b_ref[...])
        gate = _sigmoid(_dot(x2.astype(BF16), gw_ref[...]))
        o_ref[...] = x2 + _dot(p_ref[...].astype(BF16), pw_ref[...]) * gate

    row = lambda v: v.reshape(1, -1)
    fulls = mix.fulls + (row(lm_g), row(lm_b), w1.astype(BF16), w2.astype(BF16), row(g), row(b), ple_w.astype(BF16),
                         gate_w.astype(BF16))
    return pl.pallas_call(
        body, grid=(m // tm,),
        in_specs=[pl.BlockSpec((tm, D_MODEL), lambda i: (i, 0))]
        + [pl.BlockSpec((tm, r.shape[1]), lambda i: (i, 0)) for r in mix.rows]
        + [pl.BlockSpec((None, tm, D_PLE), lambda i: (layer, i, 0))]
        + [_resident_spec(a) for a in fulls],
        out_specs=pl.BlockSpec((tm, D_MODEL), lambda i: (i, 0)),
        out_shape=jax.ShapeDtypeStruct((m, D_MODEL), F32),
        scratch_shapes=[pltpu.VMEM((tm, D_FF), BF16)],
        compiler_params=_cp(("parallel",)), name=name,
    )(x, *mix.rows, p_all, *fulls)


def _gdn_proj(x, w_in, a_log, dt_bias, tm, name):
    w_ba = jnp.pad(w_in[:, GDN_CONV_DIM + GDN_V:], ((0, 0), (0, LANES - 2 * GDN_VH))).astype(BF16)
    pad = lambda v: jnp.pad(v.reshape(1, -1), ((0, 0), (GDN_VH, LANES - 2 * GDN_VH)))

    def gates(xb, w_ref, al_ref, dt_ref):
        y = _dot(xb, w_ref[...])
        lane = _iota2(y.shape, 1)
        g = -jnp.exp(al_ref[...]) * _softplus(y + dt_ref[...])
        return jnp.where(lane < GDN_VH, _sigmoid(y), g)

    return _mm(x, w_in, GDN_CONV_DIM + GDN_V, tm, 512, name, gates, (w_ba, pad(a_log), pad(dt_bias)), LANES)


def _l2n(x, scale):
    return x * (lax.rsqrt(jnp.sum(x * x, -1, keepdims=True) + 1e-6) * scale)


def _gdn_act(y, o_ref):
    y = _silu(y)
    for h in range(GDN_KH):
        sl = slice(h * GDN_HD, (h + 1) * GDN_HD)
        o_ref[:, sl] = _l2n(y[:, sl], GDN_HD ** -0.5)
        sl = slice(GDN_QK + h * GDN_HD, GDN_QK + (h + 1) * GDN_HD)
        o_ref[:, sl] = _l2n(y[:, sl], 1.0)
    o_ref[:, 2 * GDN_QK:] = y[:, 2 * GDN_QK:]


def _gdn_conv_prompt(proj, conv_w, bsz, t, tt, name):
    nt = t // tt
    c = GDN_CONV_DIM

    def body(x_ref, w_ref, o_ref, tail_ref, buf_ref):
        i = pl.program_id(1)

        @pl.when(i == 0)
        def _():
            buf_ref[pl.ds(0, SUBLANES)] = jnp.zeros((SUBLANES, c), F32)

        x = x_ref[...]
        w = w_ref[...]
        buf_ref[pl.ds(SUBLANES, tt)] = x
        y = x * w[3:4]
        for j in range(1, GDN_CONV_W):
            y = y + buf_ref[pl.ds(SUBLANES - j, tt)] * w[3 - j:4 - j]
        _gdn_act(y, o_ref)
        buf_ref[pl.ds(0, SUBLANES)] = x[tt - SUBLANES:]
        tail_ref[0] = x[tt - SUBLANES:]

    return pl.pallas_call(
        body, grid=(bsz, nt),
        in_specs=[pl.BlockSpec((tt, c), lambda b, i: (b * nt + i, 0)), _full_spec(conv_w)],
        out_specs=[pl.BlockSpec((tt, c), lambda b, i: (b * nt + i, 0)),
                   pl.BlockSpec((1, SUBLANES, c), lambda b, i: (b, 0, 0))],
        out_shape=[jax.ShapeDtypeStruct((bsz * t, c), F32), jax.ShapeDtypeStruct((bsz, SUBLANES, c), F32)],
        scratch_shapes=[pltpu.VMEM((SUBLANES + tt, c), F32)],
        compiler_params=_cp(("parallel", "arbitrary")), name=name)(proj, conv_w)


def _neumann_inv(ms, n):
    eye = (_iota2((n, n), 0) == _iota2((n, n), 1)).astype(F32)
    ts = [eye - m for m in ms]
    ps = [_dot(m, m) for m in ms]
    k = 2
    while True:
        ts = [t + _dot(t, p) for t, p in zip(ts, ps)]
        k *= 2
        if k >= n:
            return ts
        ps = [_dot(p, p) for p in ps]


def _gdn_chunks(qkv, bg, proj, norm_g, bsz, t, name):
    c = GDN_CHUNK
    nc = t // c
    z_blk = GDN_CONV_DIM // GDN_V

    def body(q_ref, k_ref, v_ref, bg_ref, z_ref, ng_ref, o_ref, s_out_ref, s_ref):
        i = pl.program_id(1)

        @pl.when(i == 0)
        def _():
            s_ref[...] = jnp.zeros_like(s_ref)

        bgv = bg_ref[...]
        ltri = _tri(c).astype(F32)
        gc = _dot3(ltri, bgv)
        gct = jnp.concatenate([gc, jnp.zeros_like(gc)], axis=0).T
        incl = _tri(c)
        strict = _tri(c, strict=True)
        ng = ng_ref[...]
        rep = GDN_VH // GDN_KH
        heads = range(GDN_VH)
        hsl = lambda n: slice(n * GDN_HD, (n + 1) * GDN_HD)
        gram = [_dot_nt(jnp.concatenate([k_ref[:, hsl(n)], q_ref[:, hsl(n)]], axis=0), k_ref[:, hsl(n)])
                for n in range(GDN_KH)]
        ms, aqks, rhss, q_ins, k_outs, g_ends = [], [], [], [], [], []
        for h in heads:
            kh = h // rep
            kk = k_ref[:, hsl(kh)]
            beta = bgv[:, h:h + 1]
            gcol = gc[:, GDN_VH + h:GDN_VH + h + 1]
            grow = gct[GDN_VH + h:GDN_VH + h + 1, :c]
            decay = jnp.where(incl, jnp.exp(jnp.where(incl, gcol - grow, 0.0)), 0.0)
            ms.append(jnp.where(strict, gram[kh][:c] * beta * decay, 0.0))
            aqks.append(gram[kh][c:] * decay)
            egc = jnp.exp(gcol)
            rhss.append(jnp.concatenate([v_ref[:, hsl(h)] * beta, kk * (beta * egc)], axis=1))
            q_ins.append(q_ref[:, hsl(kh)] * egc)
            glast = gc[c - 1:c, GDN_VH + h:GDN_VH + h + 1]
            k_outs.append(kk * jnp.exp(glast - gcol))
            g_ends.append(jnp.exp(glast))
        tinvs = _neumann_inv(ms, c)
        sols = [_dot(tinvs[h], rhss[h]) for h in heads]
        wqs = [_dot(jnp.concatenate([sols[h][:, GDN_HD:], q_ins[h]], axis=0), s_ref[h]) for h in heads]
        v_news = [sols[h][:, :GDN_HD] - wqs[h][:c] for h in heads]
        avs = [_dot(aqks[h], v_news[h]) for h in heads]
        kvs = [_dot_tn(k_outs[h], v_news[h]) for h in heads]
        for h in heads:
            s_ref[h] = s_ref[h] * g_ends[h] + kvs[h]
            o = wqs[h][c:] + avs[h]
            o = o * lax.rsqrt(jnp.mean(o * o, -1, keepdims=True) + 1e-6) * ng
            o_ref[:, hsl(h)] = o * _silu(z_ref[:, hsl(h)])

        @pl.when(i == nc - 1)
        def _():
            s_out_ref[0] = s_ref[...]

    return pl.pallas_call(
        body, grid=(bsz, nc),
        in_specs=[pl.BlockSpec((c, GDN_QK), lambda b, i: (b * nc + i, 0)),
                  pl.BlockSpec((c, GDN_QK), lambda b, i: (b * nc + i, 1)),
                  pl.BlockSpec((c, GDN_V), lambda b, i: (b * nc + i, 1)),
                  pl.BlockSpec((c, LANES), lambda b, i: (b * nc + i, 0)),
                  pl.BlockSpec((c, GDN_V), lambda b, i: (b * nc + i, z_blk)),
                  pl.BlockSpec((1, GDN_HD), lambda b, i: (0, 0))],
        out_specs=[pl.BlockSpec((c, GDN_V), lambda b, i: (b * nc + i, 0)),
                   pl.BlockSpec((1, GDN_VH, GDN_HD, GDN_HD), lambda b, i: (b, 0, 0, 0))],
        out_shape=[jax.ShapeDtypeStruct((bsz * t, GDN_V), F32),
                   jax.ShapeDtypeStruct((bsz, GDN_VH, GDN_HD, GDN_HD), F32)],
        scratch_shapes=[pltpu.VMEM((GDN_VH, GDN_HD, GDN_HD), F32)],
        compiler_params=_cp(("parallel", "arbitrary")), name=name,
    )(qkv, qkv, qkv, bg, proj, norm_g.reshape(1, -1))


def _gdn_prompt(x, bsz, t, w_in, conv_w, a_log, dt_bias, norm_g, w_out):
    proj, bg = _gdn_proj(x, w_in, a_log, dt_bias, min(bsz * t, 2048), "gdn_proj")
    qkv, tail = _gdn_conv_prompt(proj, conv_w, bsz, t, min(t, 256), "gdn_conv")
    o, s = _gdn_chunks(qkv, bg, proj, norm_g, bsz, t, "gdn_chunks")
    return _proj_mix(o, w_out), tail[:, SUBLANES - (GDN_CONV_W - 1):], s


def _s5_discretize(a_re, a_im, log_dt, b_re, b_im):
    g, p = a_re.shape
    bt_re = jnp.swapaxes(b_re, 1, 2)
    bt_im = jnp.swapaxes(b_im, 1, 2)

    def body(ar_ref, ai_ref, ldt_ref, br_ref, bi_ref, lr_ref, li_ref, bbr_ref, bbi_ref):
        ar, ai = ar_ref[...], ai_ref[...]
        dt = jnp.exp(ldt_ref[...])
        mag = jnp.exp(ar * dt)
        lr, li = mag * jnp.cos(ai * dt), mag * jnp.sin(ai * dt)
        den = ar * ar + ai * ai
        f_re = ((lr - 1.0) * ar + li * ai) / den
        f_im = (li * ar - (lr - 1.0) * ai) / den
        lr_ref[...] = lr
        li_ref[...] = li
        br, bi = br_ref[...], bi_ref[...]
        fr, fi = f_re[:, None, :], f_im[:, None, :]
        bbr_ref[...] = fr * br - fi * bi
        bbi_ref[...] = fr * bi + fi * br

    args = (a_re, a_im, log_dt.reshape(g, 1), bt_re, bt_im)
    return pl.pallas_call(
        body, grid=(1,), in_specs=[_full_spec(a) for a in args],
        out_specs=[pl.BlockSpec((g, p), lambda i: (0, 0))] * 2 + [pl.BlockSpec(bt_re.shape, lambda i: (0, 0, 0))] * 2,
        out_shape=[jax.ShapeDtypeStruct((g, p), F32)] * 2 + [jax.ShapeDtypeStruct(bt_re.shape, F32)] * 2,
        name="s5_discretize")(*args)


def _blockdiag(a, per):
    g, r, c = a.shape
    a4 = a.reshape(g // per, per, r, c)
    eye = jnp.eye(per, dtype=a.dtype)
    return jnp.einsum("jgrc,gh->jgrhc", a4, eye).reshape(g // per, per * r, per * c)


S5_GPT = LANES // S5_GROUP
S5_NT = S5_GROUPS // S5_GPT
S5_HT = S5_GPT * S5_STATE


def _s5_weights(a_re, a_im, log_dt, b_re, b_im, c_re, c_im):
    lr, li, bbr, bbi = _s5_discretize(a_re, a_im, log_dt, b_re, b_im)
    bcat = jnp.concatenate([_blockdiag(bbr, S5_GPT), _blockdiag(bbi, S5_GPT)], axis=2)
    ccat = jnp.concatenate([_blockdiag(jnp.swapaxes(c_re, 1, 2), S5_GPT),
                            -_blockdiag(jnp.swapaxes(c_im, 1, 2), S5_GPT)], axis=1)
    return lr.reshape(1, S5_H), li.reshape(1, S5_H), bcat, ccat


def _gelu(y):
    return 0.5 * y * (1.0 + jnp.tanh(math.sqrt(2.0 / math.pi) * (y + 0.044715 * (y * y * y))))


def _s5_scan(x, bsz, t, s_re0, s_im0, lr, li, bcat, ccat, d_skip, tc, name):
    d = x.shape[1]
    nt = t // tc
    rows = bsz * tc
    lq = 1024

    def body(x_ref, sr0_ref, si0_ref, lr_ref, li_ref, b_ref, c_ref, d_ref, z_ref, sr_ref, si_ref, hre, him, xs, zs):
        i = pl.program_id(0)

        @pl.when(i == 0)
        def _():
            sr_ref[...] = sr0_ref[...]
            si_ref[...] = si0_ref[...]

        xs[...] = jnp.swapaxes(x_ref[...], 0, 1).reshape(rows, d)

        def x_tile(j):
            return xs[:, j * LANES:(j + 1) * LANES]

        for j in range(S5_NT):
            bu = _dot(x_tile(j), b_ref[j])
            hre[:, j * S5_HT:(j + 1) * S5_HT] = bu[:, :S5_HT]
            him[:, j * S5_HT:(j + 1) * S5_HT] = bu[:, S5_HT:]

        for q in range(S5_H // lq):
            ls = slice(q * lq, (q + 1) * lq)
            lam_r = jnp.broadcast_to(lr_ref[:, ls], (bsz, lq))
            lam_i = jnp.broadcast_to(li_ref[:, ls], (bsz, lq))

            def step(tt, carry):
                sr, si = carry
                idx = pl.ds(pl.multiple_of(tt * bsz, bsz), bsz)
                nr = lam_r * sr - lam_i * si + hre[idx, ls]
                ni = lam_r * si + lam_i * sr + him[idx, ls]
                hre[idx, ls] = nr
                him[idx, ls] = ni
                return nr, ni

            sr, si = lax.fori_loop(0, tc, step, (sr_ref[:, ls], si_ref[:, ls]), unroll=8)
            sr_ref[:, ls] = sr
            si_ref[:, ls] = si

        dsk = d_ref[...]
        for j in range(S5_NT):
            hs = slice(j * S5_HT, (j + 1) * S5_HT)
            cj = c_ref[j]
            y = _dot(hre[:, hs], cj[:S5_HT]) + _dot(him[:, hs], cj[S5_HT:])
            zs[:, j * LANES:(j + 1) * LANES] = _gelu(y + dsk[:, j * LANES:(j + 1) * LANES] * x_tile(j))
        z_ref[...] = jnp.swapaxes(zs[...].reshape(tc, bsz, d), 0, 1)

    fulls = (s_re0, s_im0, lr, li, bcat, ccat, d_skip.reshape(1, d))
    z, s_re, s_im = pl.pallas_call(
        body, grid=(nt,),
        in_specs=[pl.BlockSpec((bsz, tc, d), lambda i: (0, i, 0))] + [_full_spec(a) for a in fulls],
        out_specs=[pl.BlockSpec((bsz, tc, d), lambda i: (0, i, 0)),
                   pl.BlockSpec((bsz, S5_H), lambda i: (0, 0)), pl.BlockSpec((bsz, S5_H), lambda i: (0, 0))],
        out_shape=[jax.ShapeDtypeStruct((bsz, t, d), F32),
                   jax.ShapeDtypeStruct((bsz, S5_H), F32), jax.ShapeDtypeStruct((bsz, S5_H), F32)],
        scratch_shapes=[pltpu.VMEM((rows, S5_H), F32), pltpu.VMEM((rows, S5_H), F32),
                        pltpu.VMEM((rows, d), F32), pltpu.VMEM((rows, d), F32)],
        compiler_params=_cp(("arbitrary",)), name=name)(x.reshape(bsz, t, d), *fulls)
    return z.reshape(bsz * t, d), s_re, s_im


def _s5_mix(z, w_o, w_gate):
    def fn(z_t, wo_ref, wg_ref):
        zb = z_t.astype(BF16)
        return _dot(zb, wo_ref[...]) * _sigmoid(_dot(zb, wg_ref[...]))

    return Mix((z,), (w_o.astype(BF16), w_gate.astype(BF16)), fn)


def _s5_prompt(x, bsz, t, a_re, a_im, log_dt, b_re, b_im, c_re, c_im, d_skip, w_o, w_gate):
    lr, li, bcat, ccat = _s5_weights(a_re, a_im, log_dt, b_re, b_im, c_re, c_im)
    zero = jnp.zeros((bsz, S5_H), F32)
    z, s_re, s_im = _s5_scan(x, bsz, t, zero, zero, lr, li, bcat, ccat, d_skip, min(t, 64), "s5_scan")
    return _s5_mix(z, w_o, w_gate), s_re.reshape(bsz, S5_GROUPS, S5_STATE), s_im.reshape(bsz, S5_GROUPS, S5_STATE)


GLA_PROJ = 2 * GLA_QK + 2 * GLA_V


def _gla_proj(x, w_in, w_gk2, b_gk, tm, name):
    w1 = jnp.pad(w_in[:, GLA_PROJ:], ((0, 0), (0, LANES - GLA_RANK))).astype(BF16)
    w2 = jnp.pad(w_gk2, ((0, LANES - GLA_RANK), (0, 0)))

    def logd(xb, w1_ref, w2_ref, b_ref):
        y = _dot(_dot(xb, w1_ref[...]), w2_ref[...]) + b_ref[...]
        return -_softplus(-y) * (1.0 / GLA_GATE_NORM)

    return _mm(x, w_in, GLA_PROJ, tm, 512, name, logd, (w1, w2, b_gk.reshape(1, -1)), GLA_QK)


def _gla_chunks(proj, logd, norm_g, bsz, t, name):
    c = GLA_CHUNK
    rows = 64
    nr = t // rows
    scale = GLA_DK ** -0.5

    def body(q_ref, k_ref, v_ref, gate_ref, ld_ref, ng_ref, o_ref, s_out_ref, st_ref):
        i = pl.program_id(1)

        @pl.when(i == 0)
        def _():
            st_ref[...] = jnp.zeros_like(st_ref)

        lblk = _tri(rows, block=c).astype(F32)
        bc_all = _dot3(lblk, ld_ref[...])
        incl = _tri(c)
        ng = ng_ref[...]
        subs = range(rows // c)
        heads = range(GLA_HEADS)
        rsl = lambda s: slice(s * c, (s + 1) * c)
        ksl = lambda h: slice(h * GLA_DK, (h + 1) * GLA_DK)
        vsl = lambda h: slice(h * GLA_DV, (h + 1) * GLA_DV)
        q_ins, k_outs, g_ends, a_s = {}, {}, {}, {}
        for s in subs:
            for h in heads:
                bc = bc_all[rsl(s), ksl(h)]
                k = k_ref[rsl(s), ksl(h)]
                bcl = bc[c - 1:c]
                q_ins[s, h] = q_ref[rsl(s), ksl(h)] * scale * jnp.exp(bc)
                k_outs[s, h] = k * jnp.exp(bcl - bc)
                g_ends[s, h] = jnp.exp(bcl)
                a_s[s, h] = jnp.where(incl, _dot_nt(q_ins[s, h], k * jnp.exp(-bc)), 0.0)
        o_intra = {(s, h): _dot(a_s[s, h], v_ref[rsl(s), vsl(h)]) for s in subs for h in heads}
        for s in subs:
            sts = [st_ref[h] for h in heads]
            o_inter = [_dot_nt(q_ins[s, h], sts[h]) for h in heads]
            kvs = [_dot_tn(v_ref[rsl(s), vsl(h)], k_outs[s, h]) for h in heads]
            for h in heads:
                st_ref[h] = sts[h] * g_ends[s, h] + kvs[h]
                o = o_inter[h] + o_intra[s, h]
                o = o * lax.rsqrt(jnp.mean(o * o, -1, keepdims=True) + 1e-6) * ng
                o_ref[rsl(s), vsl(h)] = o * _silu(gate_ref[rsl(s), vsl(h)])

        @pl.when(i == nr - 1)
        def _():
            for h in range(GLA_HEADS):
                s_out_ref[0, h] = st_ref[h].T

    return pl.pallas_call(
        body, grid=(bsz, nr),
        in_specs=[pl.BlockSpec((rows, GLA_QK), lambda b, i: (b * nr + i, 0)),
                  pl.BlockSpec((rows, GLA_QK), lambda b, i: (b * nr + i, 1)),
                  pl.BlockSpec((rows, GLA_V), lambda b, i: (b * nr + i, 1)),
                  pl.BlockSpec((rows, GLA_V), lambda b, i: (b * nr + i, 2)),
                  pl.BlockSpec((rows, GLA_QK), lambda b, i: (b * nr + i, 0)),
                  pl.BlockSpec((1, GLA_DV), lambda b, i: (0, 0))],
        out_specs=[pl.BlockSpec((rows, GLA_V), lambda b, i: (b * nr + i, 0)),
                   pl.BlockSpec((1, GLA_HEADS, GLA_DK, GLA_DV), lambda b, i: (b, 0, 0, 0))],
        out_shape=[jax.ShapeDtypeStruct((bsz * t, GLA_V), F32),
                   jax.ShapeDtypeStruct((bsz, GLA_HEADS, GLA_DK, GLA_DV), F32)],
        scratch_shapes=[pltpu.VMEM((GLA_HEADS, GLA_DV, GLA_DK), F32)],
        compiler_params=_cp(("parallel", "arbitrary")), name=name,
    )(proj, proj, proj, proj, logd, norm_g.reshape(1, -1))


def _gla_prompt(x, bsz, t, w_in, w_gk2, b_gk, norm_g, w_out):
    proj, logd = _gla_proj(x, w_in, w_gk2, b_gk, min(bsz * t, 2048), "gla_proj")
    o, s = _gla_chunks(proj, logd, norm_g, bsz, t, "gla_chunks")
    return _proj_mix(o, w_out), s


def _rwkv_proj(x, xp, bsz, t, mu, w_rkv, w0, w_w1, w_w2, a0, w_a1, w_a2, w_g1, w_g2, tm, name):
    d = D_MODEL
    nt = t // tm
    carried = xp is None
    bf = lambda w: w.astype(BF16)

    def body(*refs):
        if carried:
            x_ref, refs, buf_ref = refs[0], refs[1:-1], refs[-1]
        else:
            x_ref, xp_ref, refs = refs[0], refs[1], refs[2:]
        mu_ref, wr_ref, w0_ref, ww1, ww2, a0_ref, wa1, wa2, wg1, wg2, r_ref, k_ref, v_ref, lw_ref, a_ref, g_ref = refs
        x = x_ref[...]
        if carried:
            @pl.when(pl.program_id(1) == 0)
            def _():
                buf_ref[pl.ds(0, SUBLANES)] = jnp.zeros((SUBLANES, d), F32)

            buf_ref[pl.ds(SUBLANES, tm)] = x
            x_prev = buf_ref[pl.ds(SUBLANES - 1, tm)]
            buf_ref[pl.ds(0, SUBLANES)] = x[tm - SUBLANES:]
        else:
            x_prev = xp_ref[...]
        dx = x_prev - x
        xs = lambda s: (x + dx * mu_ref[s:s + 1]).astype(BF16)
        for s, o_ref in enumerate((r_ref, k_ref, v_ref)):
            o_ref[...] = _dot(xs(s), wr_ref[s])
        lora = lambda h, w2_ref: _dot(h.astype(BF16), w2_ref[...])
        w_log = -_softplus(-(w0_ref[...] + lora(jnp.tanh(_dot(xs(3), ww1[...])), ww2))) - 0.5
        lw_ref[...] = -jnp.exp(w_log)
        a_ref[...] = _sigmoid(a0_ref[...] + lora(_dot(xs(4), wa1[...]), wa2))
        g_ref[...] = lora(_sigmoid(_dot(xs(5), wg1[...])), wg2)

    fulls = (mu, bf(w_rkv), w0.reshape(1, -1), bf(w_w1), bf(w_w2), a0.reshape(1, -1), bf(w_a1), bf(w_a2), bf(w_g1),
             bf(w_g2))
    tile = pl.BlockSpec((tm, d), lambda b, i: (b * nt + i, 0))
    return pl.pallas_call(
        body, grid=(bsz, nt),
        in_specs=[tile] * (1 if carried else 2) + [_resident_spec(a) for a in fulls],
        out_specs=[tile] * 6,
        out_shape=[jax.ShapeDtypeStruct((bsz * t, d), F32)] * 6,
        scratch_shapes=[pltpu.VMEM((SUBLANES + tm, d), F32)] if carried else [],
        compiler_params=_cp(("parallel", "arbitrary")), name=name,
    )(*((x,) if carried else (x, xp)), *fulls)


def _rwkv_head_inputs(r, k, v, a, kk_w, ka_w, sl):
    kraw = k[:, sl]
    kkn = _l2n(kraw * kk_w[:, sl], 1.0)
    ah = a[:, sl]
    kh = kraw * (1.0 + (ah - 1.0) * ka_w[:, sl])
    return r[:, sl], kh, v[:, sl], kkn, kkn * ah


def _rwkv_head_out(y, rh, kh, vh, g, rk_w, lng, lnb, sl):
    yc = y - jnp.mean(y, -1, keepdims=True)
    yn = yc * lax.rsqrt(jnp.mean(yc * yc, -1, keepdims=True) + RW_GN_EPS) * lng[:, sl] + lnb[:, sl]
    bonus = jnp.sum(rh * kh * rk_w[:, sl], -1, keepdims=True) * vh
    return (yn + bonus) * g[:, sl]


def _rwkv_chunks(r, k, v, lw, a, g, k_k, k_a, r_k, ln_g, ln_b, bsz, t, name):
    c = RW_CHUNK
    nc = t // c
    hd = RW_HD
    nb = 1
    d = D_MODEL

    def body(r_ref, k_ref, v_ref, lw_ref, a_ref, g_ref, kk_ref, ka_ref, rk_ref, lng_ref, lnb_ref,
             o_ref, s_out_ref, s_ref):
        i = pl.program_id(1)

        @pl.when(i == 0)
        def _():
            s_ref[...] = jnp.zeros_like(s_ref)

        kk_w, ka_w, rk_w, lng, lnb = kk_ref[...], ka_ref[...], rk_ref[...], lng_ref[...], lnb_ref[...]
        tri = _tri(c).astype(F32)
        strict = _tri(c, strict=True)
        incl = _tri(c)
        hsl = lambda n: slice(n * hd, (n + 1) * hd)
        inst = [(m, h) for m in range(nb) for h in range(RW_H)]
        prep = []
        for m in range(nb):
            lw = lw_ref[m]
            gam = _dot3(tri, lw)
            glast = gam[c - 1:c]
            r, k, a = r_ref[m], k_ref[m], a_ref[m]
            kk = k * kk_w
            scale = jnp.concatenate(
                [jnp.broadcast_to(lax.rsqrt(jnp.sum(kk[:, hsl(h)] * kk[:, hsl(h)], -1, keepdims=True) + 1e-6), (c, hd))
                 for h in range(RW_H)], axis=1)
            kkn = kk * scale
            kh = k * (1.0 + (a - 1.0) * ka_w)
            bh = kkn * a
            e_neg = jnp.exp(-gam)
            e_out = jnp.exp(glast - gam)
            prep.append(dict(a1=kkn * jnp.exp(gam - lw), r1=r * jnp.exp(gam), b1=bh * e_neg, k1=kh * e_neg,
                             b1o=bh * e_out, k1o=kh * e_out, g_end=jnp.exp(glast), r=r, kh=kh, v=v_ref[m],
                             g=g_ref[m]))
        ars = [jnp.concatenate([prep[m]["a1"][:, hsl(h)], prep[m]["r1"][:, hsl(h)]], axis=0) for m, h in inst]
        bks = [jnp.concatenate([prep[m]["b1"][:, hsl(h)], prep[m]["k1"][:, hsl(h)]], axis=0) for m, h in inst]
        vhs = [prep[m]["v"][:, hsl(h)] for m, h in inst]
        n = range(len(inst))
        gmats = [_dot_nt(ars[j], bks[j]) for j in n]
        tinvs = _neumann_inv([jnp.where(strict, gm_[:c, :c], 0.0) for gm_ in gmats], c)
        makvs = [_dot(jnp.where(strict, gmats[j][:c, c:], 0.0), vhs[j]) for j in n]
        rbks = [jnp.concatenate([jnp.where(incl, gm_[c:, :c], 0.0), jnp.where(incl, gm_[c:, c:], 0.0)], axis=1)
                for gm_ in gmats]
        a_ss = [_dot_nt(ars[j], s_ref[m, h]) for j, (m, h) in enumerate(inst)]
        uvs = [jnp.concatenate([_dot(tinvs[j], -a_ss[j][:c] - makvs[j]), vhs[j]], axis=0) for j in n]
        ys = [a_ss[j][c:] + _dot(rbks[j], uvs[j]) for j in n]
        svs = [_dot_tn(uvs[j], jnp.concatenate([prep[m]["b1o"][:, hsl(h)], prep[m]["k1o"][:, hsl(h)]], axis=0))
               for j, (m, h) in enumerate(inst)]
        for j, (m, h) in enumerate(inst):
            s_ref[m, h] = s_ref[m, h] * prep[m]["g_end"][:, hsl(h)] + svs[j]
        rkr = [prep[m]["r"] * prep[m]["kh"] * rk_w for m in range(nb)]
        sum1 = [jnp.sum(ys[j], -1, keepdims=True) for j in n]
        bons = [jnp.sum(rkr[m][:, hsl(h)], -1, keepdims=True) for m, h in inst]
        ycs = [ys[j] - sum1[j] * (1.0 / hd) for j in n]
        sum2 = [jnp.sum(ycs[j] * ycs[j], -1, keepdims=True) for j in n]
        for j, (m, h) in enumerate(inst):
            yn = ycs[j] * lax.rsqrt(sum2[j] * (1.0 / hd) + RW_GN_EPS) * lng[:, hsl(h)] + lnb[:, hsl(h)]
            o_ref[m, :, hsl(h)] = (yn + bons[j] * vhs[j]) * prep[m]["g"][:, hsl(h)]

        @pl.when(i == nc - 1)
        def _():
            s_out_ref[...] = s_ref[...]

    row = lambda w: w.reshape(1, d)
    v3 = lambda z: z.reshape(bsz, t, d)
    blk = pl.BlockSpec((nb, c, d), lambda b, i: (b, i, 0))
    par = pl.BlockSpec((1, d), lambda b, i: (0, 0))
    y, s = pl.pallas_call(
        body, grid=(bsz // nb, nc),
        in_specs=[blk] * 6 + [par] * 5,
        out_specs=[blk, pl.BlockSpec((nb, RW_H, hd, hd), lambda b, i: (b, 0, 0, 0))],
        out_shape=[jax.ShapeDtypeStruct((bsz, t, d), F32), jax.ShapeDtypeStruct((bsz, RW_H, hd, hd), F32)],
        scratch_shapes=[pltpu.VMEM((nb, RW_H, hd, hd), F32)],
        compiler_params=_cp(("parallel", "arbitrary")), name=name,
    )(v3(r), v3(k), v3(v), v3(lw), v3(a), v3(g), row(k_k), row(k_a), row(r_k), row(ln_g), row(ln_b))
    return y.reshape(bsz * t, d), s


def _rwkv_prompt(x, bsz, t, mu, w_rkv, w0, w_w1, w_w2, a0, w_a1, w_a2, w_g1, w_g2, k_k, k_a, r_k, gn_g, gn_b, w_o):
    r, k, v, lw, a, g = _rwkv_proj(x, None, bsz, t, mu, w_rkv, w0, w_w1, w_w2, a0, w_a1, w_a2, w_g1, w_g2,
                                   min(t, 256), "rwkv_proj")
    y, s = _rwkv_chunks(r, k, v, lw, a, g, k_k, k_a, r_k, gn_g, gn_b, bsz, t, "rwkv_chunks")
    return _proj_mix(y, w_o), x.reshape(bsz, t, D_MODEL)[:, -1], s


def _eye(n):
    return _iota2((n, n), 0) == _iota2((n, n), 1)


def _to_col(row, eye):
    return jnp.sum(jnp.where(eye, row, 0.0), axis=1, keepdims=True)


def _to_row(col, eye):
    return jnp.sum(jnp.where(eye, col, 0.0), axis=0, keepdims=True)


def _row3(a):
    return a.reshape(a.shape[0], 1, a.shape[1])


def _rows_spec(width, col_block=0, nb=1):
    return pl.BlockSpec((nb, 1, width), lambda b: (b, 0, col_block))


STEP_ROWS = 4


def _gdn_conv_step(proj, buf, conv_w, name):
    def body(p_ref, b0, b1, b2, w_ref, o_ref):
        w = w_ref[...]
        y = p_ref[:, :GDN_CONV_DIM] * w[3:4] + b2[...] * w[2:3] + b1[...] * w[1:2] + b0[...] * w[0:1]
        _gdn_act(y, o_ref)

    rows = [proj, buf[:, 0], buf[:, 1], buf[:, 2]]
    return _rowwise(body, rows, [conv_w], [GDN_CONV_DIM], proj.shape[0], name)[0]


def _gdn_step(qkv, bg, proj, s0, norm_g, name):
    bsz = qkv.shape[0]
    rep = GDN_VH // GDN_KH
    nb = STEP_ROWS if bsz % STEP_ROWS == 0 else 1

    def body(q_ref, k_ref, v_ref, bg_ref, z_ref, ng_ref, s_ref, o_ref, so_ref):
        eye = _eye(GDN_HD)
        ng = ng_ref[...]
        hsl = lambda n: slice(n * GDN_HD, (n + 1) * GDN_HD)
        jk = [(j, kh) for j in range(nb) for kh in range(GDN_KH)]
        jh = [(j, h) for j in range(nb) for h in range(GDN_VH)]
        qrow = {(j, kh): q_ref[j, :, hsl(kh)] for j, kh in jk}
        krow = {(j, kh): k_ref[j, :, hsl(kh)] for j, kh in jk}
        qcol = {i: _to_col(qrow[i], eye) for i in jk}
        kcol = {i: _to_col(krow[i], eye) for i in jk}
        qk = {i: jnp.sum(qrow[i] * krow[i], axis=1, keepdims=True) for i in jk}
        eg = {(j, h): jnp.exp(bg_ref[j, :, GDN_VH + h:GDN_VH + h + 1]) for j, h in jh}
        ks = {(j, h): jnp.sum(kcol[j, h // rep] * s_ref[j, h], axis=0, keepdims=True) for j, h in jh}
        qs = {(j, h): jnp.sum(qcol[j, h // rep] * s_ref[j, h], axis=0, keepdims=True) for j, h in jh}
        v_new = {(j, h): bg_ref[j, :, h:h + 1] * (v_ref[j, :, hsl(h)] - eg[j, h] * ks[j, h]) for j, h in jh}
        o = {(j, h): eg[j, h] * qs[j, h] + qk[j, h // rep] * v_new[j, h] for j, h in jh}
        ms = {i: jnp.mean(o[i] * o[i], -1, keepdims=True) for i in jh}
        for j, h in jh:
            so_ref[j, h] = s_ref[j, h] * eg[j, h] + kcol[j, h // rep] * v_new[j, h]
            o_ref[j, :, hsl(h)] = o[j, h] * lax.rsqrt(ms[j, h] + 1e-6) * ng * _silu(z_ref[j, :, hsl(h)])

    st_spec = pl.BlockSpec((nb, GDN_VH, GDN_HD, GDN_HD), lambda b: (b, 0, 0, 0))
    o, s = pl.pallas_call(
        body, grid=(bsz // nb,),
        in_specs=[_rows_spec(GDN_QK, 0, nb), _rows_spec(GDN_QK, 1, nb), _rows_spec(GDN_V, 1, nb),
                  _rows_spec(LANES, 0, nb), _rows_spec(GDN_V, GDN_CONV_DIM // GDN_V, nb),
                  pl.BlockSpec((1, GDN_HD), lambda b: (0, 0)), st_spec],
        out_specs=[_rows_spec(GDN_V, 0, nb), st_spec],
        out_shape=[jax.ShapeDtypeStruct((bsz, 1, GDN_V), F32), jax.ShapeDtypeStruct(s0.shape, F32)],
        compiler_params=_cp(("parallel",)), name=name,
    )(_row3(qkv), _row3(qkv), _row3(qkv), _row3(bg), _row3(proj), norm_g.reshape(1, -1), s0)
    return o.reshape(bsz, GDN_V), s


def _gdn_sample(x, buf, s0, w_in, conv_w, a_log, dt_bias, norm_g, w_out):
    proj, bg = _gdn_proj(x, w_in, a_log, dt_bias, x.shape[0], "gdn_proj_s")
    qkv = _gdn_conv_step(proj, buf, conv_w, "gdn_conv_s")
    o, s = _gdn_step(qkv, bg, proj, s0, norm_g, "gdn_step_s")
    new_buf = jnp.concatenate([buf[:, 1:], proj[:, None, :GDN_CONV_DIM]], axis=1)
    return _proj_mix(o, w_out), new_buf, s


def _s5_step(x, s_re, s_im, lr, li, bcat, ccat, d_skip, name):
    def body(x_ref, sr_ref, si_ref, lr_ref, li_ref, b_ref, c_ref, d_ref, z_ref, hr_ref, hi_ref):
        x = x_ref[...]
        dsk = d_ref[...]
        for j in range(S5_NT):
            hs = slice(j * S5_HT, (j + 1) * S5_HT)
            xs = x[:, j * LANES:(j + 1) * LANES]
            bu = _dot(xs, b_ref[j])
            lam_r, lam_i = lr_ref[:, hs], li_ref[:, hs]
            sr, si = sr_ref[:, hs], si_ref[:, hs]
            h_re = lam_r * sr - lam_i * si + bu[:, :S5_HT]
            h_im = lam_r * si + lam_i * sr + bu[:, S5_HT:]
            hr_ref[:, hs] = h_re
            hi_ref[:, hs] = h_im
            cj = c_ref[j]
            y = _dot(h_re, cj[:S5_HT]) + _dot(h_im, cj[S5_HT:])
            z_ref[:, j * LANES:(j + 1) * LANES] = _gelu(y + dsk[:, j * LANES:(j + 1) * LANES] * xs)

    fulls = [lr, li, bcat, ccat, d_skip.reshape(1, -1)]
    return _rowwise(body, [x, s_re, s_im], fulls, [D_MODEL, S5_H, S5_H], x.shape[0], name)


def _s5_sample(x, s_re0, s_im0, a_re, a_im, log_dt, b_re, b_im, c_re, c_im, d_skip, w_o, w_gate):
    m = x.shape[0]
    lr, li, bcat, ccat = _s5_weights(a_re, a_im, log_dt, b_re, b_im, c_re, c_im)
    z, h_re, h_im = _s5_step(x, s_re0.reshape(m, S5_H), s_im0.reshape(m, S5_H), lr, li, bcat, ccat, d_skip, "s5_step_s")
    return _s5_mix(z, w_o, w_gate), h_re.reshape(m, S5_GROUPS, S5_STATE), h_im.reshape(m, S5_GROUPS, S5_STATE)


def _gla_step(proj, logd, s0, norm_g, name):
    bsz = proj.shape[0]
    scale = GLA_DK ** -0.5
    nb = STEP_ROWS if bsz % STEP_ROWS == 0 else 1

    def body(q_ref, k_ref, v_ref, gate_ref, ld_ref, ng_ref, s_ref, o_ref, so_ref):
        eye = _eye(GLA_DK)
        ng = ng_ref[...]
        ksl = lambda h: slice(h * GLA_DK, (h + 1) * GLA_DK)
        vsl = lambda h: slice(h * GLA_DV, (h + 1) * GLA_DV)
        jh = [(j, h) for j in range(nb) for h in range(GLA_HEADS)]
        bc = {(j, h): ld_ref[j, :, ksl(h)] for j, h in jh}
        k = {(j, h): k_ref[j, :, ksl(h)] for j, h in jh}
        q_in = {(j, h): q_ref[j, :, ksl(h)] * scale * jnp.exp(bc[j, h]) for j, h in jh}
        a = {i: jnp.sum(q_in[i] * (k[i] * jnp.exp(-bc[i])), axis=1, keepdims=True) for i in jh}
        qcol = {i: _to_col(q_in[i], eye) for i in jh}
        gcol = {i: _to_col(jnp.exp(bc[i]), eye) for i in jh}
        kcol = {i: _to_col(k[i], eye) for i in jh}
        o = {(j, h): jnp.sum(qcol[j, h] * s_ref[j, h], axis=0, keepdims=True) + a[j, h] * v_ref[j, :, vsl(h)]
             for j, h in jh}
        ms = {i: jnp.mean(o[i] * o[i], -1, keepdims=True) for i in jh}
        for j, h in jh:
            so_ref[j, h] = s_ref[j, h] * gcol[j, h] + kcol[j, h] * v_ref[j, :, vsl(h)]
            o_ref[j, :, vsl(h)] = o[j, h] * lax.rsqrt(ms[j, h] + 1e-6) * ng * _silu(gate_ref[j, :, vsl(h)])

    st_spec = pl.BlockSpec((nb, GLA_HEADS, GLA_DK, GLA_DV), lambda b: (b, 0, 0, 0))
    o, s = pl.pallas_call(
        body, grid=(bsz // nb,),
        in_specs=[_rows_spec(GLA_QK, 0, nb), _rows_spec(GLA_QK, 1, nb), _rows_spec(GLA_V, 1, nb),
                  _rows_spec(GLA_V, 2, nb), _rows_spec(GLA_QK, 0, nb), pl.BlockSpec((1, GLA_DV), lambda b: (0, 0)),
                  st_spec],
        out_specs=[_rows_spec(GLA_V, 0, nb), st_spec],
        out_shape=[jax.ShapeDtypeStruct((bsz, 1, GLA_V), F32), jax.ShapeDtypeStruct(s0.shape, F32)],
        compiler_params=_cp(("parallel",)), name=name,
    )(_row3(proj), _row3(proj), _row3(proj), _row3(proj), _row3(logd), norm_g.reshape(1, -1), s0)
    return o.reshape(bsz, GLA_V), s


def _gla_sample(x, s0, w_in, w_gk2, b_gk, norm_g, w_out):
    proj, logd = _gla_proj(x, w_in, w_gk2, b_gk, x.shape[0], "gla_proj_s")
    o, s = _gla_step(proj, logd, s0, norm_g, "gla_step_s")
    return _proj_mix(o, w_out), s


def _dot3_r(x, l):
    hi = x.astype(BF16).astype(F32)
    r1 = x - hi
    mid = r1.astype(BF16).astype(F32)
    lo = r1 - mid
    return _dot(hi, l) + _dot(mid, l) + _dot(lo, l)


def _head_ones():
    return jnp.kron(jnp.eye(RW_H, dtype=F32), jnp.ones((RW_HD, RW_HD), F32))


def _rwkv_step_prep(r, k, v, lw, a, k_k, k_a, name):
    bsz = r.shape[0]

    def body(r_ref, k_ref, v_ref, lw_ref, a_ref, kk_ref, ka_ref, ones_ref, rt, kt, vt, kkt, bt, dt, kh_ref):
        k, a = k_ref[...], a_ref[...]
        kk = k * kk_ref[...]
        kkn = kk * lax.rsqrt(_dot3_r(kk * kk, ones_ref[...]) + 1e-6)
        kh = k * (1.0 + (a - 1.0) * ka_ref[...])
        kh_ref[...] = kh
        rt[...] = r_ref[...].T
        kt[...] = kh.T
        vt[...] = v_ref[...].T
        kkt[...] = kkn.T
        bt[...] = (kkn * a).T
        dt[...] = jnp.exp(lw_ref[...]).T

    args = (r, k, v, lw, a, k_k.reshape(1, -1), k_a.reshape(1, -1), _head_ones())
    tshape = jax.ShapeDtypeStruct((D_MODEL, bsz), F32)
    return pl.pallas_call(
        body, grid=(1,), in_specs=[_full_spec(t) for t in args],
        out_specs=[pl.BlockSpec((D_MODEL, bsz), lambda i: (0, 0))] * 6 + [pl.BlockSpec((bsz, D_MODEL), lambda i: (0, 0))],
        out_shape=[tshape] * 6 + [jax.ShapeDtypeStruct((bsz, D_MODEL), F32)],
        compiler_params=_cp(("arbitrary",)), name=name)(*args)


def _rwkv_step_lanes(rt, kt, vt, kkt, bt, dt, s_t, name):
    hd = RW_HD
    bsz = s_t.shape[-1]

    def body(rt_ref, kt_ref, vt_ref, kkt_ref, bt_ref, dt_ref, s_ref, y_ref, so_ref):
        hs = pl.ds(pl.multiple_of(pl.program_id(0) * hd, hd), hd)
        r_h, k_h, v_h, kk_h, b_h, d_h = (ref[hs, :] for ref in (rt_ref, kt_ref, vt_ref, kkt_ref, bt_ref, dt_ref))
        ys = []
        for vi in range(hd):
            s = s_ref[0, vi]
            sa = -jnp.sum(s * kk_h, axis=0, keepdims=True)
            s_new = s * d_h + sa * b_h + v_h[vi:vi + 1] * k_h
            so_ref[0, vi] = s_new
            ys.append(jnp.sum(s_new * r_h, axis=0, keepdims=True))
        y_ref[...] = jnp.concatenate(ys, axis=0)

    vec = pl.BlockSpec((D_MODEL, bsz), lambda h: (0, 0))
    st = pl.BlockSpec((1, hd, hd, bsz), lambda h: (h, 0, 0, 0))
    return pl.pallas_call(
        body, grid=(RW_H,), in_specs=[vec] * 6 + [st],
        out_specs=[pl.BlockSpec((hd, bsz), lambda h: (h, 0)), st],
        out_shape=[jax.ShapeDtypeStruct((D_MODEL, bsz), F32), jax.ShapeDtypeStruct(s_t.shape, F32)],
        compiler_params=_cp(("parallel",)), name=name)(rt, kt, vt, kkt, bt, dt, s_t)


def _rwkv_step_out(yt, r, kh, v, g, r_k, gn_g, gn_b, name):
    def body(yt_ref, r_ref, k_ref, v_ref, g_ref, rk_ref, gg_ref, gb_ref, ones_ref, o_ref):
        ones = ones_ref[...]
        y = yt_ref[...].T
        yc = y - _dot3_r(y, ones) * (1.0 / RW_HD)
        yn = yc * lax.rsqrt(_dot3_r(yc * yc, ones) * (1.0 / RW_HD) + RW_GN_EPS) * gg_ref[...] + gb_ref[...]
        bonus = _dot3_r(r_ref[...] * k_ref[...] * rk_ref[...], ones) * v_ref[...]
        o_ref[...] = (yn + bonus) * g_ref[...]

    row = lambda w: w.reshape(1, D_MODEL)
    args = (yt, r, kh, v, g, row(r_k), row(gn_g), row(gn_b), _head_ones())
    return pl.pallas_call(
        body, grid=(1,), in_specs=[_full_spec(t) for t in args],
        out_specs=pl.BlockSpec(r.shape, lambda i: (0, 0)),
        out_shape=jax.ShapeDtypeStruct(r.shape, F32),
        compiler_params=_cp(("arbitrary",)), name=name)(*args)


def _rwkv_sample(x, shift0, s0, mu, w_rkv, w0, w_w1, w_w2, a0, w_a1, w_a2, w_g1, w_g2, k_k, k_a, r_k, gn_g, gn_b, w_o):
    m = x.shape[0]
    r, k, v, lw, a, g = _rwkv_proj(x, shift0, 1, m, mu, w_rkv, w0, w_w1, w_w2, a0, w_a1, w_a2, w_g1, w_g2, m,
                                   "rwkv_proj_s")
    rt, kt, vt, kkt, bt, dt, kh = _rwkv_step_prep(r, k, v, lw, a, k_k, k_a, "rwkv_prep_s")
    yt, s_t = _rwkv_step_lanes(rt, kt, vt, kkt, bt, dt, jnp.transpose(s0, (1, 2, 3, 0)), "rwkv_step_s")
    o = _rwkv_step_out(yt, r, kh, v, g, r_k, gn_g, gn_b, "rwkv_out_s")
    return _proj_mix(o, w_o), x, jnp.transpose(s_t, (3, 0, 1, 2))


def kernel(x_prompt, x_sample, state_gdn_conv, state_gdn, state_s5_re, state_s5_im, state_gla, state_rwkv_shift,
           state_rwkv, p_prompt, p_sample, gdn_w_in, gdn_conv_w, gdn_a_log, gdn_dt_bias, gdn_norm_g, gdn_w_out,
           s5_a_re, s5_a_im, s5_log_dt, s5_b_re, s5_b_im, s5_c_re, s5_c_im, s5_d, s5_w_o, s5_w_gate,
           gla_w_in, gla_w_gk2, gla_b_gk, gla_norm_g, gla_w_out,
           rwkv_mu, rwkv_w_rkv, rwkv_w0, rwkv_w_w1, rwkv_w_w2, rwkv_a0, rwkv_w_a1, rwkv_w_a2, rwkv_w_g1, rwkv_w_g2,
           rwkv_k_k, rwkv_k_a, rwkv_r_k, rwkv_ln_g, rwkv_ln_b, rwkv_w_o,
           ln_mix_g, ln_mix_b, ln_ffn_g, ln_ffn_b, mlp_w1, mlp_w2, ple_w, ple_gate_w):
    bsz, t, d = x_prompt.shape
    bs = x_sample.shape[0]
    gdn_w = (gdn_w_in[0], gdn_conv_w[0], gdn_a_log[0], gdn_dt_bias[0], gdn_norm_g[0], gdn_w_out[0])
    s5_w = (s5_a_re[0], s5_a_im[0], s5_log_dt[0], s5_b_re[0], s5_b_im[0], s5_c_re[0], s5_c_im[0], s5_d[0],
            s5_w_o[0], s5_w_gate[0])
    gla_w = (gla_w_in[0], gla_w_gk2[0], gla_b_gk[0], gla_norm_g[0], gla_w_out[0])
    rwkv_w = (rwkv_mu[0], rwkv_w_rkv[0], rwkv_w0[0], rwkv_w_w1[0], rwkv_w_w2[0], rwkv_a0[0], rwkv_w_a1[0],
              rwkv_w_a2[0], rwkv_w_g1[0], rwkv_w_g2[0], rwkv_k_k[0], rwkv_k_a[0], rwkv_r_k[0], rwkv_ln_g[0],
              rwkv_ln_b[0], rwkv_w_o[0])

    def tail(x, mix, p_all, i, tm, tag):
        return _block_tail(x, mix, ln_mix_g[i], ln_mix_b[i], p_all, i, mlp_w1[i], mlp_w2[i], ln_ffn_g[i], ln_ffn_b[i],
                           ple_w[i], ple_gate_w[i], tm, 1024, f"tail{i}_{tag}")

    xp = x_prompt.reshape(bsz * t, d)
    pp = p_prompt.reshape(DEPTH, bsz * t, D_PLE)
    tm_p = min(bsz * t, 512)
    mix, gc_p, gs_p = _gdn_prompt(xp, bsz, t, *gdn_w)
    xp = tail(xp, mix, pp, 0, tm_p, "p")
    mix, sr_p, si_p = _s5_prompt(xp, bsz, t, *s5_w)
    xp = tail(xp, mix, pp, 1, tm_p, "p")
    mix, la_p = _gla_prompt(xp, bsz, t, *gla_w)
    xp = tail(xp, mix, pp, 2, tm_p, "p")
    mix, sh_p, rs_p = _rwkv_prompt(xp, bsz, t, *rwkv_w)
    xp = tail(xp, mix, pp, 3, tm_p, "p")

    xs = x_sample.reshape(bs, d)
    ps = p_sample.reshape(DEPTH, bs, D_PLE)
    mix, gc_s, gs_s = _gdn_sample(xs, state_gdn_conv[0], state_gdn[0], *gdn_w)
    xs = tail(xs, mix, ps, 0, bs, "s")
    mix, sr_s, si_s = _s5_sample(xs, state_s5_re[0], state_s5_im[0], *s5_w)
    xs = tail(xs, mix, ps, 1, bs, "s")
    mix, la_s = _gla_sample(xs, state_gla[0], *gla_w)
    xs = tail(xs, mix, ps, 2, bs, "s")
    mix, sh_s, rs_s = _rwkv_sample(xs, state_rwkv_shift[0], state_rwkv[0], *rwkv_w)
    xs = tail(xs, mix, ps, 3, bs, "s")

    e = lambda a: a[None]
    return (xp.reshape(bsz, t, d), xs.reshape(bs, 1, d), e(gc_p), e(gc_s), e(gs_p), e(gs_s), e(sr_p), e(sr_s),
            e(si_p), e(si_s), e(la_p), e(la_s), e(sh_p), e(sh_s), e(rs_p), e(rs_s))
```

```python
import math
from typing import Callable, NamedTuple

import jax
import jax.numpy as jnp
from jax import lax
from jax.experimental import pallas as pl
from jax.experimental.pallas import tpu as pltpu

F32 = jnp.float32
BF16 = jnp.bfloat16

D_MODEL = 1024
DEPTH = 4
D_PLE = 256
D_FF = 4 * D_MODEL
LN_EPS = 1e-5
ALPHA = (2.0 * DEPTH) ** 0.25

GDN_HD = 128
GDN_KH = 8
GDN_VH = 16
GDN_QK = GDN_KH * GDN_HD
GDN_V = GDN_VH * GDN_HD
GDN_CONV_DIM = 2 * GDN_QK + GDN_V
GDN_CONV_W = 4
GDN_CHUNK = 64

S5_GROUP = 16
S5_GROUPS = D_MODEL // S5_GROUP
S5_STATE = 64
S5_H = S5_GROUPS * S5_STATE

GLA_HEADS = 4
GLA_DK = 128
GLA_DV = 256
GLA_QK = GLA_HEADS * GLA_DK
GLA_V = GLA_HEADS * GLA_DV
GLA_RANK = 16
GLA_GATE_NORM = 16.0
GLA_CHUNK = 16

RW_HD = 64
RW_H = D_MODEL // RW_HD
RW_GN_EPS = 64e-5
RW_CHUNK = 64

LANES = 128
SUBLANES = 8
VMEM_LIMIT = 56 * 1024 * 1024


def _cp(sem, vmem=VMEM_LIMIT):
    return pltpu.CompilerParams(dimension_semantics=sem, vmem_limit_bytes=vmem)


def _dot(a, b):
    return jnp.dot(a, b, preferred_element_type=F32)


def _dot_nt(a, b):
    return lax.dot_general(a, b, (((1,), (1,)), ((), ())), preferred_element_type=F32)


def _dot_tn(a, b):
    return lax.dot_general(a, b, (((0,), (0,)), ((), ())), preferred_element_type=F32)


def _dot3(l, x):
    hi = x.astype(BF16).astype(F32)
    r1 = x - hi
    mid = r1.astype(BF16).astype(F32)
    lo = r1 - mid
    return _dot(l, hi) + _dot(l, mid) + _dot(l, lo)


def _softplus(x):
    return jnp.maximum(x, 0.0) + jnp.log(1.0 + jnp.exp(-jnp.abs(x)))


def _sigmoid(x):
    return 1.0 / (1.0 + jnp.exp(-x))


def _silu(x):
    return x * _sigmoid(x)


def _ln(x, g, b):
    xc = x - jnp.mean(x, -1, keepdims=True)
    var = jnp.mean(xc * xc, -1, keepdims=True)
    return xc * lax.rsqrt(var + LN_EPS) * g + b


def _iota2(shape, axis):
    return lax.broadcasted_iota(jnp.int32, shape, axis)


def _tri(n, strict=False, block=None):
    r = _iota2((n, n), 0)
    c = _iota2((n, n), 1)
    m = (r > c) if strict else (r >= c)
    if block is not None:
        m = m & ((r // block) == (c // block))
    return m


def _full_spec(a):
    n = a.ndim
    return pl.BlockSpec(a.shape, lambda *_: (0,) * n)


def _mm(x, w, n_cols, tm, tn, name, side_fn, side_fulls, side_cols):
    m, k = x.shape

    def body(x_ref, w_ref, *rest):
        side_refs, o_ref, s_ref, xb_ref = rest[:-3], rest[-3], rest[-2], rest[-1]

        @pl.when(pl.program_id(1) == 0)
        def _():
            xb_ref[...] = x_ref[...].astype(BF16)
            s_ref[...] = side_fn(xb_ref[...], *side_refs)

        o_ref[...] = _dot(xb_ref[...], w_ref[...])

    return pl.pallas_call(
        body, grid=(m // tm, n_cols // tn),
        in_specs=[pl.BlockSpec((tm, k), lambda i, j: (i, 0)), pl.BlockSpec((k, tn), lambda i, j: (0, j))]
        + [_full_spec(a) for a in side_fulls],
        out_specs=[pl.BlockSpec((tm, tn), lambda i, j: (i, j)), pl.BlockSpec((tm, side_cols), lambda i, j: (i, 0))],
        out_shape=[jax.ShapeDtypeStruct((m, n_cols), F32), jax.ShapeDtypeStruct((m, side_cols), F32)],
        scratch_shapes=[pltpu.VMEM((tm, k), BF16)],
        compiler_params=_cp(("parallel", "arbitrary")), name=name)(x, w.astype(BF16), *side_fulls)


def _rowwise(body, rows, fulls, out_cols, tm, name):
    m = rows[0].shape[0]
    in_specs = [pl.BlockSpec((tm, r.shape[1]), lambda i: (i, 0)) for r in rows] + [_full_spec(f) for f in fulls]
    out_specs = [pl.BlockSpec((tm, c), lambda i: (i, 0)) for c in out_cols]
    out_shape = [jax.ShapeDtypeStruct((m, c), F32) for c in out_cols]
    res = pl.pallas_call(body, grid=(m // tm,), in_specs=in_specs, out_specs=out_specs, out_shape=out_shape,
                         compiler_params=_cp(("parallel",)), name=name)(*rows, *fulls)
    return res


def _resident_spec(a):
    n = a.ndim
    return pl.BlockSpec(a.shape, lambda *_: (0,) * n, pipeline_mode=pl.Buffered(1))


class Mix(NamedTuple):
    rows: tuple
    fulls: tuple
    fn: Callable


def _proj_mix(o, w_out):
    return Mix((o,), (w_out.astype(BF16),), lambda o_t, w_ref: _dot(o_t.astype(BF16), w_ref[...]))


def _block_tail(x, mix, lm_g, lm_b, p_all, layer, w1, w2, g, b, ple_w, gate_w, tm, tf, name):
    m = x.shape[0]
    nf = D_FF // tf
    nrow, nfull = len(mix.rows), len(mix.fulls)

    def body(*refs):
        x_ref, p_ref = refs[0], refs[1 + nrow]
        row_refs = refs[1:1 + nrow]
        full_refs = refs[2 + nrow:2 + nrow + nfull]
        lmg_ref, lmb_ref, w1_ref, w2_ref, g_ref, b_ref, pw_ref, gw_ref, o_ref, h_ref = refs[2 + nrow + nfull:]
        x1 = _ln(ALPHA * x_ref[...] + mix.fn(*[r[...] for r in row_refs], *full_refs), lmg_ref[...], lmb_ref[...])
        xb = x1.astype(BF16)
        for f in range(nf):
            h = jnp.maximum(_dot(xb, w1_ref[:, f * tf:(f + 1) * tf]), 0.0)
            h_ref[:, f * tf:(f + 1) * tf] = (h * h).astype(BF16)
        x2 = _ln(ALPHA * x1 + _dot(h_ref[...], w2_ref[...]), g_ref[...], b_ref[...])
        gate = _sigmoid(_dot(x2.astype(BF16), gw_ref[...]))
        o_ref[...] = x2 + _dot(p_ref[...].astype(BF16), pw_ref[...]) * gate

    row = lambda v: v.reshape(1, -1)
    fulls = mix.fulls + (row(lm_g), row(lm_b), w1.astype(BF16), w2.astype(BF16), row(g), row(b), ple_w.astype(BF16),
                         gate_w.astype(BF16))
    return pl.pallas_call(
        body, grid=(m // tm,),
        in_specs=[pl.BlockSpec((tm, D_MODEL), lambda i: (i, 0))]
        + [pl.BlockSpec((tm, r.shape[1]), lambda i: (i, 0)) for r in mix.rows]
        + [pl.BlockSpec((None, tm, D_PLE), lambda i: (layer, i, 0))]
        + [_resident_spec(a) for a in fulls],
        out_specs=pl.BlockSpec((tm, D_MODEL), lambda i: (i, 0)),
        out_shape=jax.ShapeDtypeStruct((m, D_MODEL), F32),
        scratch_shapes=[pltpu.VMEM((tm, D_FF), BF16)],
        compiler_params=_cp(("parallel",)), name=name,
    )(x, *mix.rows, p_all, *fulls)


def _gdn_proj(x, w_in, a_log, dt_bias, tm, name):
    w_ba = jnp.pad(w_in[:, GDN_CONV_DIM + GDN_V:], ((0, 0), (0, LANES - 2 * GDN_VH))).astype(BF16)
    pad = lambda v: jnp.pad(v.reshape(1, -1), ((0, 0), (GDN_VH, LANES - 2 * GDN_VH)))

    def gates(xb, w_ref, al_ref, dt_ref):
        y = _dot(xb, w_ref[...])
        lane = _iota2(y.shape, 1)
        g = -jnp.exp(al_ref[...]) * _softplus(y + dt_ref[...])
        return jnp.where(lane < GDN_VH, _sigmoid(y), g)

    return _mm(x, w_in, GDN_CONV_DIM + GDN_V, tm, 512, name, gates, (w_ba, pad(a_log), pad(dt_bias)), LANES)


def _l2n(x, scale):
    return x * (lax.rsqrt(jnp.sum(x * x, -1, keepdims=True) + 1e-6) * scale)


def _gdn_act(y, o_ref):
    y = _silu(y)
    for h in range(GDN_KH):
        sl = slice(h * GDN_HD, (h + 1) * GDN_HD)
        o_ref[:, sl] = _l2n(y[:, sl], GDN_HD ** -0.5)
        sl = slice(GDN_QK + h * GDN_HD, GDN_QK + (h + 1) * GDN_HD)
        o_ref[:, sl] = _l2n(y[:, sl], 1.0)
    o_ref[:, 2 * GDN_QK:] = y[:, 2 * GDN_QK:]


def _gdn_conv_prompt(proj, conv_w, bsz, t, tt, name):
    nt = t // tt
    c = GDN_CONV_DIM

    def body(x_ref, w_ref, o_ref, tail_ref, buf_ref):
        i = pl.program_id(1)

        @pl.when(i == 0)
        def _():
            buf_ref[pl.ds(0, SUBLANES)] = jnp.zeros((SUBLANES, c), F32)

        x = x_ref[...]
        w = w_ref[...]
        buf_ref[pl.ds(SUBLANES, tt)] = x
        y = x * w[3:4]
        for j in range(1, GDN_CONV_W):
            y = y + buf_ref[pl.ds(SUBLANES - j, tt)] * w[3 - j:4 - j]
        _gdn_act(y, o_ref)
        buf_ref[pl.ds(0, SUBLANES)] = x[tt - SUBLANES:]
        tail_ref[0] = x[tt - SUBLANES:]

    return pl.pallas_call(
        body, grid=(bsz, nt),
        in_specs=[pl.BlockSpec((tt, c), lambda b, i: (b * nt + i, 0)), _full_spec(conv_w)],
        out_specs=[pl.BlockSpec((tt, c), lambda b, i: (b * nt + i, 0)),
                   pl.BlockSpec((1, SUBLANES, c), lambda b, i: (b, 0, 0))],
        out_shape=[jax.ShapeDtypeStruct((bsz * t, c), F32), jax.ShapeDtypeStruct((bsz, SUBLANES, c), F32)],
        scratch_shapes=[pltpu.VMEM((SUBLANES + tt, c), F32)],
        compiler_params=_cp(("parallel", "arbitrary")), name=name)(proj, conv_w)


def _neumann_inv(ms, n):
    eye = (_iota2((n, n), 0) == _iota2((n, n), 1)).astype(F32)
    ts = [eye - m for m in ms]
    ps = [_dot(m, m) for m in ms]
    k = 2
    while True:
        ts = [t + _dot(t, p) for t, p in zip(ts, ps)]
        k *= 2
        if k >= n:
            return ts
        ps = [_dot(p, p) for p in ps]


def _gdn_chunks(qkv, bg, proj, norm_g, bsz, t, name):
    c = GDN_CHUNK
    nc = t // c
    z_blk = GDN_CONV_DIM // GDN_V

    def body(q_ref, k_ref, v_ref, bg_ref, z_ref, ng_ref, o_ref, s_out_ref, s_ref):
        i = pl.program_id(1)

        @pl.when(i == 0)
        def _():
            s_ref[...] = jnp.zeros_like(s_ref)

        bgv = bg_ref[...]
        ltri = _tri(c).astype(F32)
        gc = _dot3(ltri, bgv)
        gct = jnp.concatenate([gc, jnp.zeros_like(gc)], axis=0).T
        incl = _tri(c)
        strict = _tri(c, strict=True)
        ng = ng_ref[...]
        rep = GDN_VH // GDN_KH
        heads = range(GDN_VH)
        hsl = lambda n: slice(n * GDN_HD, (n + 1) * GDN_HD)
        gram = [_dot_nt(jnp.concatenate([k_ref[:, hsl(n)], q_ref[:, hsl(n)]], axis=0), k_ref[:, hsl(n)])
                for n in range(GDN_KH)]
        ms, aqks, rhss, q_ins, k_outs, g_ends = [], [], [], [], [], []
        for h in heads:
            kh = h // rep
            kk = k_ref[:, hsl(kh)]
            beta = bgv[:, h:h + 1]
            gcol = gc[:, GDN_VH + h:GDN_VH + h + 1]
            grow = gct[GDN_VH + h:GDN_VH + h + 1, :c]
            decay = jnp.where(incl, jnp.exp(jnp.where(incl, gcol - grow, 0.0)), 0.0)
            ms.append(jnp.where(strict, gram[kh][:c] * beta * decay, 0.0))
            aqks.append(gram[kh][c:] * decay)
            egc = jnp.exp(gcol)
            rhss.append(jnp.concatenate([v_ref[:, hsl(h)] * beta, kk * (beta * egc)], axis=1))
            q_ins.append(q_ref[:, hsl(kh)] * egc)
            glast = gc[c - 1:c, GDN_VH + h:GDN_VH + h + 1]
            k_outs.append(kk * jnp.exp(glast - gcol))
            g_ends.append(jnp.exp(glast))
        tinvs = _neumann_inv(ms, c)
        sols = [_dot(tinvs[h], rhss[h]) for h in heads]
        wqs = [_dot(jnp.concatenate([sols[h][:, GDN_HD:], q_ins[h]], axis=0), s_ref[h]) for h in heads]
        v_news = [sols[h][:, :GDN_HD] - wqs[h][:c] for h in heads]
        avs = [_dot(aqks[h], v_news[h]) for h in heads]
        kvs = [_dot_tn(k_outs[h], v_news[h]) for h in heads]
        for h in heads:
            s_ref[h] = s_ref[h] * g_ends[h] + kvs[h]
            o = wqs[h][c:] + avs[h]
            o = o * lax.rsqrt(jnp.mean(o * o, -1, keepdims=True) + 1e-6) * ng
            o_ref[:, hsl(h)] = o * _silu(z_ref[:, hsl(h)])

        @pl.when(i == nc - 1)
        def _():
            s_out_ref[0] = s_ref[...]

    return pl.pallas_call(
        body, grid=(bsz, nc),
        in_specs=[pl.BlockSpec((c, GDN_QK), lambda b, i: (b * nc + i, 0)),
                  pl.BlockSpec((c, GDN_QK), lambda b, i: (b * nc + i, 1)),
                  pl.BlockSpec((c, GDN_V), lambda b, i: (b * nc + i, 1)),
                  pl.BlockSpec((c, LANES), lambda b, i: (b * nc + i, 0)),
                  pl.BlockSpec((c, GDN_V), lambda b, i: (b * nc + i, z_blk)),
                  pl.BlockSpec((1, GDN_HD), lambda b, i: (0, 0))],
        out_specs=[pl.BlockSpec((c, GDN_V), lambda b, i: (b * nc + i, 0)),
                   pl.BlockSpec((1, GDN_VH, GDN_HD, GDN_HD), lambda b, i: (b, 0, 0, 0))],
        out_shape=[jax.ShapeDtypeStruct((bsz * t, GDN_V), F32),
                   jax.ShapeDtypeStruct((bsz, GDN_VH, GDN_HD, GDN_HD), F32)],
        scratch_shapes=[pltpu.VMEM((GDN_VH, GDN_HD, GDN_HD), F32)],
        compiler_params=_cp(("parallel", "arbitrary")), name=name,
    )(qkv, qkv, qkv, bg, proj, norm_g.reshape(1, -1))


def _gdn_prompt(x, bsz, t, w_in, conv_w, a_log, dt_bias, norm_g, w_out):
    proj, bg = _gdn_proj(x, w_in, a_log, dt_bias, min(bsz * t, 2048), "gdn_proj")
    qkv, tail = _gdn_conv_prompt(proj, conv_w, bsz, t, min(t, 256), "gdn_conv")
    o, s = _gdn_chunks(qkv, bg, proj, norm_g, bsz, t, "gdn_chunks")
    return _proj_mix(o, w_out), tail[:, SUBLANES - (GDN_CONV_W - 1):], s


def _s5_discretize(a_re, a_im, log_dt, b_re, b_im):
    g, p = a_re.shape
    bt_re = jnp.swapaxes(b_re, 1, 2)
    bt_im = jnp.swapaxes(b_im, 1, 2)

    def body(ar_ref, ai_ref, ldt_ref, br_ref, bi_ref, lr_ref, li_ref, bbr_ref, bbi_ref):
        ar, ai = ar_ref[...], ai_ref[...]
        dt = jnp.exp(ldt_ref[...])
        mag = jnp.exp(ar * dt)
        lr, li = mag * jnp.cos(ai * dt), mag * jnp.sin(ai * dt)
        den = ar * ar + ai * ai
        f_re = ((lr - 1.0) * ar + li * ai) / den
        f_im = (li * ar - (lr - 1.0) * ai) / den
        lr_ref[...] = lr
        li_ref[...] = li
        br, bi = br_ref[...], bi_ref[...]
        fr, fi = f_re[:, None, :], f_im[:, None, :]
        bbr_ref[...] = fr * br - fi * bi
        bbi_ref[...] = fr * bi + fi * br

    args = (a_re, a_im, log_dt.reshape(g, 1), bt_re, bt_im)
    return pl.pallas_call(
        body, grid=(1,), in_specs=[_full_spec(a) for a in args],
        out_specs=[pl.BlockSpec((g, p), lambda i: (0, 0))] * 2 + [pl.BlockSpec(bt_re.shape, lambda i: (0, 0, 0))] * 2,
        out_shape=[jax.ShapeDtypeStruct((g, p), F32)] * 2 + [jax.ShapeDtypeStruct(bt_re.shape, F32)] * 2,
        name="s5_discretize")(*args)


def _blockdiag(a, per):
    g, r, c = a.shape
    a4 = a.reshape(g // per, per, r, c)
    eye = jnp.eye(per, dtype=a.dtype)
    return jnp.einsum("jgrc,gh->jgrhc", a4, eye).reshape(g // per, per * r, per * c)


S5_GPT = LANES // S5_GROUP
S5_NT = S5_GROUPS // S5_GPT
S5_HT = S5_GPT * S5_STATE


def _s5_weights(a_re, a_im, log_dt, b_re, b_im, c_re, c_im):
    lr, li, bbr, bbi = _s5_discretize(a_re, a_im, log_dt, b_re, b_im)
    bcat = jnp.concatenate([_blockdiag(bbr, S5_GPT), _blockdiag(bbi, S5_GPT)], axis=2)
    ccat = jnp.concatenate([_blockdiag(jnp.swapaxes(c_re, 1, 2), S5_GPT),
                            -_blockdiag(jnp.swapaxes(c_im, 1, 2), S5_GPT)], axis=1)
    return lr.reshape(1, S5_H), li.reshape(1, S5_H), bcat, ccat


def _gelu(y):
    return 0.5 * y * (1.0 + jnp.tanh(math.sqrt(2.0 / math.pi) * (y + 0.044715 * (y * y * y))))


def _s5_scan(x, bsz, t, s_re0, s_im0, lr, li, bcat, ccat, d_skip, tc, name):
    d = x.shape[1]
    nt = t // tc
    rows = bsz * tc
    lq = 1024

    def body(x_ref, sr0_ref, si0_ref, lr_ref, li_ref, b_ref, c_ref, d_ref, z_ref, sr_ref, si_ref, hre, him, xs, zs):
        i = pl.program_id(0)

        @pl.when(i == 0)
        def _():
            sr_ref[...] = sr0_ref[...]
            si_ref[...] = si0_ref[...]

        xs[...] = jnp.swapaxes(x_ref[...], 0, 1).reshape(rows, d)

        def x_tile(j):
            return xs[:, j * LANES:(j + 1) * LANES]

        for j in range(S5_NT):
            bu = _dot(x_tile(j), b_ref[j])
            hre[:, j * S5_HT:(j + 1) * S5_HT] = bu[:, :S5_HT]
            him[:, j * S5_HT:(j + 1) * S5_HT] = bu[:, S5_HT:]

        for q in range(S5_H // lq):
            ls = slice(q * lq, (q + 1) * lq)
            lam_r = jnp.broadcast_to(lr_ref[:, ls], (bsz, lq))
            lam_i = jnp.broadcast_to(li_ref[:, ls], (bsz, lq))

            def step(tt, carry):
                sr, si = carry
                idx = pl.ds(pl.multiple_of(tt * bsz, bsz), bsz)
                nr = lam_r * sr - lam_i * si + hre[idx, ls]
                ni = lam_r * si + lam_i * sr + him[idx, ls]
                hre[idx, ls] = nr
                him[idx, ls] = ni
                return nr, ni

            sr, si = lax.fori_loop(0, tc, step, (sr_ref[:, ls], si_ref[:, ls]), unroll=8)
            sr_ref[:, ls] = sr
            si_ref[:, ls] = si

        dsk = d_ref[...]
        for j in range(S5_NT):
            hs = slice(j * S5_HT, (j + 1) * S5_HT)
            cj = c_ref[j]
            y = _dot(hre[:, hs], cj[:S5_HT]) + _dot(him[:, hs], cj[S5_HT:])
            zs[:, j * LANES:(j + 1) * LANES] = _gelu(y + dsk[:, j * LANES:(j + 1) * LANES] * x_tile(j))
        z_ref[...] = jnp.swapaxes(zs[...].reshape(tc, bsz, d), 0, 1)

    fulls = (s_re0, s_im0, lr, li, bcat, ccat, d_skip.reshape(1, d))
    z, s_re, s_im = pl.pallas_call(
        body, grid=(nt,),
        in_specs=[pl.BlockSpec((bsz, tc, d), lambda i: (0, i, 0))] + [_full_spec(a) for a in fulls],
        out_specs=[pl.BlockSpec((bsz, tc, d), lambda i: (0, i, 0)),
                   pl.BlockSpec((bsz, S5_H), lambda i: (0, 0)), pl.BlockSpec((bsz, S5_H), lambda i: (0, 0))],
        out_shape=[jax.ShapeDtypeStruct((bsz, t, d), F32),
                   jax.ShapeDtypeStruct((bsz, S5_H), F32), jax.ShapeDtypeStruct((bsz, S5_H), F32)],
        scratch_shapes=[pltpu.VMEM((rows, S5_H), F32), pltpu.VMEM((rows, S5_H), F32),
                        pltpu.VMEM((rows, d), F32), pltpu.VMEM((rows, d), F32)],
        compiler_params=_cp(("arbitrary",)), name=name)(x.reshape(bsz, t, d), *fulls)
    return z.reshape(bsz * t, d), s_re, s_im


def _s5_mix(z, w_o, w_gate):
    def fn(z_t, wo_ref, wg_ref):
        zb = z_t.astype(BF16)
        return _dot(zb, wo_ref[...]) * _sigmoid(_dot(zb, wg_ref[...]))

    return Mix((z,), (w_o.astype(BF16), w_gate.astype(BF16)), fn)


def _s5_prompt(x, bsz, t, a_re, a_im, log_dt, b_re, b_im, c_re, c_im, d_skip, w_o, w_gate):
    lr, li, bcat, ccat = _s5_weights(a_re, a_im, log_dt, b_re, b_im, c_re, c_im)
    zero = jnp.zeros((bsz, S5_H), F32)
    z, s_re, s_im = _s5_scan(x, bsz, t, zero, zero, lr, li, bcat, ccat, d_skip, min(t, 64), "s5_scan")
    return _s5_mix(z, w_o, w_gate), s_re.reshape(bsz, S5_GROUPS, S5_STATE), s_im.reshape(bsz, S5_GROUPS, S5_STATE)


GLA_PROJ = 2 * GLA_QK + 2 * GLA_V


def _gla_proj(x, w_in, w_gk2, b_gk, tm, name):
    w1 = jnp.pad(w_in[:, GLA_PROJ:], ((0, 0), (0, LANES - GLA_RANK))).astype(BF16)
    w2 = jnp.pad(w_gk2, ((0, LANES - GLA_RANK), (0, 0)))

    def logd(xb, w1_ref, w2_ref, b_ref):
        y = _dot(_dot(xb, w1_ref[...]), w2_ref[...]) + b_ref[...]
        return -_softplus(-y) * (1.0 / GLA_GATE_NORM)

    return _mm(x, w_in, GLA_PROJ, tm, 512, name, logd, (w1, w2, b_gk.reshape(1, -1)), GLA_QK)


def _gla_chunks(proj, logd, norm_g, bsz, t, name):
    c = GLA_CHUNK
    rows = 64
    nr = t // rows
    scale = GLA_DK ** -0.5

    def body(q_ref, k_ref, v_ref, gate_ref, ld_ref, ng_ref, o_ref, s_out_ref, st_ref):
        i = pl.program_id(1)

        @pl.when(i == 0)
        def _():
            st_ref[...] = jnp.zeros_like(st_ref)

        lblk = _tri(rows, block=c).astype(F32)
        bc_all = _dot3(lblk, ld_ref[...])
        incl = _tri(c)
        ng = ng_ref[...]
        subs = range(rows // c)
        heads = range(GLA_HEADS)
        rsl = lambda s: slice(s * c, (s + 1) * c)
        ksl = lambda h: slice(h * GLA_DK, (h + 1) * GLA_DK)
        vsl = lambda h: slice(h * GLA_DV, (h + 1) * GLA_DV)
        q_ins, k_outs, g_ends, a_s = {}, {}, {}, {}
        for s in subs:
            for h in heads:
                bc = bc_all[rsl(s), ksl(h)]
                k = k_ref[rsl(s), ksl(h)]
                bcl = bc[c - 1:c]
                q_ins[s, h] = q_ref[rsl(s), ksl(h)] * scale * jnp.exp(bc)
                k_outs[s, h] = k * jnp.exp(bcl - bc)
                g_ends[s, h] = jnp.exp(bcl)
                a_s[s, h] = jnp.where(incl, _dot_nt(q_ins[s, h], k * jnp.exp(-bc)), 0.0)
        o_intra = {(s, h): _dot(a_s[s, h], v_ref[rsl(s), vsl(h)]) for s in subs for h in heads}
        for s in subs:
            sts = [st_ref[h] for h in heads]
            o_inter = [_dot_nt(q_ins[s, h], sts[h]) for h in heads]
            kvs = [_dot_tn(v_ref[rsl(s), vsl(h)], k_outs[s, h]) for h in heads]
            for h in heads:
                st_ref[h] = sts[h] * g_ends[s, h] + kvs[h]
                o = o_inter[h] + o_intra[s, h]
                o = o * lax.rsqrt(jnp.mean(o * o, -1, keepdims=True) + 1e-6) * ng
                o_ref[rsl(s), vsl(h)] = o * _silu(gate_ref[rsl(s), vsl(h)])

        @pl.when(i == nr - 1)
        def _():
            for h in range(GLA_HEADS):
                s_out_ref[0, h] = st_ref[h].T

    return pl.pallas_call(
        body, grid=(bsz, nr),
        in_specs=[pl.BlockSpec((rows, GLA_QK), lambda b, i: (b * nr + i, 0)),
                  pl.BlockSpec((rows, GLA_QK), lambda b, i: (b * nr + i, 1)),
                  pl.BlockSpec((rows, GLA_V), lambda b, i: (b * nr + i, 1)),
                  pl.BlockSpec((rows, GLA_V), lambda b, i: (b * nr + i, 2)),
                  pl.BlockSpec((rows, GLA_QK), lambda b, i: (b * nr + i, 0)),
                  pl.BlockSpec((1, GLA_DV), lambda b, i: (0, 0))],
        out_specs=[pl.BlockSpec((rows, GLA_V), lambda b, i: (b * nr + i, 0)),
                   pl.BlockSpec((1, GLA_HEADS, GLA_DK, GLA_DV), lambda b, i: (b, 0, 0, 0))],
        out_shape=[jax.ShapeDtypeStruct((bsz * t, GLA_V), F32),
                   jax.ShapeDtypeStruct((bsz, GLA_HEADS, GLA_DK, GLA_DV), F32)],
        scratch_shapes=[pltpu.VMEM((GLA_HEADS, GLA_DV, GLA_DK), F32)],
        compiler_params=_cp(("parallel", "arbitrary")), name=name,
    )(proj, proj, proj, proj, logd, norm_g.reshape(1, -1))


def _gla_prompt(x, bsz, t, w_in, w_gk2, b_gk, norm_g, w_out):
    proj, logd = _gla_proj(x, w_in, w_gk2, b_gk, min(bsz * t, 2048), "gla_proj")
    o, s = _gla_chunks(proj, logd, norm_g, bsz, t, "gla_chunks")
    return _proj_mix(o, w_out), s


def _rwkv_proj(x, xp, bsz, t, mu, w_rkv, w0, w_w1, w_w2, a0, w_a1, w_a2, w_g1, w_g2, tm, name):
    d = D_MODEL
    nt = t // tm
    carried = xp is None
    bf = lambda w: w.astype(BF16)

    def body(*refs):
        if carried:
            x_ref, refs, buf_ref = refs[0], refs[1:-1], refs[-1]
        else:
            x_ref, xp_ref, refs = refs[0], refs[1], refs[2:]
        mu_ref, wr_ref, w0_ref, ww1, ww2, a0_ref, wa1, wa2, wg1, wg2, r_ref, k_ref, v_ref, lw_ref, a_ref, g_ref = refs
        x = x_ref[...]
        if carried:
            @pl.when(pl.program_id(1) == 0)
            def _():
                buf_ref[pl.ds(0, SUBLANES)] = jnp.zeros((SUBLANES, d), F32)

            buf_ref[pl.ds(SUBLANES, tm)] = x
            x_prev = buf_ref[pl.ds(SUBLANES - 1, tm)]
            buf_ref[pl.ds(0, SUBLANES)] = x[tm - SUBLANES:]
        else:
            x_prev = xp_ref[...]
        dx = x_prev - x
        xs = lambda s: (x + dx * mu_ref[s:s + 1]).astype(BF16)
        for s, o_ref in enumerate((r_ref, k_ref, v_ref)):
            o_ref[...] = _dot(xs(s), wr_ref[s])
        lora = lambda h, w2_ref: _dot(h.astype(BF16), w2_ref[...])
        w_log = -_softplus(-(w0_ref[...] + lora(jnp.tanh(_dot(xs(3), ww1[...])), ww2))) - 0.5
        lw_ref[...] = -jnp.exp(w_log)
        a_ref[...] = _sigmoid(a0_ref[...] + lora(_dot(xs(4), wa1[...]), wa2))
        g_ref[...] = lora(_sigmoid(_dot(xs(5), wg1[...])), wg2)

    fulls = (mu, bf(w_rkv), w0.reshape(1, -1), bf(w_w1), bf(w_w2), a0.reshape(1, -1), bf(w_a1), bf(w_a2), bf(w_g1),
             bf(w_g2))
    tile = pl.BlockSpec((tm, d), lambda b, i: (b * nt + i, 0))
    return pl.pallas_call(
        body, grid=(bsz, nt),
        in_specs=[tile] * (1 if carried else 2) + [_resident_spec(a) for a in fulls],
        out_specs=[tile] * 6,
        out_shape=[jax.ShapeDtypeStruct((bsz * t, d), F32)] * 6,
        scratch_shapes=[pltpu.VMEM((SUBLANES + tm, d), F32)] if carried else [],
        compiler_params=_cp(("parallel", "arbitrary")), name=name,
    )(*((x,) if carried else (x, xp)), *fulls)


def _rwkv_head_inputs(r, k, v, a, kk_w, ka_w, sl):
    kraw = k[:, sl]
    kkn = _l2n(kraw * kk_w[:, sl], 1.0)
    ah = a[:, sl]
    kh = kraw * (1.0 + (ah - 1.0) * ka_w[:, sl])
    return r[:, sl], kh, v[:, sl], kkn, kkn * ah


def _rwkv_head_out(y, rh, kh, vh, g, rk_w, lng, lnb, sl):
    yc = y - jnp.mean(y, -1, keepdims=True)
    yn = yc * lax.rsqrt(jnp.mean(yc * yc, -1, keepdims=True) + RW_GN_EPS) * lng[:, sl] + lnb[:, sl]
    bonus = jnp.sum(rh * kh * rk_w[:, sl], -1, keepdims=True) * vh
    return (yn + bonus) * g[:, sl]


def _rwkv_chunks(r, k, v, lw, a, g, k_k, k_a, r_k, ln_g, ln_b, bsz, t, name):
    c = RW_CHUNK
    nc = t // c
    hd = RW_HD
    nb = 1
    d = D_MODEL

    def body(r_ref, k_ref, v_ref, lw_ref, a_ref, g_ref, kk_ref, ka_ref, rk_ref, lng_ref, lnb_ref,
             o_ref, s_out_ref, s_ref):
        i = pl.program_id(1)

        @pl.when(i == 0)
        def _():
            s_ref[...] = jnp.zeros_like(s_ref)

        kk_w, ka_w, rk_w, lng, lnb = kk_ref[...], ka_ref[...], rk_ref[...], lng_ref[...], lnb_ref[...]
        tri = _tri(c).astype(F32)
        strict = _tri(c, strict=True)
        incl = _tri(c)
        hsl = lambda n: slice(n * hd, (n + 1) * hd)
        inst = [(m, h) for m in range(nb) for h in range(RW_H)]
        prep = []
        for m in range(nb):
            lw = lw_ref[m]
            gam = _dot3(tri, lw)
            glast = gam[c - 1:c]
            r, k, a = r_ref[m], k_ref[m], a_ref[m]
            kk = k * kk_w
            scale = jnp.concatenate(
                [jnp.broadcast_to(lax.rsqrt(jnp.sum(kk[:, hsl(h)] * kk[:, hsl(h)], -1, keepdims=True) + 1e-6), (c, hd))
                 for h in range(RW_H)], axis=1)
            kkn = kk * scale
            kh = k * (1.0 + (a - 1.0) * ka_w)
            bh = kkn * a
            e_neg = jnp.exp(-gam)
            e_out = jnp.exp(glast - gam)
            prep.append(dict(a1=kkn * jnp.exp(gam - lw), r1=r * jnp.exp(gam), b1=bh * e_neg, k1=kh * e_neg,
                             b1o=bh * e_out, k1o=kh * e_out, g_end=jnp.exp(glast), r=r, kh=kh, v=v_ref[m],
                             g=g_ref[m]))
        ars = [jnp.concatenate([prep[m]["a1"][:, hsl(h)], prep[m]["r1"][:, hsl(h)]], axis=0) for m, h in inst]
        bks = [jnp.concatenate([prep[m]["b1"][:, hsl(h)], prep[m]["k1"][:, hsl(h)]], axis=0) for m, h in inst]
        vhs = [prep[m]["v"][:, hsl(h)] for m, h in inst]
        n = range(len(inst))
        gmats = [_dot_nt(ars[j], bks[j]) for j in n]
        tinvs = _neumann_inv([jnp.where(strict, gm_[:c, :c], 0.0) for gm_ in gmats], c)
        makvs = [_dot(jnp.where(strict, gmats[j][:c, c:], 0.0), vhs[j]) for j in n]
        rbks = [jnp.concatenate([jnp.where(incl, gm_[c:, :c], 0.0), jnp.where(incl, gm_[c:, c:], 0.0)], axis=1)
                for gm_ in gmats]
        a_ss = [_dot_nt(ars[j], s_ref[m, h]) for j, (m, h) in enumerate(inst)]
        uvs = [jnp.concatenate([_dot(tinvs[j], -a_ss[j][:c] - makvs[j]), vhs[j]], axis=0) for j in n]
        ys = [a_ss[j][c:] + _dot(rbks[j], uvs[j]) for j in n]
        svs = [_dot_tn(uvs[j], jnp.concatenate([prep[m]["b1o"][:, hsl(h)], prep[m]["k1o"][:, hsl(h)]], axis=0))
               for j, (m, h) in enumerate(inst)]
        for j, (m, h) in enumerate(inst):
            s_ref[m, h] = s_ref[m, h] * prep[m]["g_end"][:, hsl(h)] + svs[j]
        rkr = [prep[m]["r"] * prep[m]["kh"] * rk_w for m in range(nb)]
        sum1 = [jnp.sum(ys[j], -1, keepdims=True) for j in n]
        bons = [jnp.sum(rkr[m][:, hsl(h)], -1, keepdims=True) for m, h in inst]
        ycs = [ys[j] - sum1[j] * (1.0 / hd) for j in n]
        sum2 = [jnp.sum(ycs[j] * ycs[j], -1, keepdims=True) for j in n]
        for j, (m, h) in enumerate(inst):
            yn = ycs[j] * lax.rsqrt(sum2[j] * (1.0 / hd) + RW_GN_EPS) * lng[:, hsl(h)] + lnb[:, hsl(h)]
            o_ref[m, :, hsl(h)] = (yn + bons[j] * vhs[j]) * prep[m]["g"][:, hsl(h)]

        @pl.when(i == nc - 1)
        def _():
            s_out_ref[...] = s_ref[...]

    row = lambda w: w.reshape(1, d)
    v3 = lambda z: z.reshape(bsz, t, d)
    blk = pl.BlockSpec((nb, c, d), lambda b, i: (b, i, 0))
    par = pl.BlockSpec((1, d), lambda b, i: (0, 0))
    y, s = pl.pallas_call(
        body, grid=(bsz // nb, nc),
        in_specs=[blk] * 6 + [par] * 5,
        out_specs=[blk, pl.BlockSpec((nb, RW_H, hd, hd), lambda b, i: (b, 0, 0, 0))],
        out_shape=[jax.ShapeDtypeStruct((bsz, t, d), F32), jax.ShapeDtypeStruct((bsz, RW_H, hd, hd), F32)],
        scratch_shapes=[pltpu.VMEM((nb, RW_H, hd, hd), F32)],
        compiler_params=_cp(("parallel", "arbitrary")), name=name,
    )(v3(r), v3(k), v3(v), v3(lw), v3(a), v3(g), row(k_k), row(k_a), row(r_k), row(ln_g), row(ln_b))
    return y.reshape(bsz * t, d), s


def _rwkv_chunks_pairs(r, k, v, lw, a, g, k_k, k_a, r_k, ln_g, ln_b, bsz, t, name):
    c = RW_CHUNK
    nc = t // c
    hd = RW_HD
    d = D_MODEL
    npair = RW_H // 2

    def body(r_ref, k_ref, v_ref, lw_ref, a_ref, g_ref, kk_ref, ka_ref, rk_ref, lng_ref, lnb_ref,
             o_ref, s_out_ref, s_ref):
        i = pl.program_id(1)

        @pl.when(i == 0)
        def _():
            s_ref[...] = jnp.zeros_like(s_ref)

        kk_w, ka_w, rk_w, lng, lnb = kk_ref[...], ka_ref[...], rk_ref[...], lng_ref[...], lnb_ref[...]
        tri = _tri(c).astype(F32)
        strict = _tri(c, strict=True)
        incl = _tri(c)
        lo = _iota2((c, LANES), 1) < hd
        lo2 = _iota2((2 * c, LANES), 1) < hd
        bdiag = (_iota2((LANES, LANES), 0) < hd) == (_iota2((LANES, LANES), 1) < hd)
        psl = lambda p: slice(p * LANES, (p + 1) * LANES)
        pairs = range(npair)
        heads = [(p, hb) for p in pairs for hb in (0, 1)]
        own = lambda hb, x, m: jnp.where(m, x, 0.0) if hb == 0 else jnp.where(m, 0.0, x)

        def head_sums(x):
            return jnp.where(lo, jnp.sum(jnp.where(lo, x, 0.0), -1, keepdims=True),
                             jnp.sum(jnp.where(lo, 0.0, x), -1, keepdims=True))

        lw = lw_ref[0]
        gam = _dot3(tri, lw)
        glast = gam[c - 1:c]
        r, k, a, v, g = r_ref[0], k_ref[0], a_ref[0], v_ref[0], g_ref[0]
        kk = k * kk_w
        ssq = [head_sums(kk[:, psl(p)] * kk[:, psl(p)]) for p in pairs]
        kkn = kk * lax.rsqrt(jnp.concatenate(ssq, axis=1) + 1e-6)
        kh = k * (1.0 + (a - 1.0) * ka_w)
        bh = kkn * a
        e_neg = jnp.exp(-gam)
        e_out = jnp.exp(glast - gam)
        a1, r1, b1, k1 = kkn * jnp.exp(gam - lw), r * jnp.exp(gam), bh * e_neg, kh * e_neg
        b1o, k1o, g_end = bh * e_out, kh * e_out, jnp.exp(glast)
        ars = [jnp.concatenate([a1[:, psl(p)], r1[:, psl(p)]], axis=0) for p in pairs]
        bks = [jnp.concatenate([b1[:, psl(p)], k1[:, psl(p)]], axis=0) for p in pairs]
        bkos = [jnp.concatenate([b1o[:, psl(p)], k1o[:, psl(p)]], axis=0) for p in pairs]
        vps = [v[:, psl(p)] for p in pairs]
        gmats = [_dot_nt(ars[p], own(hb, bks[p], lo2)) for p, hb in heads]
        tinvs = _neumann_inv([jnp.where(strict, gm_[:c, :c], 0.0) for gm_ in gmats], c)
        makvs = [_dot(jnp.where(strict, gmats[j][:c, c:], 0.0), vps[p]) for j, (p, hb) in enumerate(heads)]
        rbks = [jnp.concatenate([jnp.where(incl, gm_[c:, :c], 0.0), jnp.where(incl, gm_[c:, c:], 0.0)], axis=1)
                for gm_ in gmats]
        a_ss = [_dot_nt(ars[p], s_ref[p]) for p in pairs]
        rhs = [-a_ss[p][:c] - jnp.where(lo, makvs[2 * p], makvs[2 * p + 1]) for p in pairs]
        us = [_dot(tinvs[j], rhs[p]) for j, (p, hb) in enumerate(heads)]
        uvs = [jnp.concatenate([jnp.where(lo, us[2 * p], us[2 * p + 1]), vps[p]], axis=0) for p in pairs]
        yhs = [_dot(rbks[j], uvs[p]) for j, (p, hb) in enumerate(heads)]
        ys = [a_ss[p][c:] + jnp.where(lo, yhs[2 * p], yhs[2 * p + 1]) for p in pairs]
        svs = [_dot_tn(uvs[p], bkos[p]) for p in pairs]
        for p in pairs:
            s_ref[p] = s_ref[p] * g_end[:, psl(p)] + jnp.where(bdiag, svs[p], 0.0)
        rkr = r * kh * rk_w
        mean = [head_sums(ys[p]) * (1.0 / hd) for p in pairs]
        bons = [head_sums(rkr[:, psl(p)]) for p in pairs]
        ycs = [ys[p] - mean[p] for p in pairs]
        var = [head_sums(ycs[p] * ycs[p]) * (1.0 / hd) for p in pairs]
        for p in pairs:
            yn = ycs[p] * lax.rsqrt(var[p] + RW_GN_EPS) * lng[:, psl(p)] + lnb[:, psl(p)]
            o_ref[0, :, psl(p)] = (yn + bons[p] * vps[p]) * g[:, psl(p)]

        @pl.when(i == nc - 1)
        def _():
            for p in pairs:
                blk = s_ref[p]
                s_out_ref[0, 2 * p] = blk[:hd, :hd]
                s_out_ref[0, 2 * p + 1] = blk[hd:, hd:]

    row = lambda w: w.reshape(1, d)
    v3 = lambda z: z.reshape(bsz, t, d)
    blk = pl.BlockSpec((1, c, d), lambda b, i: (b, i, 0))
    par = pl.BlockSpec((1, d), lambda b, i: (0, 0))
    y, s = pl.pallas_call(
        body, grid=(bsz, nc),
        in_specs=[blk] * 6 + [par] * 5,
        out_specs=[blk, pl.BlockSpec((1, RW_H, hd, hd), lambda b, i: (b, 0, 0, 0))],
        out_shape=[jax.ShapeDtypeStruct((bsz, t, d), F32), jax.ShapeDtypeStruct((bsz, RW_H, hd, hd), F32)],
        scratch_shapes=[pltpu.VMEM((npair, LANES, LANES), F32)],
        compiler_params=_cp(("parallel", "arbitrary")), name=name,
    )(v3(r), v3(k), v3(v), v3(lw), v3(a), v3(g), row(k_k), row(k_a), row(r_k), row(ln_g), row(ln_b))
    return y.reshape(bsz * t, d), s


def _rwkv_prompt(x, bsz, t, mu, w_rkv, w0, w_w1, w_w2, a0, w_a1, w_a2, w_g1, w_g2, k_k, k_a, r_k, gn_g, gn_b, w_o):
    r, k, v, lw, a, g = _rwkv_proj(x, None, bsz, t, mu, w_rkv, w0, w_w1, w_w2, a0, w_a1, w_a2, w_g1, w_g2,
                                   min(t, 256), "rwkv_proj")
    y, s = _rwkv_chunks_pairs(r, k, v, lw, a, g, k_k, k_a, r_k, gn_g, gn_b, bsz, t, "rwkv_chunks")
    return _proj_mix(y, w_o), x.reshape(bsz, t, D_MODEL)[:, -1], s


def _eye(n):
    return _iota2((n, n), 0) == _iota2((n, n), 1)


def _to_col(row, eye):
    return jnp.sum(jnp.where(eye, row, 0.0), axis=1, keepdims=True)


def _to_row(col, eye):
    return jnp.sum(jnp.where(eye, col, 0.0), axis=0, keepdims=True)


def _row3(a):
    return a.reshape(a.shape[0], 1, a.shape[1])


def _rows_spec(width, col_block=0, nb=1):
    return pl.BlockSpec((nb, 1, width), lambda b: (b, 0, col_block))


STEP_ROWS = 4


def _gdn_conv_step(proj, buf, conv_w, name):
    def body(p_ref, b0, b1, b2, w_ref, o_ref):
        w = w_ref[...]
        y = p_ref[:, :GDN_CONV_DIM] * w[3:4] + b2[...] * w[2:3] + b1[...] * w[1:2] + b0[...] * w[0:1]
        _gdn_act(y, o_ref)

    rows = [proj, buf[:, 0], buf[:, 1], buf[:, 2]]
    return _rowwise(body, rows, [conv_w], [GDN_CONV_DIM], proj.shape[0], name)[0]


def _gdn_step(qkv, bg, proj, s0, norm_g, name):
    bsz = qkv.shape[0]
    rep = GDN_VH // GDN_KH
    nb = STEP_ROWS if bsz % STEP_ROWS == 0 else 1

    def body(q_ref, k_ref, v_ref, bg_ref, z_ref, ng_ref, s_ref, o_ref, so_ref):
        eye = _eye(GDN_HD)
        ng = ng_ref[...]
        hsl = lambda n: slice(n * GDN_HD, (n + 1) * GDN_HD)
        jk = [(j, kh) for j in range(nb) for kh in range(GDN_KH)]
        jh = [(j, h) for j in range(nb) for h in range(GDN_VH)]
        qrow = {(j, kh): q_ref[j, :, hsl(kh)] for j, kh in jk}
        krow = {(j, kh): k_ref[j, :, hsl(kh)] for j, kh in jk}
        qcol = {i: _to_col(qrow[i], eye) for i in jk}
        kcol = {i: _to_col(krow[i], eye) for i in jk}
        qk = {i: jnp.sum(qrow[i] * krow[i], axis=1, keepdims=True) for i in jk}
        eg = {(j, h): jnp.exp(bg_ref[j, :, GDN_VH + h:GDN_VH + h + 1]) for j, h in jh}
        ks = {(j, h): jnp.sum(kcol[j, h // rep] * s_ref[j, h], axis=0, keepdims=True) for j, h in jh}
        qs = {(j, h): jnp.sum(qcol[j, h // rep] * s_ref[j, h], axis=0, keepdims=True) for j, h in jh}
        v_new = {(j, h): bg_ref[j, :, h:h + 1] * (v_ref[j, :, hsl(h)] - eg[j, h] * ks[j, h]) for j, h in jh}
        o = {(j, h): eg[j, h] * qs[j, h] + qk[j, h // rep] * v_new[j, h] for j, h in jh}
        ms = {i: jnp.mean(o[i] * o[i], -1, keepdims=True) for i in jh}
        for j, h in jh:
            so_ref[j, h] = s_ref[j, h] * eg[j, h] + kcol[j, h // rep] * v_new[j, h]
            o_ref[j, :, hsl(h)] = o[j, h] * lax.rsqrt(ms[j, h] + 1e-6) * ng * _silu(z_ref[j, :, hsl(h)])

    st_spec = pl.BlockSpec((nb, GDN_VH, GDN_HD, GDN_HD), lambda b: (b, 0, 0, 0))
    o, s = pl.pallas_call(
        body, grid=(bsz // nb,),
        in_specs=[_rows_spec(GDN_QK, 0, nb), _rows_spec(GDN_QK, 1, nb), _rows_spec(GDN_V, 1, nb),
                  _rows_spec(LANES, 0, nb), _rows_spec(GDN_V, GDN_CONV_DIM // GDN_V, nb),
                  pl.BlockSpec((1, GDN_HD), lambda b: (0, 0)), st_spec],
        out_specs=[_rows_spec(GDN_V, 0, nb), st_spec],
        out_shape=[jax.ShapeDtypeStruct((bsz, 1, GDN_V), F32), jax.ShapeDtypeStruct(s0.shape, F32)],
        compiler_params=_cp(("parallel",)), name=name,
    )(_row3(qkv), _row3(qkv), _row3(qkv), _row3(bg), _row3(proj), norm_g.reshape(1, -1), s0)
    return o.reshape(bsz, GDN_V), s


def _gdn_sample(x, buf, s0, w_in, conv_w, a_log, dt_bias, norm_g, w_out):
    proj, bg = _gdn_proj(x, w_in, a_log, dt_bias, x.shape[0], "gdn_proj_s")
    qkv = _gdn_conv_step(proj, buf, conv_w, "gdn_conv_s")
    o, s = _gdn_step(qkv, bg, proj, s0, norm_g, "gdn_step_s")
    new_buf = jnp.concatenate([buf[:, 1:], proj[:, None, :GDN_CONV_DIM]], axis=1)
    return _proj_mix(o, w_out), new_buf, s


def _s5_step(x, s_re, s_im, lr, li, bcat, ccat, d_skip, name):
    def body(x_ref, sr_ref, si_ref, lr_ref, li_ref, b_ref, c_ref, d_ref, z_ref, hr_ref, hi_ref):
        x = x_ref[...]
        dsk = d_ref[...]
        for j in range(S5_NT):
            hs = slice(j * S5_HT, (j + 1) * S5_HT)
            xs = x[:, j * LANES:(j + 1) * LANES]
            bu = _dot(xs, b_ref[j])
            lam_r, lam_i = lr_ref[:, hs], li_ref[:, hs]
            sr, si = sr_ref[:, hs], si_ref[:, hs]
            h_re = lam_r * sr - lam_i * si + bu[:, :S5_HT]
            h_im = lam_r * si + lam_i * sr + bu[:, S5_HT:]
            hr_ref[:, hs] = h_re
            hi_ref[:, hs] = h_im
            cj = c_ref[j]
            y = _dot(h_re, cj[:S5_HT]) + _dot(h_im, cj[S5_HT:])
            z_ref[:, j * LANES:(j + 1) * LANES] = _gelu(y + dsk[:, j * LANES:(j + 1) * LANES] * xs)

    fulls = [lr, li, bcat, ccat, d_skip.reshape(1, -1)]
    return _rowwise(body, [x, s_re, s_im], fulls, [D_MODEL, S5_H, S5_H], x.shape[0], name)


def _s5_sample(x, s_re0, s_im0, a_re, a_im, log_dt, b_re, b_im, c_re, c_im, d_skip, w_o, w_gate):
    m = x.shape[0]
    lr, li, bcat, ccat = _s5_weights(a_re, a_im, log_dt, b_re, b_im, c_re, c_im)
    z, h_re, h_im = _s5_step(x, s_re0.reshape(m, S5_H), s_im0.reshape(m, S5_H), lr, li, bcat, ccat, d_skip, "s5_step_s")
    return _s5_mix(z, w_o, w_gate), h_re.reshape(m, S5_GROUPS, S5_STATE), h_im.reshape(m, S5_GROUPS, S5_STATE)


def _gla_step(proj, logd, s0, norm_g, name):
    bsz = proj.shape[0]
    scale = GLA_DK ** -0.5
    nb = STEP_ROWS if bsz % STEP_ROWS == 0 else 1

    def body(q_ref, k_ref, v_ref, gate_ref, ld_ref, ng_ref, s_ref, o_ref, so_ref):
        eye = _eye(GLA_DK)
        ng = ng_ref[...]
        ksl = lambda h: slice(h * GLA_DK, (h + 1) * GLA_DK)
        vsl = lambda h: slice(h * GLA_DV, (h + 1) * GLA_DV)
        jh = [(j, h) for j in range(nb) for h in range(GLA_HEADS)]
        bc = {(j, h): ld_ref[j, :, ksl(h)] for j, h in jh}
        k = {(j, h): k_ref[j, :, ksl(h)] for j, h in jh}
        q_in = {(j, h): q_ref[j, :, ksl(h)] * scale * jnp.exp(bc[j, h]) for j, h in jh}
        a = {i: jnp.sum(q_in[i] * (k[i] * jnp.exp(-bc[i])), axis=1, keepdims=True) for i in jh}
        qcol = {i: _to_col(q_in[i], eye) for i in jh}
        gcol = {i: _to_col(jnp.exp(bc[i]), eye) for i in jh}
        kcol = {i: _to_col(k[i], eye) for i in jh}
        o = {(j, h): jnp.sum(qcol[j, h] * s_ref[j, h], axis=0, keepdims=True) + a[j, h] * v_ref[j, :, vsl(h)]
             for j, h in jh}
        ms = {i: jnp.mean(o[i] * o[i], -1, keepdims=True) for i in jh}
        for j, h in jh:
            so_ref[j, h] = s_ref[j, h] * gcol[j, h] + kcol[j, h] * v_ref[j, :, vsl(h)]
            o_ref[j, :, vsl(h)] = o[j, h] * lax.rsqrt(ms[j, h] + 1e-6) * ng * _silu(gate_ref[j, :, vsl(h)])

    st_spec = pl.BlockSpec((nb, GLA_HEADS, GLA_DK, GLA_DV), lambda b: (b, 0, 0, 0))
    o, s = pl.pallas_call(
        body, grid=(bsz // nb,),
        in_specs=[_rows_spec(GLA_QK, 0, nb), _rows_spec(GLA_QK, 1, nb), _rows_spec(GLA_V, 1, nb),
                  _rows_spec(GLA_V, 2, nb), _rows_spec(GLA_QK, 0, nb), pl.BlockSpec((1, GLA_DV), lambda b: (0, 0)),
                  st_spec],
        out_specs=[_rows_spec(GLA_V, 0, nb), st_spec],
        out_shape=[jax.ShapeDtypeStruct((bsz, 1, GLA_V), F32), jax.ShapeDtypeStruct(s0.shape, F32)],
        compiler_params=_cp(("parallel",)), name=name,
    )(_row3(proj), _row3(proj), _row3(proj), _row3(proj), _row3(logd), norm_g.reshape(1, -1), s0)
    return o.reshape(bsz, GLA_V), s


def _gla_sample(x, s0, w_in, w_gk2, b_gk, norm_g, w_out):
    proj, logd = _gla_proj(x, w_in, w_gk2, b_gk, x.shape[0], "gla_proj_s")
    o, s = _gla_step(proj, logd, s0, norm_g, "gla_step_s")
    return _proj_mix(o, w_out), s


def _dot3_r(x, l):
    hi = x.astype(BF16).astype(F32)
    r1 = x - hi
    mid = r1.astype(BF16).astype(F32)
    lo = r1 - mid
    return _dot(hi, l) + _dot(mid, l) + _dot(lo, l)


def _head_ones():
    return jnp.kron(jnp.eye(RW_H, dtype=F32), jnp.ones((RW_HD, RW_HD), F32))


def _rwkv_step_prep(r, k, v, lw, a, k_k, k_a, name):
    bsz = r.shape[0]

    def body(r_ref, k_ref, v_ref, lw_ref, a_ref, kk_ref, ka_ref, ones_ref, rt, kt, vt, kkt, bt, dt, kh_ref):
        k, a = k_ref[...], a_ref[...]
        kk = k * kk_ref[...]
        kkn = kk * lax.rsqrt(_dot3_r(kk * kk, ones_ref[...]) + 1e-6)
        kh = k * (1.0 + (a - 1.0) * ka_ref[...])
        kh_ref[...] = kh
        rt[...] = r_ref[...].T
        kt[...] = kh.T
        vt[...] = v_ref[...].T
        kkt[...] = kkn.T
        bt[...] = (kkn * a).T
        dt[...] = jnp.exp(lw_ref[...]).T

    args = (r, k, v, lw, a, k_k.reshape(1, -1), k_a.reshape(1, -1), _head_ones())
    tshape = jax.ShapeDtypeStruct((D_MODEL, bsz), F32)
    return pl.pallas_call(
        body, grid=(1,), in_specs=[_full_spec(t) for t in args],
        out_specs=[pl.BlockSpec((D_MODEL, bsz), lambda i: (0, 0))] * 6 + [pl.BlockSpec((bsz, D_MODEL), lambda i: (0, 0))],
        out_shape=[tshape] * 6 + [jax.ShapeDtypeStruct((bsz, D_MODEL), F32)],
        compiler_params=_cp(("arbitrary",)), name=name)(*args)


def _rwkv_step_lanes(rt, kt, vt, kkt, bt, dt, s_t, name):
    hd = RW_HD
    bsz = s_t.shape[-1]

    def body(rt_ref, kt_ref, vt_ref, kkt_ref, bt_ref, dt_ref, s_ref, y_ref, so_ref):
        hs = pl.ds(pl.multiple_of(pl.program_id(0) * hd, hd), hd)
        r_h, k_h, v_h, kk_h, b_h, d_h = (ref[hs, :] for ref in (rt_ref, kt_ref, vt_ref, kkt_ref, bt_ref, dt_ref))
        ys = []
        for vi in range(hd):
            s = s_ref[0, vi]
            sa = -jnp.sum(s * kk_h, axis=0, keepdims=True)
            s_new = s * d_h + sa * b_h + v_h[vi:vi + 1] * k_h
            so_ref[0, vi] = s_new
            ys.append(jnp.sum(s_new * r_h, axis=0, keepdims=True))
        y_ref[...] = jnp.concatenate(ys, axis=0)

    vec = pl.BlockSpec((D_MODEL, bsz), lambda h: (0, 0))
    st = pl.BlockSpec((1, hd, hd, bsz), lambda h: (h, 0, 0, 0))
    return pl.pallas_call(
        body, grid=(RW_H,), in_specs=[vec] * 6 + [st],
        out_specs=[pl.BlockSpec((hd, bsz), lambda h: (h, 0)), st],
        out_shape=[jax.ShapeDtypeStruct((D_MODEL, bsz), F32), jax.ShapeDtypeStruct(s_t.shape, F32)],
        compiler_params=_cp(("parallel",)), name=name)(rt, kt, vt, kkt, bt, dt, s_t)


def _rwkv_step_out(yt, r, kh, v, g, r_k, gn_g, gn_b, name):
    def body(yt_ref, r_ref, k_ref, v_ref, g_ref, rk_ref, gg_ref, gb_ref, ones_ref, o_ref):
        ones = ones_ref[...]
        y = yt_ref[...].T
        yc = y - _dot3_r(y, ones) * (1.0 / RW_HD)
        yn = yc * lax.rsqrt(_dot3_r(yc * yc, ones) * (1.0 / RW_HD) + RW_GN_EPS) * gg_ref[...] + gb_ref[...]
        bonus = _dot3_r(r_ref[...] * k_ref[...] * rk_ref[...], ones) * v_ref[...]
        o_ref[...] = (yn + bonus) * g_ref[...]

    row = lambda w: w.reshape(1, D_MODEL)
    args = (yt, r, kh, v, g, row(r_k), row(gn_g), row(gn_b), _head_ones())
    return pl.pallas_call(
        body, grid=(1,), in_specs=[_full_spec(t) for t in args],
        out_specs=pl.BlockSpec(r.shape, lambda i: (0, 0)),
        out_shape=jax.ShapeDtypeStruct(r.shape, F32),
        compiler_params=_cp(("arbitrary",)), name=name)(*args)


def _rwkv_sample(x, shift0, s0, mu, w_rkv, w0, w_w1, w_w2, a0, w_a1, w_a2, w_g1, w_g2, k_k, k_a, r_k, gn_g, gn_b, w_o):
    m = x.shape[0]
    r, k, v, lw, a, g = _rwkv_proj(x, shift0, 1, m, mu, w_rkv, w0, w_w1, w_w2, a0, w_a1, w_a2, w_g1, w_g2, m,
                                   "rwkv_proj_s")
    rt, kt, vt, kkt, bt, dt, kh = _rwkv_step_prep(r, k, v, lw, a, k_k, k_a, "rwkv_prep_s")
    yt, s_t = _rwkv_step_lanes(rt, kt, vt, kkt, bt, dt, jnp.transpose(s0, (1, 2, 3, 0)), "rwkv_step_s")
    o = _rwkv_step_out(yt, r, kh, v, g, r_k, gn_g, gn_b, "rwkv_out_s")
    return _proj_mix(o, w_o), x, jnp.transpose(s_t, (3, 0, 1, 2))


def kernel(x_prompt, x_sample, state_gdn_conv, state_gdn, state_s5_re, state_s5_im, state_gla, state_rwkv_shift,
           state_rwkv, p_prompt, p_sample, gdn_w_in, gdn_conv_w, gdn_a_log, gdn_dt_bias, gdn_norm_g, gdn_w_out,
           s5_a_re, s5_a_im, s5_log_dt, s5_b_re, s5_b_im, s5_c_re, s5_c_im, s5_d, s5_w_o, s5_w_gate,
           gla_w_in, gla_w_gk2, gla_b_gk, gla_norm_g, gla_w_out,
           rwkv_mu, rwkv_w_rkv, rwkv_w0, rwkv_w_w1, rwkv_w_w2, rwkv_a0, rwkv_w_a1, rwkv_w_a2, rwkv_w_g1, rwkv_w_g2,
           rwkv_k_k, rwkv_k_a, rwkv_r_k, rwkv_ln_g, rwkv_ln_b, rwkv_w_o,
           ln_mix_g, ln_mix_b, ln_ffn_g, ln_ffn_b, mlp_w1, mlp_w2, ple_w, ple_gate_w):
    bsz, t, d = x_prompt.shape
    bs = x_sample.shape[0]
    gdn_w = (gdn_w_in[0], gdn_conv_w[0], gdn_a_log[0], gdn_dt_bias[0], gdn_norm_g[0], gdn_w_out[0])
    s5_w = (s5_a_re[0], s5_a_im[0], s5_log_dt[0], s5_b_re[0], s5_b_im[0], s5_c_re[0], s5_c_im[0], s5_d[0],
            s5_w_o[0], s5_w_gate[0])
    gla_w = (gla_w_in[0], gla_w_gk2[0], gla_b_gk[0], gla_norm_g[0], gla_w_out[0])
    rwkv_w = (rwkv_mu[0], rwkv_w_rkv[0], rwkv_w0[0], rwkv_w_w1[0], rwkv_w_w2[0], rwkv_a0[0], rwkv_w_a1[0],
              rwkv_w_a2[0], rwkv_w_g1[0], rwkv_w_g2[0], rwkv_k_k[0], rwkv_k_a[0], rwkv_r_k[0], rwkv_ln_g[0],
              rwkv_ln_b[0], rwkv_w_o[0])

    def tail(x, mix, p_all, i, tm, tag):
        return _block_tail(x, mix, ln_mix_g[i], ln_mix_b[i], p_all, i, mlp_w1[i], mlp_w2[i], ln_ffn_g[i], ln_ffn_b[i],
                           ple_w[i], ple_gate_w[i], tm, 1024, f"tail{i}_{tag}")

    xp = x_prompt.reshape(bsz * t, d)
    pp = p_prompt.reshape(DEPTH, bsz * t, D_PLE)
    tm_p = min(bsz * t, 512)
    mix, gc_p, gs_p = _gdn_prompt(xp, bsz, t, *gdn_w)
    xp = tail(xp, mix, pp, 0, tm_p, "p")
    mix, sr_p, si_p = _s5_prompt(xp, bsz, t, *s5_w)
    xp = tail(xp, mix, pp, 1, tm_p, "p")
    mix, la_p = _gla_prompt(xp, bsz, t, *gla_w)
    xp = tail(xp, mix, pp, 2, tm_p, "p")
    mix, sh_p, rs_p = _rwkv_prompt(xp, bsz, t, *rwkv_w)
    xp = tail(xp, mix, pp, 3, tm_p, "p")

    xs = x_sample.reshape(bs, d)
    ps = p_sample.reshape(DEPTH, bs, D_PLE)
    mix, gc_s, gs_s = _gdn_sample(xs, state_gdn_conv[0], state_gdn[0], *gdn_w)
    xs = tail(xs, mix, ps, 0, bs, "s")
    mix, sr_s, si_s = _s5_sample(xs, state_s5_re[0], state_s5_im[0], *s5_w)
    xs = tail(xs, mix, ps, 1, bs, "s")
    mix, la_s = _gla_sample(xs, state_gla[0], *gla_w)
    xs = tail(xs, mix, ps, 2, bs, "s")
    mix, sh_s, rs_s = _rwkv_sample(xs, state_rwkv_shift[0], state_rwkv[0], *rwkv_w)
    xs = tail(xs, mix, ps, 3, bs, "s")

    e = lambda a: a[None]
    return (xp.reshape(bsz, t, d), xs.reshape(bs, 1, d), e(gc_p), e(gc_s), e(gs_p), e(gs_s), e(sr_p), e(sr_s),
            e(si_p), e(si_s), e(la_p), e(la_s), e(sh_p), e(sh_s), e(rs_p), e(rs_s))
```

```python
import math
from typing import Callable, NamedTuple

import jax
import jax.numpy as jnp
from jax import lax
from jax.experimental import pallas as pl
from jax.experimental.pallas import tpu as pltpu

F32 = jnp.float32
BF16 = jnp.bfloat16

D_MODEL = 1024
DEPTH = 4
D_PLE = 256
D_FF = 4 * D_MODEL
LN_EPS = 1e-5
ALPHA = (2.0 * DEPTH) ** 0.25

GDN_HD = 128
GDN_KH = 8
GDN_VH = 16
GDN_QK = GDN_KH * GDN_HD
GDN_V = GDN_VH * GDN_HD
GDN_CONV_DIM = 2 * GDN_QK + GDN_V
GDN_CONV_W = 4
GDN_CHUNK = 64

S5_GROUP = 16
S5_GROUPS = D_MODEL // S5_GROUP
S5_STATE = 64
S5_H = S5_GROUPS * S5_STATE

GLA_HEADS = 4
GLA_DK = 128
GLA_DV = 256
GLA_QK = GLA_HEADS * GLA_DK
GLA_V = GLA_HEADS * GLA_DV
GLA_RANK = 16
GLA_GATE_NORM = 16.0
GLA_CHUNK = 16

RW_HD = 64
RW_H = D_MODEL // RW_HD
RW_GN_EPS = 64e-5
RW_CHUNK = 64

LANES = 128
SUBLANES = 8
VMEM_LIMIT = 56 * 1024 * 1024


def _cp(sem, vmem=VMEM_LIMIT):
    return pltpu.CompilerParams(dimension_semantics=sem, vmem_limit_bytes=vmem)


def _dot(a, b):
    return jnp.dot(a, b, preferred_element_type=F32)


def _dot_nt(a, b):
    return lax.dot_general(a, b, (((1,), (1,)), ((), ())), preferred_element_type=F32)


def _dot_tn(a, b):
    return lax.dot_general(a, b, (((0,), (0,)), ((), ())), preferred_element_type=F32)


def _dot3(l, x):
    hi = x.astype(BF16).astype(F32)
    r1 = x - hi
    mid = r1.astype(BF16).astype(F32)
    lo = r1 - mid
    return _dot(l, hi) + _dot(l, mid) + _dot(l, lo)


def _softplus(x):
    return jnp.maximum(x, 0.0) + jnp.log(1.0 + jnp.exp(-jnp.abs(x)))


def _sigmoid(x):
    return 1.0 / (1.0 + jnp.exp(-x))


def _silu(x):
    return x * _sigmoid(x)


def _ln(x, g, b):
    xc = x - jnp.mean(x, -1, keepdims=True)
    var = jnp.mean(xc * xc, -1, keepdims=True)
    return xc * lax.rsqrt(var + LN_EPS) * g + b


def _iota2(shape, axis):
    return lax.broadcasted_iota(jnp.int32, shape, axis)


def _tri(n, strict=False, block=None):
    r = _iota2((n, n), 0)
    c = _iota2((n, n), 1)
    m = (r > c) if strict else (r >= c)
    if block is not None:
        m = m & ((r // block) == (c // block))
    return m


def _full_spec(a):
    n = a.ndim
    return pl.BlockSpec(a.shape, lambda *_: (0,) * n)


def _mm(x, w, n_cols, tm, tn, name, side_fn, side_fulls, side_cols):
    m, k = x.shape

    def body(x_ref, w_ref, *rest):
        side_refs, o_ref, s_ref, xb_ref = rest[:-3], rest[-3], rest[-2], rest[-1]

        @pl.when(pl.program_id(1) == 0)
        def _():
            xb_ref[...] = x_ref[...].astype(BF16)
            s_ref[...] = side_fn(xb_ref[...], *side_refs)

        o_ref[...] = _dot(xb_ref[...], w_ref[...])

    return pl.pallas_call(
        body, grid=(m // tm, n_cols // tn),
        in_specs=[pl.BlockSpec((tm, k), lambda i, j: (i, 0)), pl.BlockSpec((k, tn), lambda i, j: (0, j))]
        + [_full_spec(a) for a in side_fulls],
        out_specs=[pl.BlockSpec((tm, tn), lambda i, j: (i, j)), pl.BlockSpec((tm, side_cols), lambda i, j: (i, 0))],
        out_shape=[jax.ShapeDtypeStruct((m, n_cols), F32), jax.ShapeDtypeStruct((m, side_cols), F32)],
        scratch_shapes=[pltpu.VMEM((tm, k), BF16)],
        compiler_params=_cp(("parallel", "arbitrary")), name=name)(x, w.astype(BF16), *side_fulls)


def _rowwise(body, rows, fulls, out_cols, tm, name):
    m = rows[0].shape[0]
    in_specs = [pl.BlockSpec((tm, r.shape[1]), lambda i: (i, 0)) for r in rows] + [_full_spec(f) for f in fulls]
    out_specs = [pl.BlockSpec((tm, c), lambda i: (i, 0)) for c in out_cols]
    out_shape = [jax.ShapeDtypeStruct((m, c), F32) for c in out_cols]
    res = pl.pallas_call(body, grid=(m // tm,), in_specs=in_specs, out_specs=out_specs, out_shape=out_shape,
                         compiler_params=_cp(("parallel",)), name=name)(*rows, *fulls)
    return res


def _resident_spec(a):
    n = a.ndim
    return pl.BlockSpec(a.shape, lambda *_: (0,) * n, pipeline_mode=pl.Buffered(1))


class Mix(NamedTuple):
    rows: tuple
    fulls: tuple
    fn: Callable


def _proj_mix(o, w_out):
    return Mix((o,), (w_out.astype(BF16),), lambda o_t, w_ref: _dot(o_t.astype(BF16), w_ref[...]))


def _block_tail(x, mix, lm_g, lm_b, p_all, layer, w1, w2, g, b, ple_w, gate_w, tm, tf, name):
    m = x.shape[0]
    nf = D_FF // tf
    nrow, nfull = len(mix.rows), len(mix.fulls)

    def body(*refs):
        x_ref, p_ref = refs[0], refs[1 + nrow]
        row_refs = refs[1:1 + nrow]
        full_refs = refs[2 + nrow:2 + nrow + nfull]
        lmg_ref, lmb_ref, w1_ref, w2_ref, g_ref, b_ref, pw_ref, gw_ref, o_ref, h_ref = refs[2 + nrow + nfull:]
        x1 = _ln(ALPHA * x_ref[...] + mix.fn(*[r[...] for r in row_refs], *full_refs), lmg_ref[...], lmb_ref[...])
        xb = x1.astype(BF16)
        for f in range(nf):
            h = jnp.maximum(_dot(xb, w1_ref[:, f * tf:(f + 1) * tf]), 0.0)
            h_ref[:, f * tf:(f + 1) * tf] = (h * h).astype(BF16)
        x2 = _ln(ALPHA * x1 + _dot(h_ref[...], w2_ref[...]), g_ref[...], b_ref[...])
        gate = _sigmoid(_dot(x2.astype(BF16), gw_ref[...]))
        o_ref[...] = x2 + _dot(p_ref[...].astype(BF16), pw_ref[...]) * gate

    row = lambda v: v.reshape(1, -1)
    fulls = mix.fulls + (row(lm_g), row(lm_b), w1.astype(BF16), w2.astype(BF16), row(g), row(b), ple_w.astype(BF16),
                         gate_w.astype(BF16))
    return pl.pallas_call(
        body, grid=(m // tm,),
        in_specs=[pl.BlockSpec((tm, D_MODEL), lambda i: (i, 0))]
        + [pl.BlockSpec((tm, r.shape[1]), lambda i: (i, 0)) for r in mix.rows]
        + [pl.BlockSpec((None, tm, D_PLE), lambda i: (layer, i, 0))]
        + [_resident_spec(a) for a in fulls],
        out_specs=pl.BlockSpec((tm, D_MODEL), lambda i: (i, 0)),
        out_shape=jax.ShapeDtypeStruct((m, D_MODEL), F32),
        scratch_shapes=[pltpu.VMEM((tm, D_FF), BF16)],
        compiler_params=_cp(("parallel",)), name=name,
    )(x, *mix.rows, p_all, *fulls)


def _gdn_proj(x, w_in, a_log, dt_bias, tm, name):
    w_ba = jnp.pad(w_in[:, GDN_CONV_DIM + GDN_V:], ((0, 0), (0, LANES - 2 * GDN_VH))).astype(BF16)
    pad = lambda v: jnp.pad(v.reshape(1, -1), ((0, 0), (GDN_VH, LANES - 2 * GDN_VH)))

    def gates(xb, w_ref, al_ref, dt_ref):
        y = _dot(xb, w_ref[...])
        lane = _iota2(y.shape, 1)
        g = -jnp.exp(al_ref[...]) * _softplus(y + dt_ref[...])
        return jnp.where(lane < GDN_VH, _sigmoid(y), g)

    return _mm(x, w_in, GDN_CONV_DIM + GDN_V, tm, 512, name, gates, (w_ba, pad(a_log), pad(dt_bias)), LANES)


def _l2n(x, scale):
    return x * (lax.rsqrt(jnp.sum(x * x, -1, keepdims=True) + 1e-6) * scale)


def _gdn_act(y, o_ref):
    y = _silu(y)
    for h in range(GDN_KH):
        sl = slice(h * GDN_HD, (h + 1) * GDN_HD)
        o_ref[:, sl] = _l2n(y[:, sl], GDN_HD ** -0.5)
        sl = slice(GDN_QK + h * GDN_HD, GDN_QK + (h + 1) * GDN_HD)
        o_ref[:, sl] = _l2n(y[:, sl], 1.0)
    o_ref[:, 2 * GDN_QK:] = y[:, 2 * GDN_QK:]


def _gdn_conv_prompt(proj, conv_w, bsz, t, tt, name):
    nt = t // tt
    c = GDN_CONV_DIM

    def body(x_ref, w_ref, o_ref, tail_ref, buf_ref):
        i = pl.program_id(1)

        @pl.when(i == 0)
        def _():
            buf_ref[pl.ds(0, SUBLANES)] = jnp.zeros((SUBLANES, c), F32)

        x = x_ref[...]
        w = w_ref[...]
        buf_ref[pl.ds(SUBLANES, tt)] = x
        y = x * w[3:4]
        for j in range(1, GDN_CONV_W):
            y = y + buf_ref[pl.ds(SUBLANES - j, tt)] * w[3 - j:4 - j]
        _gdn_act(y, o_ref)
        buf_ref[pl.ds(0, SUBLANES)] = x[tt - SUBLANES:]
        tail_ref[0] = x[tt - SUBLANES:]

    return pl.pallas_call(
        body, grid=(bsz, nt),
        in_specs=[pl.BlockSpec((tt, c), lambda b, i: (b * nt + i, 0)), _full_spec(conv_w)],
        out_specs=[pl.BlockSpec((tt, c), lambda b, i: (b * nt + i, 0)),
                   pl.BlockSpec((1, SUBLANES, c), lambda b, i: (b, 0, 0))],
        out_shape=[jax.ShapeDtypeStruct((bsz * t, c), F32), jax.ShapeDtypeStruct((bsz, SUBLANES, c), F32)],
        scratch_shapes=[pltpu.VMEM((SUBLANES + tt, c), F32)],
        compiler_params=_cp(("parallel", "arbitrary")), name=name)(proj, conv_w)


def _neumann_inv(ms, n):
    eye = (_iota2((n, n), 0) == _iota2((n, n), 1)).astype(F32)
    ts = [eye - m for m in ms]
    ps = [_dot(m, m) for m in ms]
    k = 2
    while True:
        ts = [t + _dot(t, p) for t, p in zip(ts, ps)]
        k *= 2
        if k >= n:
            return ts
        ps = [_dot(p, p) for p in ps]


def _gdn_chunks(qkv, bg, proj, norm_g, bsz, t, name):
    c = GDN_CHUNK
    nc = t // c
    z_blk = GDN_CONV_DIM // GDN_V

    def body(q_ref, k_ref, v_ref, bg_ref, z_ref, ng_ref, o_ref, s_out_ref, s_ref):
        i = pl.program_id(1)

        @pl.when(i == 0)
        def _():
            s_ref[...] = jnp.zeros_like(s_ref)

        bgv = bg_ref[...]
        ltri = _tri(c).astype(F32)
        gc = _dot3(ltri, bgv)
        gct = jnp.concatenate([gc, jnp.zeros_like(gc)], axis=0).T
        incl = _tri(c)
        strict = _tri(c, strict=True)
        ng = ng_ref[...]
        rep = GDN_VH // GDN_KH
        heads = range(GDN_VH)
        hsl = lambda n: slice(n * GDN_HD, (n + 1) * GDN_HD)
        gram = [_dot_nt(jnp.concatenate([k_ref[:, hsl(n)], q_ref[:, hsl(n)]], axis=0), k_ref[:, hsl(n)])
                for n in range(GDN_KH)]
        ms, aqks, rhss, q_ins, k_outs, g_ends = [], [], [], [], [], []
        for h in heads:
            kh = h // rep
            kk = k_ref[:, hsl(kh)]
            beta = jnp.broadcast_to(bgv[:, h:h + 1], (c, GDN_HD))
            gcol = jnp.broadcast_to(gc[:, GDN_VH + h:GDN_VH + h + 1], (c, GDN_HD))
            grow = gct[GDN_VH + h:GDN_VH + h + 1, :c]
            decay = jnp.where(incl, jnp.exp(jnp.where(incl, gcol[:, :c] - grow, 0.0)), 0.0)
            ms.append(jnp.where(strict, gram[kh][:c] * beta[:, :c] * decay, 0.0))
            aqks.append(gram[kh][c:] * decay)
            egc = jnp.exp(gcol)
            rhss.append(jnp.concatenate([v_ref[:, hsl(h)] * beta, kk * (beta * egc)], axis=1))
            q_ins.append(q_ref[:, hsl(kh)] * egc)
            glast = gc[c - 1:c, GDN_VH + h:GDN_VH + h + 1]
            k_outs.append(kk * jnp.exp(glast - gcol))
            g_ends.append(jnp.exp(glast))
        tinvs = _neumann_inv(ms, c)
        sols = [_dot(tinvs[h], rhss[h]) for h in heads]
        wqs = [_dot(jnp.concatenate([sols[h][:, GDN_HD:], q_ins[h]], axis=0), s_ref[h]) for h in heads]
        v_news = [sols[h][:, :GDN_HD] - wqs[h][:c] for h in heads]
        avs = [_dot(aqks[h], v_news[h]) for h in heads]
        kvs = [_dot_tn(k_outs[h], v_news[h]) for h in heads]
        for h in heads:
            s_ref[h] = s_ref[h] * g_ends[h] + kvs[h]
            o = wqs[h][c:] + avs[h]
            o = o * lax.rsqrt(jnp.mean(o * o, -1, keepdims=True) + 1e-6) * ng
            o_ref[:, hsl(h)] = o * _silu(z_ref[:, hsl(h)])

        @pl.when(i == nc - 1)
        def _():
            s_out_ref[0] = s_ref[...]

    return pl.pallas_call(
        body, grid=(bsz, nc),
        in_specs=[pl.BlockSpec((c, GDN_QK), lambda b, i: (b * nc + i, 0)),
                  pl.BlockSpec((c, GDN_QK), lambda b, i: (b * nc + i, 1)),
                  pl.BlockSpec((c, GDN_V), lambda b, i: (b * nc + i, 1)),
                  pl.BlockSpec((c, LANES), lambda b, i: (b * nc + i, 0)),
                  pl.BlockSpec((c, GDN_V), lambda b, i: (b * nc + i, z_blk)),
                  pl.BlockSpec((1, GDN_HD), lambda b, i: (0, 0))],
        out_specs=[pl.BlockSpec((c, GDN_V), lambda b, i: (b * nc + i, 0)),
                   pl.BlockSpec((1, GDN_VH, GDN_HD, GDN_HD), lambda b, i: (b, 0, 0, 0))],
        out_shape=[jax.ShapeDtypeStruct((bsz * t, GDN_V), F32),
                   jax.ShapeDtypeStruct((bsz, GDN_VH, GDN_HD, GDN_HD), F32)],
        scratch_shapes=[pltpu.VMEM((GDN_VH, GDN_HD, GDN_HD), F32)],
        compiler_params=_cp(("parallel", "arbitrary")), name=name,
    )(qkv, qkv, qkv, bg, proj, norm_g.reshape(1, -1))


def _gdn_prompt(x, bsz, t, w_in, conv_w, a_log, dt_bias, norm_g, w_out):
    proj, bg = _gdn_proj(x, w_in, a_log, dt_bias, min(bsz * t, 2048), "gdn_proj")
    qkv, tail = _gdn_conv_prompt(proj, conv_w, bsz, t, min(t, 256), "gdn_conv")
    o, s = _gdn_chunks(qkv, bg, proj, norm_g, bsz, t, "gdn_chunks")
    return _proj_mix(o, w_out), tail[:, SUBLANES - (GDN_CONV_W - 1):], s


def _s5_discretize(a_re, a_im, log_dt, b_re, b_im):
    g, p = a_re.shape
    bt_re = jnp.swapaxes(b_re, 1, 2)
    bt_im = jnp.swapaxes(b_im, 1, 2)

    def body(ar_ref, ai_ref, ldt_ref, br_ref, bi_ref, lr_ref, li_ref, bbr_ref, bbi_ref):
        ar, ai = ar_ref[...], ai_ref[...]
        dt = jnp.exp(ldt_ref[...])
        mag = jnp.exp(ar * dt)
        lr, li = mag * jnp.cos(ai * dt), mag * jnp.sin(ai * dt)
        den = ar * ar + ai * ai
        f_re = ((lr - 1.0) * ar + li * ai) / den
        f_im = (li * ar - (lr - 1.0) * ai) / den
        lr_ref[...] = lr
        li_ref[...] = li
        br, bi = br_ref[...], bi_ref[...]
        fr, fi = f_re[:, None, :], f_im[:, None, :]
        bbr_ref[...] = fr * br - fi * bi
        bbi_ref[...] = fr * bi + fi * br

    args = (a_re, a_im, log_dt.reshape(g, 1), bt_re, bt_im)
    return pl.pallas_call(
        body, grid=(1,), in_specs=[_full_spec(a) for a in args],
        out_specs=[pl.BlockSpec((g, p), lambda i: (0, 0))] * 2 + [pl.BlockSpec(bt_re.shape, lambda i: (0, 0, 0))] * 2,
        out_shape=[jax.ShapeDtypeStruct((g, p), F32)] * 2 + [jax.ShapeDtypeStruct(bt_re.shape, F32)] * 2,
        name="s5_discretize")(*args)


def _blockdiag(a, per):
    g, r, c = a.shape
    a4 = a.reshape(g // per, per, r, c)
    eye = jnp.eye(per, dtype=a.dtype)
    return jnp.einsum("jgrc,gh->jgrhc", a4, eye).reshape(g // per, per * r, per * c)


S5_GPT = LANES // S5_GROUP
S5_NT = S5_GROUPS // S5_GPT
S5_HT = S5_GPT * S5_STATE


def _s5_weights(a_re, a_im, log_dt, b_re, b_im, c_re, c_im):
    lr, li, bbr, bbi = _s5_discretize(a_re, a_im, log_dt, b_re, b_im)
    bcat = jnp.concatenate([_blockdiag(bbr, S5_GPT), _blockdiag(bbi, S5_GPT)], axis=2)
    ccat = jnp.concatenate([_blockdiag(jnp.swapaxes(c_re, 1, 2), S5_GPT),
                            -_blockdiag(jnp.swapaxes(c_im, 1, 2), S5_GPT)], axis=1)
    return lr.reshape(1, S5_H), li.reshape(1, S5_H), bcat, ccat


def _gelu(y):
    return 0.5 * y * (1.0 + jnp.tanh(math.sqrt(2.0 / math.pi) * (y + 0.044715 * (y * y * y))))


def _s5_scan(x, bsz, t, s_re0, s_im0, lr, li, bcat, ccat, d_skip, tc, name):
    d = x.shape[1]
    nt = t // tc
    rows = bsz * tc
    lq = 1024

    def body(x_ref, sr0_ref, si0_ref, lr_ref, li_ref, b_ref, c_ref, d_ref, z_ref, sr_ref, si_ref, hre, him, xs, zs):
        i = pl.program_id(0)

        @pl.when(i == 0)
        def _():
            sr_ref[...] = sr0_ref[...]
            si_ref[...] = si0_ref[...]

        xs[...] = jnp.swapaxes(x_ref[...], 0, 1).reshape(rows, d)

        def x_tile(j):
            return xs[:, j * LANES:(j + 1) * LANES]

        for j in range(S5_NT):
            bu = _dot(x_tile(j), b_ref[j])
            hre[:, j * S5_HT:(j + 1) * S5_HT] = bu[:, :S5_HT]
            him[:, j * S5_HT:(j + 1) * S5_HT] = bu[:, S5_HT:]

        for q in range(S5_H // lq):
            ls = slice(q * lq, (q + 1) * lq)
            lam_r = jnp.broadcast_to(lr_ref[:, ls], (bsz, lq))
            lam_i = jnp.broadcast_to(li_ref[:, ls], (bsz, lq))

            def step(tt, carry):
                sr, si = carry
                idx = pl.ds(pl.multiple_of(tt * bsz, bsz), bsz)
                nr = lam_r * sr - lam_i * si + hre[idx, ls]
                ni = lam_r * si + lam_i * sr + him[idx, ls]
                hre[idx, ls] = nr
                him[idx, ls] = ni
                return nr, ni

            sr, si = lax.fori_loop(0, tc, step, (sr_ref[:, ls], si_ref[:, ls]), unroll=8)
            sr_ref[:, ls] = sr
            si_ref[:, ls] = si

        dsk = d_ref[...]
        for j in range(S5_NT):
            hs = slice(j * S5_HT, (j + 1) * S5_HT)
            cj = c_ref[j]
            y = _dot(hre[:, hs], cj[:S5_HT]) + _dot(him[:, hs], cj[S5_HT:])
            zs[:, j * LANES:(j + 1) * LANES] = _gelu(y + dsk[:, j * LANES:(j + 1) * LANES] * x_tile(j))
        z_ref[...] = jnp.swapaxes(zs[...].reshape(tc, bsz, d), 0, 1)

    fulls = (s_re0, s_im0, lr, li, bcat, ccat, d_skip.reshape(1, d))
    z, s_re, s_im = pl.pallas_call(
        body, grid=(nt,),
        in_specs=[pl.BlockSpec((bsz, tc, d), lambda i: (0, i, 0))] + [_full_spec(a) for a in fulls],
        out_specs=[pl.BlockSpec((bsz, tc, d), lambda i: (0, i, 0)),
                   pl.BlockSpec((bsz, S5_H), lambda i: (0, 0)), pl.BlockSpec((bsz, S5_H), lambda i: (0, 0))],
        out_shape=[jax.ShapeDtypeStruct((bsz, t, d), F32),
                   jax.ShapeDtypeStruct((bsz, S5_H), F32), jax.ShapeDtypeStruct((bsz, S5_H), F32)],
        scratch_shapes=[pltpu.VMEM((rows, S5_H), F32), pltpu.VMEM((rows, S5_H), F32),
                        pltpu.VMEM((rows, d), F32), pltpu.VMEM((rows, d), F32)],
        compiler_params=_cp(("arbitrary",)), name=name)(x.reshape(bsz, t, d), *fulls)
    return z.reshape(bsz * t, d), s_re, s_im


def _s5_mix(z, w_o, w_gate):
    def fn(z_t, wo_ref, wg_ref):
        zb = z_t.astype(BF16)
        return _dot(zb, wo_ref[...]) * _sigmoid(_dot(zb, wg_ref[...]))

    return Mix((z,), (w_o.astype(BF16), w_gate.astype(BF16)), fn)


def _s5_prompt(x, bsz, t, a_re, a_im, log_dt, b_re, b_im, c_re, c_im, d_skip, w_o, w_gate):
    lr, li, bcat, ccat = _s5_weights(a_re, a_im, log_dt, b_re, b_im, c_re, c_im)
    zero = jnp.zeros((bsz, S5_H), F32)
    z, s_re, s_im = _s5_scan(x, bsz, t, zero, zero, lr, li, bcat, ccat, d_skip, min(t, 64), "s5_scan")
    return _s5_mix(z, w_o, w_gate), s_re.reshape(bsz, S5_GROUPS, S5_STATE), s_im.reshape(bsz, S5_GROUPS, S5_STATE)


GLA_PROJ = 2 * GLA_QK + 2 * GLA_V


def _gla_proj(x, w_in, w_gk2, b_gk, tm, name):
    w1 = jnp.pad(w_in[:, GLA_PROJ:], ((0, 0), (0, LANES - GLA_RANK))).astype(BF16)
    w2 = jnp.pad(w_gk2, ((0, LANES - GLA_RANK), (0, 0)))

    def logd(xb, w1_ref, w2_ref, b_ref):
        y = _dot(_dot(xb, w1_ref[...]), w2_ref[...]) + b_ref[...]
        return -_softplus(-y) * (1.0 / GLA_GATE_NORM)

    return _mm(x, w_in, GLA_PROJ, tm, 512, name, logd, (w1, w2, b_gk.reshape(1, -1)), GLA_QK)


def _gla_chunks(proj, logd, norm_g, bsz, t, name):
    c = GLA_CHUNK
    rows = 128 if t % 128 == 0 else 64
    nr = t // rows
    scale = GLA_DK ** -0.5

    def body(q_ref, k_ref, v_ref, gate_ref, ld_ref, ng_ref, o_ref, s_out_ref, st_ref):
        i = pl.program_id(1)

        @pl.when(i == 0)
        def _():
            st_ref[...] = jnp.zeros_like(st_ref)

        lblk = _tri(rows, block=c).astype(F32)
        bc_all = _dot3(lblk, ld_ref[...])
        incl = _tri(c)
        ng = ng_ref[...]
        subs = range(rows // c)
        heads = range(GLA_HEADS)
        rsl = lambda s: slice(s * c, (s + 1) * c)
        ksl = lambda h: slice(h * GLA_DK, (h + 1) * GLA_DK)
        vsl = lambda h: slice(h * GLA_DV, (h + 1) * GLA_DV)
        q_ins, k_outs, g_ends, a_s = {}, {}, {}, {}
        for s in subs:
            for h in heads:
                bc = bc_all[rsl(s), ksl(h)]
                k = k_ref[rsl(s), ksl(h)]
                bcl = bc[c - 1:c]
                q_ins[s, h] = q_ref[rsl(s), ksl(h)] * scale * jnp.exp(bc)
                k_outs[s, h] = k * jnp.exp(bcl - bc)
                g_ends[s, h] = jnp.exp(bcl)
                a_s[s, h] = jnp.where(incl, _dot_nt(q_ins[s, h], k * jnp.exp(-bc)), 0.0)
        o_intra = {(s, h): _dot(a_s[s, h], v_ref[rsl(s), vsl(h)]) for s in subs for h in heads}
        for s in subs:
            sts = [st_ref[h] for h in heads]
            o_inter = [_dot_nt(q_ins[s, h], sts[h]) for h in heads]
            kvs = [_dot_tn(v_ref[rsl(s), vsl(h)], k_outs[s, h]) for h in heads]
            for h in heads:
                st_ref[h] = sts[h] * g_ends[s, h] + kvs[h]
                o = o_inter[h] + o_intra[s, h]
                o = o * lax.rsqrt(jnp.mean(o * o, -1, keepdims=True) + 1e-6) * ng
                o_ref[rsl(s), vsl(h)] = o * _silu(gate_ref[rsl(s), vsl(h)])

        @pl.when(i == nr - 1)
        def _():
            for h in range(GLA_HEADS):
                s_out_ref[0, h] = st_ref[h].T

    return pl.pallas_call(
        body, grid=(bsz, nr),
        in_specs=[pl.BlockSpec((rows, GLA_QK), lambda b, i: (b * nr + i, 0)),
                  pl.BlockSpec((rows, GLA_QK), lambda b, i: (b * nr + i, 1)),
                  pl.BlockSpec((rows, GLA_V), lambda b, i: (b * nr + i, 1)),
                  pl.BlockSpec((rows, GLA_V), lambda b, i: (b * nr + i, 2)),
                  pl.BlockSpec((rows, GLA_QK), lambda b, i: (b * nr + i, 0)),
                  pl.BlockSpec((1, GLA_DV), lambda b, i: (0, 0))],
        out_specs=[pl.BlockSpec((rows, GLA_V), lambda b, i: (b * nr + i, 0)),
                   pl.BlockSpec((1, GLA_HEADS, GLA_DK, GLA_DV), lambda b, i: (b, 0, 0, 0))],
        out_shape=[jax.ShapeDtypeStruct((bsz * t, GLA_V), F32),
                   jax.ShapeDtypeStruct((bsz, GLA_HEADS, GLA_DK, GLA_DV), F32)],
        scratch_shapes=[pltpu.VMEM((GLA_HEADS, GLA_DV, GLA_DK), F32)],
        compiler_params=_cp(("parallel", "arbitrary")), name=name,
    )(proj, proj, proj, proj, logd, norm_g.reshape(1, -1))


def _gla_prompt(x, bsz, t, w_in, w_gk2, b_gk, norm_g, w_out):
    proj, logd = _gla_proj(x, w_in, w_gk2, b_gk, min(bsz * t, 2048), "gla_proj")
    o, s = _gla_chunks(proj, logd, norm_g, bsz, t, "gla_chunks")
    return _proj_mix(o, w_out), s


def _rwkv_proj(x, xp, bsz, t, mu, w_rkv, w0, w_w1, w_w2, a0, w_a1, w_a2, w_g1, w_g2, tm, name):
    d = D_MODEL
    nt = t // tm
    carried = xp is None
    bf = lambda w: w.astype(BF16)

    def body(*refs):
        if carried:
            x_ref, refs, buf_ref = refs[0], refs[1:-1], refs[-1]
        else:
            x_ref, xp_ref, refs = refs[0], refs[1], refs[2:]
        mu_ref, wr_ref, w0_ref, ww1, ww2, a0_ref, wa1, wa2, wg1, wg2, r_ref, k_ref, v_ref, lw_ref, a_ref, g_ref = refs
        x = x_ref[...]
        if carried:
            @pl.when(pl.program_id(1) == 0)
            def _():
                buf_ref[pl.ds(0, SUBLANES)] = jnp.zeros((SUBLANES, d), F32)

            buf_ref[pl.ds(SUBLANES, tm)] = x
            x_prev = buf_ref[pl.ds(SUBLANES - 1, tm)]
            buf_ref[pl.ds(0, SUBLANES)] = x[tm - SUBLANES:]
        else:
            x_prev = xp_ref[...]
        dx = x_prev - x
        xs = lambda s: (x + dx * mu_ref[s:s + 1]).astype(BF16)
        for s, o_ref in enumerate((r_ref, k_ref, v_ref)):
            o_ref[...] = _dot(xs(s), wr_ref[s])
        lora = lambda h, w2_ref: _dot(h.astype(BF16), w2_ref[...])
        w_log = -_softplus(-(w0_ref[...] + lora(jnp.tanh(_dot(xs(3), ww1[...])), ww2))) - 0.5
        lw_ref[...] = -jnp.exp(w_log)
        a_ref[...] = _sigmoid(a0_ref[...] + lora(_dot(xs(4), wa1[...]), wa2))
        g_ref[...] = lora(_sigmoid(_dot(xs(5), wg1[...])), wg2)

    fulls = (mu, bf(w_rkv), w0.reshape(1, -1), bf(w_w1), bf(w_w2), a0.reshape(1, -1), bf(w_a1), bf(w_a2), bf(w_g1),
             bf(w_g2))
    tile = pl.BlockSpec((tm, d), lambda b, i: (b * nt + i, 0))
    return pl.pallas_call(
        body, grid=(bsz, nt),
        in_specs=[tile] * (1 if carried else 2) + [_resident_spec(a) for a in fulls],
        out_specs=[tile] * 6,
        out_shape=[jax.ShapeDtypeStruct((bsz * t, d), F32)] * 6,
        scratch_shapes=[pltpu.VMEM((SUBLANES + tm, d), F32)] if carried else [],
        compiler_params=_cp(("parallel", "arbitrary")), name=name,
    )(*((x,) if carried else (x, xp)), *fulls)


def _rwkv_head_inputs(r, k, v, a, kk_w, ka_w, sl):
    kraw = k[:, sl]
    kkn = _l2n(kraw * kk_w[:, sl], 1.0)
    ah = a[:, sl]
    kh = kraw * (1.0 + (ah - 1.0) * ka_w[:, sl])
    return r[:, sl], kh, v[:, sl], kkn, kkn * ah


def _rwkv_head_out(y, rh, kh, vh, g, rk_w, lng, lnb, sl):
    yc = y - jnp.mean(y, -1, keepdims=True)
    yn = yc * lax.rsqrt(jnp.mean(yc * yc, -1, keepdims=True) + RW_GN_EPS) * lng[:, sl] + lnb[:, sl]
    bonus = jnp.sum(rh * kh * rk_w[:, sl], -1, keepdims=True) * vh
    return (yn + bonus) * g[:, sl]


def _rwkv_chunks(r, k, v, lw, a, g, k_k, k_a, r_k, ln_g, ln_b, bsz, t, name):
    c = RW_CHUNK
    nc = t // c
    hd = RW_HD
    nb = 1
    d = D_MODEL

    def body(r_ref, k_ref, v_ref, lw_ref, a_ref, g_ref, kk_ref, ka_ref, rk_ref, lng_ref, lnb_ref,
             o_ref, s_out_ref, s_ref):
        i = pl.program_id(1)

        @pl.when(i == 0)
        def _():
            s_ref[...] = jnp.zeros_like(s_ref)

        kk_w, ka_w, rk_w, lng, lnb = kk_ref[...], ka_ref[...], rk_ref[...], lng_ref[...], lnb_ref[...]
        tri = _tri(c).astype(F32)
        strict = _tri(c, strict=True)
        incl = _tri(c)
        hsl = lambda n: slice(n * hd, (n + 1) * hd)
        inst = [(m, h) for m in range(nb) for h in range(RW_H)]
        prep = []
        for m in range(nb):
            lw = lw_ref[m]
            gam = _dot3(tri, lw)
            glast = gam[c - 1:c]
            r, k, a = r_ref[m], k_ref[m], a_ref[m]
            kk = k * kk_w
            scale = jnp.concatenate(
                [jnp.broadcast_to(lax.rsqrt(jnp.sum(kk[:, hsl(h)] * kk[:, hsl(h)], -1, keepdims=True) + 1e-6), (c, hd))
                 for h in range(RW_H)], axis=1)
            kkn = kk * scale
            kh = k * (1.0 + (a - 1.0) * ka_w)
            bh = kkn * a
            e_neg = jnp.exp(-gam)
            e_out = jnp.exp(glast - gam)
            prep.append(dict(a1=kkn * jnp.exp(gam - lw), r1=r * jnp.exp(gam), b1=bh * e_neg, k1=kh * e_neg,
                             b1o=bh * e_out, k1o=kh * e_out, g_end=jnp.exp(glast), r=r, kh=kh, v=v_ref[m],
                             g=g_ref[m]))
        ars = [jnp.concatenate([prep[m]["a1"][:, hsl(h)], prep[m]["r1"][:, hsl(h)]], axis=0) for m, h in inst]
        bks = [jnp.concatenate([prep[m]["b1"][:, hsl(h)], prep[m]["k1"][:, hsl(h)]], axis=0) for m, h in inst]
        vhs = [prep[m]["v"][:, hsl(h)] for m, h in inst]
        n = range(len(inst))
        gmats = [_dot_nt(ars[j], bks[j]) for j in n]
        tinvs = _neumann_inv([jnp.where(strict, gm_[:c, :c], 0.0) for gm_ in gmats], c)
        makvs = [_dot(jnp.where(strict, gmats[j][:c, c:], 0.0), vhs[j]) for j in n]
        rbks = [jnp.concatenate([jnp.where(incl, gm_[c:, :c], 0.0), jnp.where(incl, gm_[c:, c:], 0.0)], axis=1)
                for gm_ in gmats]
        a_ss = [_dot_nt(ars[j], s_ref[m, h]) for j, (m, h) in enumerate(inst)]
        uvs = [jnp.concatenate([_dot(tinvs[j], -a_ss[j][:c] - makvs[j]), vhs[j]], axis=0) for j in n]
        ys = [a_ss[j][c:] + _dot(rbks[j], uvs[j]) for j in n]
        svs = [_dot_tn(uvs[j], jnp.concatenate([prep[m]["b1o"][:, hsl(h)], prep[m]["k1o"][:, hsl(h)]], axis=0))
               for j, (m, h) in enumerate(inst)]
        for j, (m, h) in enumerate(inst):
            s_ref[m, h] = s_ref[m, h] * prep[m]["g_end"][:, hsl(h)] + svs[j]
        rkr = [prep[m]["r"] * prep[m]["kh"] * rk_w for m in range(nb)]
        sum1 = [jnp.sum(ys[j], -1, keepdims=True) for j in n]
        bons = [jnp.sum(rkr[m][:, hsl(h)], -1, keepdims=True) for m, h in inst]
        ycs = [ys[j] - sum1[j] * (1.0 / hd) for j in n]
        sum2 = [jnp.sum(ycs[j] * ycs[j], -1, keepdims=True) for j in n]
        for j, (m, h) in enumerate(inst):
            yn = ycs[j] * lax.rsqrt(sum2[j] * (1.0 / hd) + RW_GN_EPS) * lng[:, hsl(h)] + lnb[:, hsl(h)]
            o_ref[m, :, hsl(h)] = (yn + bons[j] * vhs[j]) * prep[m]["g"][:, hsl(h)]

        @pl.when(i == nc - 1)
        def _():
            s_out_ref[...] = s_ref[...]

    row = lambda w: w.reshape(1, d)
    v3 = lambda z: z.reshape(bsz, t, d)
    blk = pl.BlockSpec((nb, c, d), lambda b, i: (b, i, 0))
    par = pl.BlockSpec((1, d), lambda b, i: (0, 0))
    y, s = pl.pallas_call(
        body, grid=(bsz // nb, nc),
        in_specs=[blk] * 6 + [par] * 5,
        out_specs=[blk, pl.BlockSpec((nb, RW_H, hd, hd), lambda b, i: (b, 0, 0, 0))],
        out_shape=[jax.ShapeDtypeStruct((bsz, t, d), F32), jax.ShapeDtypeStruct((bsz, RW_H, hd, hd), F32)],
        scratch_shapes=[pltpu.VMEM((nb, RW_H, hd, hd), F32)],
        compiler_params=_cp(("parallel", "arbitrary")), name=name,
    )(v3(r), v3(k), v3(v), v3(lw), v3(a), v3(g), row(k_k), row(k_a), row(r_k), row(ln_g), row(ln_b))
    return y.reshape(bsz * t, d), s


def _rwkv_chunks_pairs(r, k, v, lw, a, g, k_k, k_a, r_k, ln_g, ln_b, bsz, t, name):
    c = RW_CHUNK
    nc = t // c
    hd = RW_HD
    d = D_MODEL
    npair = RW_H // 2

    def body(r_ref, k_ref, v_ref, lw_ref, a_ref, g_ref, kk_ref, ka_ref, rk_ref, lng_ref, lnb_ref,
             o_ref, s_out_ref, s_ref):
        i = pl.program_id(1)

        @pl.when(i == 0)
        def _():
            s_ref[...] = jnp.zeros_like(s_ref)

        kk_w, ka_w, rk_w, lng, lnb = kk_ref[...], ka_ref[...], rk_ref[...], lng_ref[...], lnb_ref[...]
        tri = _tri(c).astype(F32)
        strict = _tri(c, strict=True)
        incl = _tri(c)
        lo = _iota2((c, LANES), 1) < hd
        lo2 = _iota2((2 * c, LANES), 1) < hd
        bdiag = (_iota2((LANES, LANES), 0) < hd) == (_iota2((LANES, LANES), 1) < hd)
        psl = lambda p: slice(p * LANES, (p + 1) * LANES)
        pairs = range(npair)
        heads = [(p, hb) for p in pairs for hb in (0, 1)]
        own = lambda hb, x, m: jnp.where(m, x, 0.0) if hb == 0 else jnp.where(m, 0.0, x)

        def head_sums(x):
            return jnp.where(lo, jnp.sum(jnp.where(lo, x, 0.0), -1, keepdims=True),
                             jnp.sum(jnp.where(lo, 0.0, x), -1, keepdims=True))

        lw = lw_ref[0]
        gam = _dot3(tri, lw)
        glast = gam[c - 1:c]
        r, k, a, v, g = r_ref[0], k_ref[0], a_ref[0], v_ref[0], g_ref[0]
        kk = k * kk_w
        ssq = [head_sums(kk[:, psl(p)] * kk[:, psl(p)]) for p in pairs]
        kkn = kk * lax.rsqrt(jnp.concatenate(ssq, axis=1) + 1e-6)
        kh = k * (1.0 + (a - 1.0) * ka_w)
        bh = kkn * a
        e_neg = jnp.exp(-gam)
        e_out = jnp.exp(glast - gam)
        a1, r1, b1, k1 = kkn * jnp.exp(gam - lw), r * jnp.exp(gam), bh * e_neg, kh * e_neg
        b1o, k1o, g_end = bh * e_out, kh * e_out, jnp.exp(glast)
        ars = [jnp.concatenate([a1[:, psl(p)], r1[:, psl(p)]], axis=0) for p in pairs]
        bks = [jnp.concatenate([b1[:, psl(p)], k1[:, psl(p)]], axis=0) for p in pairs]
        bkos = [jnp.concatenate([b1o[:, psl(p)], k1o[:, psl(p)]], axis=0) for p in pairs]
        vps = [v[:, psl(p)] for p in pairs]
        gmats = [_dot_nt(ars[p], own(hb, bks[p], lo2)) for p, hb in heads]
        tinvs = _neumann_inv([jnp.where(strict, gm_[:c, :c], 0.0) for gm_ in gmats], c)
        makvs = [_dot(jnp.where(strict, gmats[j][:c, c:], 0.0), vps[p]) for j, (p, hb) in enumerate(heads)]
        rbks = [jnp.concatenate([jnp.where(incl, gm_[c:, :c], 0.0), jnp.where(incl, gm_[c:, c:], 0.0)], axis=1)
                for gm_ in gmats]
        a_ss = [_dot_nt(ars[p], s_ref[p]) for p in pairs]
        rhs = [-a_ss[p][:c] - jnp.where(lo, makvs[2 * p], makvs[2 * p + 1]) for p in pairs]
        us = [_dot(tinvs[j], rhs[p]) for j, (p, hb) in enumerate(heads)]
        uvs = [jnp.concatenate([jnp.where(lo, us[2 * p], us[2 * p + 1]), vps[p]], axis=0) for p in pairs]
        yhs = [_dot(rbks[j], uvs[p]) for j, (p, hb) in enumerate(heads)]
        ys = [a_ss[p][c:] + jnp.where(lo, yhs[2 * p], yhs[2 * p + 1]) for p in pairs]
        svs = [_dot_tn(uvs[p], bkos[p]) for p in pairs]
        for p in pairs:
            s_ref[p] = s_ref[p] * g_end[:, psl(p)] + jnp.where(bdiag, svs[p], 0.0)
        rkr = r * kh * rk_w
        mean = [head_sums(ys[p]) * (1.0 / hd) for p in pairs]
        bons = [head_sums(rkr[:, psl(p)]) for p in pairs]
        ycs = [ys[p] - mean[p] for p in pairs]
        var = [head_sums(ycs[p] * ycs[p]) * (1.0 / hd) for p in pairs]
        for p in pairs:
            yn = ycs[p] * lax.rsqrt(var[p] + RW_GN_EPS) * lng[:, psl(p)] + lnb[:, psl(p)]
            o_ref[0, :, psl(p)] = (yn + bons[p] * vps[p]) * g[:, psl(p)]

        @pl.when(i == nc - 1)
        def _():
            for p in pairs:
                blk = s_ref[p]
                s_out_ref[0, 2 * p] = blk[:hd, :hd]
                s_out_ref[0, 2 * p + 1] = blk[hd:, hd:]

    row = lambda w: w.reshape(1, d)
    v3 = lambda z: z.reshape(bsz, t, d)
    blk = pl.BlockSpec((1, c, d), lambda b, i: (b, i, 0))
    par = pl.BlockSpec((1, d), lambda b, i: (0, 0))
    y, s = pl.pallas_call(
        body, grid=(bsz, nc),
        in_specs=[blk] * 6 + [par] * 5,
        out_specs=[blk, pl.BlockSpec((1, RW_H, hd, hd), lambda b, i: (b, 0, 0, 0))],
        out_shape=[jax.ShapeDtypeStruct((bsz, t, d), F32), jax.ShapeDtypeStruct((bsz, RW_H, hd, hd), F32)],
        scratch_shapes=[pltpu.VMEM((npair, LANES, LANES), F32)],
        compiler_params=_cp(("parallel", "arbitrary")), name=name,
    )(v3(r), v3(k), v3(v), v3(lw), v3(a), v3(g), row(k_k), row(k_a), row(r_k), row(ln_g), row(ln_b))
    return y.reshape(bsz * t, d), s


def _rwkv_prompt(x, bsz, t, mu, w_rkv, w0, w_w1, w_w2, a0, w_a1, w_a2, w_g1, w_g2, k_k, k_a, r_k, gn_g, gn_b, w_o):
    r, k, v, lw, a, g = _rwkv_proj(x, None, bsz, t, mu, w_rkv, w0, w_w1, w_w2, a0, w_a1, w_a2, w_g1, w_g2,
                                   min(t, 512), "rwkv_proj")
    y, s = _rwkv_chunks_pairs(r, k, v, lw, a, g, k_k, k_a, r_k, gn_g, gn_b, bsz, t, "rwkv_chunks")
    return _proj_mix(y, w_o), x.reshape(bsz, t, D_MODEL)[:, -1], s


def _eye(n):
    return _iota2((n, n), 0) == _iota2((n, n), 1)


def _to_col(row, eye):
    return jnp.sum(jnp.where(eye, row, 0.0), axis=1, keepdims=True)


def _to_row(col, eye):
    return jnp.sum(jnp.where(eye, col, 0.0), axis=0, keepdims=True)


def _row3(a):
    return a.reshape(a.shape[0], 1, a.shape[1])


def _rows_spec(width, col_block=0, nb=1):
    return pl.BlockSpec((nb, 1, width), lambda b: (b, 0, col_block))


STEP_ROWS = 4


def _gdn_conv_step(proj, buf, conv_w, name):
    def body(p_ref, b0, b1, b2, w_ref, o_ref):
        w = w_ref[...]
        y = p_ref[:, :GDN_CONV_DIM] * w[3:4] + b2[...] * w[2:3] + b1[...] * w[1:2] + b0[...] * w[0:1]
        _gdn_act(y, o_ref)

    rows = [proj, buf[:, 0], buf[:, 1], buf[:, 2]]
    return _rowwise(body, rows, [conv_w], [GDN_CONV_DIM], proj.shape[0], name)[0]


def _gdn_step(qkv, bg, proj, s0, norm_g, name):
    bsz = qkv.shape[0]
    rep = GDN_VH // GDN_KH
    nb = STEP_ROWS if bsz % STEP_ROWS == 0 else 1

    def body(q_ref, k_ref, v_ref, bg_ref, z_ref, ng_ref, s_ref, o_ref, so_ref):
        eye = _eye(GDN_HD)
        ng = ng_ref[...]
        hsl = lambda n: slice(n * GDN_HD, (n + 1) * GDN_HD)
        jk = [(j, kh) for j in range(nb) for kh in range(GDN_KH)]
        jh = [(j, h) for j in range(nb) for h in range(GDN_VH)]
        qrow = {(j, kh): q_ref[j, :, hsl(kh)] for j, kh in jk}
        krow = {(j, kh): k_ref[j, :, hsl(kh)] for j, kh in jk}
        qcol = {i: _to_col(qrow[i], eye) for i in jk}
        kcol = {i: _to_col(krow[i], eye) for i in jk}
        qk = {i: jnp.sum(qrow[i] * krow[i], axis=1, keepdims=True) for i in jk}
        eg = {(j, h): jnp.exp(bg_ref[j, :, GDN_VH + h:GDN_VH + h + 1]) for j, h in jh}
        ks = {(j, h): jnp.sum(kcol[j, h // rep] * s_ref[j, h], axis=0, keepdims=True) for j, h in jh}
        qs = {(j, h): jnp.sum(qcol[j, h // rep] * s_ref[j, h], axis=0, keepdims=True) for j, h in jh}
        v_new = {(j, h): bg_ref[j, :, h:h + 1] * (v_ref[j, :, hsl(h)] - eg[j, h] * ks[j, h]) for j, h in jh}
        o = {(j, h): eg[j, h] * qs[j, h] + qk[j, h // rep] * v_new[j, h] for j, h in jh}
        ms = {i: jnp.mean(o[i] * o[i], -1, keepdims=True) for i in jh}
        for j, h in jh:
            so_ref[j, h] = s_ref[j, h] * eg[j, h] + kcol[j, h // rep] * v_new[j, h]
            o_ref[j, :, hsl(h)] = o[j, h] * lax.rsqrt(ms[j, h] + 1e-6) * ng * _silu(z_ref[j, :, hsl(h)])

    st_spec = pl.BlockSpec((nb, GDN_VH, GDN_HD, GDN_HD), lambda b: (b, 0, 0, 0))
    o, s = pl.pallas_call(
        body, grid=(bsz // nb,),
        in_specs=[_rows_spec(GDN_QK, 0, nb), _rows_spec(GDN_QK, 1, nb), _rows_spec(GDN_V, 1, nb),
                  _rows_spec(LANES, 0, nb), _rows_spec(GDN_V, GDN_CONV_DIM // GDN_V, nb),
                  pl.BlockSpec((1, GDN_HD), lambda b: (0, 0)), st_spec],
        out_specs=[_rows_spec(GDN_V, 0, nb), st_spec],
        out_shape=[jax.ShapeDtypeStruct((bsz, 1, GDN_V), F32), jax.ShapeDtypeStruct(s0.shape, F32)],
        compiler_params=_cp(("parallel",)), name=name,
    )(_row3(qkv), _row3(qkv), _row3(qkv), _row3(bg), _row3(proj), norm_g.reshape(1, -1), s0)
    return o.reshape(bsz, GDN_V), s


def _gdn_sample(x, buf, s0, w_in, conv_w, a_log, dt_bias, norm_g, w_out):
    proj, bg = _gdn_proj(x, w_in, a_log, dt_bias, x.shape[0], "gdn_proj_s")
    qkv = _gdn_conv_step(proj, buf, conv_w, "gdn_conv_s")
    o, s = _gdn_step(qkv, bg, proj, s0, norm_g, "gdn_step_s")
    new_buf = jnp.concatenate([buf[:, 1:], proj[:, None, :GDN_CONV_DIM]], axis=1)
    return _proj_mix(o, w_out), new_buf, s


def _s5_step(x, s_re, s_im, lr, li, bcat, ccat, d_skip, name):
    def body(x_ref, sr_ref, si_ref, lr_ref, li_ref, b_ref, c_ref, d_ref, z_ref, hr_ref, hi_ref):
        x = x_ref[...]
        dsk = d_ref[...]
        for j in range(S5_NT):
            hs = slice(j * S5_HT, (j + 1) * S5_HT)
            xs = x[:, j * LANES:(j + 1) * LANES]
            bu = _dot(xs, b_ref[j])
            lam_r, lam_i = lr_ref[:, hs], li_ref[:, hs]
            sr, si = sr_ref[:, hs], si_ref[:, hs]
            h_re = lam_r * sr - lam_i * si + bu[:, :S5_HT]
            h_im = lam_r * si + lam_i * sr + bu[:, S5_HT:]
            hr_ref[:, hs] = h_re
            hi_ref[:, hs] = h_im
            cj = c_ref[j]
            y = _dot(h_re, cj[:S5_HT]) + _dot(h_im, cj[S5_HT:])
            z_ref[:, j * LANES:(j + 1) * LANES] = _gelu(y + dsk[:, j * LANES:(j + 1) * LANES] * xs)

    fulls = [lr, li, bcat, ccat, d_skip.reshape(1, -1)]
    return _rowwise(body, [x, s_re, s_im], fulls, [D_MODEL, S5_H, S5_H], x.shape[0], name)


def _s5_sample(x, s_re0, s_im0, a_re, a_im, log_dt, b_re, b_im, c_re, c_im, d_skip, w_o, w_gate):
    m = x.shape[0]
    lr, li, bcat, ccat = _s5_weights(a_re, a_im, log_dt, b_re, b_im, c_re, c_im)
    z, h_re, h_im = _s5_step(x, s_re0.reshape(m, S5_H), s_im0.reshape(m, S5_H), lr, li, bcat, ccat, d_skip, "s5_step_s")
    return _s5_mix(z, w_o, w_gate), h_re.reshape(m, S5_GROUPS, S5_STATE), h_im.reshape(m, S5_GROUPS, S5_STATE)


def _gla_step(proj, logd, s0, norm_g, name):
    bsz = proj.shape[0]
    scale = GLA_DK ** -0.5
    nb = STEP_ROWS if bsz % STEP_ROWS == 0 else 1

    def body(q_ref, k_ref, v_ref, gate_ref, ld_ref, ng_ref, s_ref, o_ref, so_ref):
        eye = _eye(GLA_DK)
        ng = ng_ref[...]
        ksl = lambda h: slice(h * GLA_DK, (h + 1) * GLA_DK)
        vsl = lambda h: slice(h * GLA_DV, (h + 1) * GLA_DV)
        jh = [(j, h) for j in range(nb) for h in range(GLA_HEADS)]
        bc = {(j, h): ld_ref[j, :, ksl(h)] for j, h in jh}
        k = {(j, h): k_ref[j, :, ksl(h)] for j, h in jh}
        q_in = {(j, h): q_ref[j, :, ksl(h)] * scale * jnp.exp(bc[j, h]) for j, h in jh}
        a = {i: jnp.sum(q_in[i] * (k[i] * jnp.exp(-bc[i])), axis=1, keepdims=True) for i in jh}
        qcol = {i: _to_col(q_in[i], eye) for i in jh}
        gcol = {i: _to_col(jnp.exp(bc[i]), eye) for i in jh}
        kcol = {i: _to_col(k[i], eye) for i in jh}
        o = {(j, h): jnp.sum(qcol[j, h] * s_ref[j, h], axis=0, keepdims=True) + a[j, h] * v_ref[j, :, vsl(h)]
             for j, h in jh}
        ms = {i: jnp.mean(o[i] * o[i], -1, keepdims=True) for i in jh}
        for j, h in jh:
            so_ref[j, h] = s_ref[j, h] * gcol[j, h] + kcol[j, h] * v_ref[j, :, vsl(h)]
            o_ref[j, :, vsl(h)] = o[j, h] * lax.rsqrt(ms[j, h] + 1e-6) * ng * _silu(gate_ref[j, :, vsl(h)])

    st_spec = pl.BlockSpec((nb, GLA_HEADS, GLA_DK, GLA_DV), lambda b: (b, 0, 0, 0))
    o, s = pl.pallas_call(
        body, grid=(bsz // nb,),
        in_specs=[_rows_spec(GLA_QK, 0, nb), _rows_spec(GLA_QK, 1, nb), _rows_spec(GLA_V, 1, nb),
                  _rows_spec(GLA_V, 2, nb), _rows_spec(GLA_QK, 0, nb), pl.BlockSpec((1, GLA_DV), lambda b: (0, 0)),
                  st_spec],
        out_specs=[_rows_spec(GLA_V, 0, nb), st_spec],
        out_shape=[jax.ShapeDtypeStruct((bsz, 1, GLA_V), F32), jax.ShapeDtypeStruct(s0.shape, F32)],
        compiler_params=_cp(("parallel",)), name=name,
    )(_row3(proj), _row3(proj), _row3(proj), _row3(proj), _row3(logd), norm_g.reshape(1, -1), s0)
    return o.reshape(bsz, GLA_V), s


def _gla_sample(x, s0, w_in, w_gk2, b_gk, norm_g, w_out):
    proj, logd = _gla_proj(x, w_in, w_gk2, b_gk, x.shape[0], "gla_proj_s")
    o, s = _gla_step(proj, logd, s0, norm_g, "gla_step_s")
    return _proj_mix(o, w_out), s


def _dot3_r(x, l):
    hi = x.astype(BF16).astype(F32)
    r1 = x - hi
    mid = r1.astype(BF16).astype(F32)
    lo = r1 - mid
    return _dot(hi, l) + _dot(mid, l) + _dot(lo, l)


def _head_ones():
    return jnp.kron(jnp.eye(RW_H, dtype=F32), jnp.ones((RW_HD, RW_HD), F32))


def _rwkv_step_prep(r, k, v, lw, a, k_k, k_a, name):
    bsz = r.shape[0]

    def body(r_ref, k_ref, v_ref, lw_ref, a_ref, kk_ref, ka_ref, ones_ref, rt, kt, vt, kkt, bt, dt, kh_ref):
        k, a = k_ref[...], a_ref[...]
        kk = k * kk_ref[...]
        kkn = kk * lax.rsqrt(_dot3_r(kk * kk, ones_ref[...]) + 1e-6)
        kh = k * (1.0 + (a - 1.0) * ka_ref[...])
        kh_ref[...] = kh
        rt[...] = r_ref[...].T
        kt[...] = kh.T
        vt[...] = v_ref[...].T
        kkt[...] = kkn.T
        bt[...] = (kkn * a).T
        dt[...] = jnp.exp(lw_ref[...]).T

    args = (r, k, v, lw, a, k_k.reshape(1, -1), k_a.reshape(1, -1), _head_ones())
    tshape = jax.ShapeDtypeStruct((D_MODEL, bsz), F32)
    return pl.pallas_call(
        body, grid=(1,), in_specs=[_full_spec(t) for t in args],
        out_specs=[pl.BlockSpec((D_MODEL, bsz), lambda i: (0, 0))] * 6 + [pl.BlockSpec((bsz, D_MODEL), lambda i: (0, 0))],
        out_shape=[tshape] * 6 + [jax.ShapeDtypeStruct((bsz, D_MODEL), F32)],
        compiler_params=_cp(("arbitrary",)), name=name)(*args)


def _rwkv_step_lanes(rt, kt, vt, kkt, bt, dt, s_t, name):
    hd = RW_HD
    bsz = s_t.shape[-1]

    def body(rt_ref, kt_ref, vt_ref, kkt_ref, bt_ref, dt_ref, s_ref, y_ref, so_ref):
        hs = pl.ds(pl.multiple_of(pl.program_id(0) * hd, hd), hd)
        r_h, k_h, v_h, kk_h, b_h, d_h = (ref[hs, :] for ref in (rt_ref, kt_ref, vt_ref, kkt_ref, bt_ref, dt_ref))
        ys = []
        for vi in range(hd):
            s = s_ref[0, vi]
            sa = -jnp.sum(s * kk_h, axis=0, keepdims=True)
            s_new = s * d_h + sa * b_h + v_h[vi:vi + 1] * k_h
            so_ref[0, vi] = s_new
            ys.append(jnp.sum(s_new * r_h, axis=0, keepdims=True))
        y_ref[...] = jnp.concatenate(ys, axis=0)

    vec = pl.BlockSpec((D_MODEL, bsz), lambda h: (0, 0))
    st = pl.BlockSpec((1, hd, hd, bsz), lambda h: (h, 0, 0, 0))
    return pl.pallas_call(
        body, grid=(RW_H,), in_specs=[vec] * 6 + [st],
        out_specs=[pl.BlockSpec((hd, bsz), lambda h: (h, 0)), st],
        out_shape=[jax.ShapeDtypeStruct((D_MODEL, bsz), F32), jax.ShapeDtypeStruct(s_t.shape, F32)],
        compiler_params=_cp(("parallel",)), name=name)(rt, kt, vt, kkt, bt, dt, s_t)


def _rwkv_step_out(yt, r, kh, v, g, r_k, gn_g, gn_b, name):
    def body(yt_ref, r_ref, k_ref, v_ref, g_ref, rk_ref, gg_ref, gb_ref, ones_ref, o_ref):
        ones = ones_ref[...]
        y = yt_ref[...].T
        yc = y - _dot3_r(y, ones) * (1.0 / RW_HD)
        yn = yc * lax.rsqrt(_dot3_r(yc * yc, ones) * (1.0 / RW_HD) + RW_GN_EPS) * gg_ref[...] + gb_ref[...]
        bonus = _dot3_r(r_ref[...] * k_ref[...] * rk_ref[...], ones) * v_ref[...]
        o_ref[...] = (yn + bonus) * g_ref[...]

    row = lambda w: w.reshape(1, D_MODEL)
    args = (yt, r, kh, v, g, row(r_k), row(gn_g), row(gn_b), _head_ones())
    return pl.pallas_call(
        body, grid=(1,), in_specs=[_full_spec(t) for t in args],
        out_specs=pl.BlockSpec(r.shape, lambda i: (0, 0)),
        out_shape=jax.ShapeDtypeStruct(r.shape, F32),
        compiler_params=_cp(("arbitrary",)), name=name)(*args)


def _rwkv_sample(x, shift0, s0, mu, w_rkv, w0, w_w1, w_w2, a0, w_a1, w_a2, w_g1, w_g2, k_k, k_a, r_k, gn_g, gn_b, w_o):
    m = x.shape[0]
    r, k, v, lw, a, g = _rwkv_proj(x, shift0, 1, m, mu, w_rkv, w0, w_w1, w_w2, a0, w_a1, w_a2, w_g1, w_g2, m,
                                   "rwkv_proj_s")
    rt, kt, vt, kkt, bt, dt, kh = _rwkv_step_prep(r, k, v, lw, a, k_k, k_a, "rwkv_prep_s")
    yt, s_t = _rwkv_step_lanes(rt, kt, vt, kkt, bt, dt, jnp.transpose(s0, (1, 2, 3, 0)), "rwkv_step_s")
    o = _rwkv_step_out(yt, r, kh, v, g, r_k, gn_g, gn_b, "rwkv_out_s")
    return _proj_mix(o, w_o), x, jnp.transpose(s_t, (3, 0, 1, 2))


def kernel(x_prompt, x_sample, state_gdn_conv, state_gdn, state_s5_re, state_s5_im, state_gla, state_rwkv_shift,
           state_rwkv, p_prompt, p_sample, gdn_w_in, gdn_conv_w, gdn_a_log, gdn_dt_bias, gdn_norm_g, gdn_w_out,
           s5_a_re, s5_a_im, s5_log_dt, s5_b_re, s5_b_im, s5_c_re, s5_c_im, s5_d, s5_w_o, s5_w_gate,
           gla_w_in, gla_w_gk2, gla_b_gk, gla_norm_g, gla_w_out,
           rwkv_mu, rwkv_w_rkv, rwkv_w0, rwkv_w_w1, rwkv_w_w2, rwkv_a0, rwkv_w_a1, rwkv_w_a2, rwkv_w_g1, rwkv_w_g2,
           rwkv_k_k, rwkv_k_a, rwkv_r_k, rwkv_ln_g, rwkv_ln_b, rwkv_w_o,
           ln_mix_g, ln_mix_b, ln_ffn_g, ln_ffn_b, mlp_w1, mlp_w2, ple_w, ple_gate_w):
    bsz, t, d = x_prompt.shape
    bs = x_sample.shape[0]
    gdn_w = (gdn_w_in[0], gdn_conv_w[0], gdn_a_log[0], gdn_dt_bias[0], gdn_norm_g[0], gdn_w_out[0])
    s5_w = (s5_a_re[0], s5_a_im[0], s5_log_dt[0], s5_b_re[0], s5_b_im[0], s5_c_re[0], s5_c_im[0], s5_d[0],
            s5_w_o[0], s5_w_gate[0])
    gla_w = (gla_w_in[0], gla_w_gk2[0], gla_b_gk[0], gla_norm_g[0], gla_w_out[0])
    rwkv_w = (rwkv_mu[0], rwkv_w_rkv[0], rwkv_w0[0], rwkv_w_w1[0], rwkv_w_w2[0], rwkv_a0[0], rwkv_w_a1[0],
              rwkv_w_a2[0], rwkv_w_g1[0], rwkv_w_g2[0], rwkv_k_k[0], rwkv_k_a[0], rwkv_r_k[0], rwkv_ln_g[0],
              rwkv_ln_b[0], rwkv_w_o[0])

    def tail(x, mix, p_all, i, tm, tag):
        return _block_tail(x, mix, ln_mix_g[i], ln_mix_b[i], p_all, i, mlp_w1[i], mlp_w2[i], ln_ffn_g[i], ln_ffn_b[i],
                           ple_w[i], ple_gate_w[i], tm, 1024, f"tail{i}_{tag}")

    xp = x_prompt.reshape(bsz * t, d)
    pp = p_prompt.reshape(DEPTH, bsz * t, D_PLE)
    tm_p = min(bsz * t, 512)
    mix, gc_p, gs_p = _gdn_prompt(xp, bsz, t, *gdn_w)
    xp = tail(xp, mix, pp, 0, tm_p, "p")
    mix, sr_p, si_p = _s5_prompt(xp, bsz, t, *s5_w)
    xp = tail(xp, mix, pp, 1, tm_p, "p")
    mix, la_p = _gla_prompt(xp, bsz, t, *gla_w)
    xp = tail(xp, mix, pp, 2, tm_p, "p")
    mix, sh_p, rs_p = _rwkv_prompt(xp, bsz, t, *rwkv_w)
    xp = tail(xp, mix, pp, 3, tm_p, "p")

    xs = x_sample.reshape(bs, d)
    ps = p_sample.reshape(DEPTH, bs, D_PLE)
    mix, gc_s, gs_s = _gdn_sample(xs, state_gdn_conv[0], state_gdn[0], *gdn_w)
    xs = tail(xs, mix, ps, 0, bs, "s")
    mix, sr_s, si_s = _s5_sample(xs, state_s5_re[0], state_s5_im[0], *s5_w)
    xs = tail(xs, mix, ps, 1, bs, "s")
    mix, la_s = _gla_sample(xs, state_gla[0], *gla_w)
    xs = tail(xs, mix, ps, 2, bs, "s")
    mix, sh_s, rs_s = _rwkv_sample(xs, state_rwkv_shift[0], state_rwkv[0], *rwkv_w)
    xs = tail(xs, mix, ps, 3, bs, "s")

    e = lambda a: a[None]
    return (xp.reshape(bsz, t, d), xs.reshape(bs, 1, d), e(gc_p), e(gc_s), e(gs_p), e(gs_s), e(sr_p), e(sr_s),
            e(si_p), e(si_s), e(la_p), e(la_s), e(sh_p), e(sh_s), e(rs_p), e(rs_s))
```
